```python
import math
import jax, jax.numpy as jnp
from jax import lax
import numpy as np


D_MODEL = 1024
BATCH = 4
SEQ = 8192
DEPTH = 1
DEC_BATCH = 8
DEC_SEQ = 16
PAST_LEN = 1024

CHUNK = 64
N_META = 16
ATTN_WIDTH = 512
CONV_WIDTH = 512
MIX_WIDTH = ATTN_WIDTH + CONV_WIDTH
N_HEADS_A = 4
HEAD_DIM = ATTN_WIDTH // (2 * N_HEADS_A)
V_DIM = 2 * HEAD_DIM
N_CONV_GROUPS = 8
CONV_W = 3
ROT_DIM = HEAD_DIM // 4
ROPE_THETA = 500000.0
Q_BLOCK = 128
EPS = 1e-6
IN_COLS = 4 * ATTN_WIDTH + 4 * CONV_WIDTH
SPLITS = (ATTN_WIDTH, 2 * ATTN_WIDTH, 3 * ATTN_WIDTH, 4 * ATTN_WIDTH,
          4 * ATTN_WIDTH + CONV_WIDTH, 4 * ATTN_WIDTH + 2 * CONV_WIDTH,
          4 * ATTN_WIDTH + 3 * CONV_WIDTH)

kernel_name = 'hymba_diffattn_shortconv_stream_step'


def rms_norm(x, g):
    xf = x.astype(jnp.float32)
    y = xf * lax.rsqrt(jnp.mean(xf * xf, axis=-1, keepdims=True) + EPS)
    return (y * g.astype(jnp.float32)).astype(x.dtype)


def lambda_init(layer_idx):
    return 0.8 - 0.6 * math.exp(-0.3 * layer_idx)


def rope_partial(t, pos):
    half = ROT_DIM // 2
    inv = ROPE_THETA ** (-jnp.arange(0, ROT_DIM, 2, dtype=jnp.float32) / ROT_DIM)
    ang = pos.astype(jnp.float32)[:, None] * inv[None, :]
    cos = jnp.cos(ang)[:, None, None, :]
    sin = jnp.sin(ang)[:, None, None, :]
    tr = t[..., :ROT_DIM].astype(jnp.float32)
    x1, x2 = tr[..., :half], tr[..., half:]
    rot = jnp.concatenate([x1 * cos - x2 * sin, x2 * cos + x1 * sin], axis=-1)
    return jnp.concatenate([rot.astype(t.dtype), t[..., ROT_DIM:]], axis=-1)


def project(h, ln_g, w_in, qn_g, kn_g, pos):
    B, T, _ = h.shape
    p = rms_norm(h, ln_g) @ w_in
    q, k, v, ga, bb, cc, hh, gc = jnp.split(p, SPLITS, axis=-1)
    q = q.reshape(B, T, N_HEADS_A, 2, HEAD_DIM)
    k = k.reshape(B, T, N_HEADS_A, 2, HEAD_DIM)
    v = v.reshape(B, T, N_HEADS_A, V_DIM)
    q = rope_partial(rms_norm(q, qn_g), pos)
    k = rope_partial(rms_norm(k, kn_g), pos)
    u = cc * hh
    return q, k, v, ga, bb, u, gc


def diff_attn(q, k, v, mask, lam):
    s = jnp.einsum('bqhmd,bkhmd->bhmqk', q, k).astype(jnp.float32) * (HEAD_DIM ** -0.5)
    if mask is not None:
        s = jnp.where(mask, s, jnp.finfo(jnp.float32).min)
    p = jax.nn.softmax(s, axis=-1)
    a = p[:, :, 0] - lam * p[:, :, 1]
    return jnp.einsum('bhqk,bkhe->bqhe', a.astype(v.dtype), v)


def prompt_attention(q, k_all, v_all, lam):
    B, S = q.shape[0], q.shape[1]
    nb = S // Q_BLOCK
    key_chunk = jnp.concatenate([-jnp.ones((N_META,), jnp.int32),
                                 jnp.arange(S, dtype=jnp.int32) // CHUNK])
    qb = q.reshape(B, nb, Q_BLOCK, N_HEADS_A, 2, HEAD_DIM).transpose(1, 0, 2, 3, 4, 5)

    def one_block(args):
        q_blk, j = args
        q_chunk = (j * Q_BLOCK + jnp.arange(Q_BLOCK, dtype=jnp.int32)) // CHUNK
        mask = key_chunk[None, :] <= q_chunk[:, None]
        return diff_attn(q_blk, k_all, v_all, mask, lam)

    o = lax.map(one_block, (qb, jnp.arange(nb, dtype=jnp.int32)))
    return o.transpose(1, 0, 2, 3, 4).reshape(B, S, N_HEADS_A, V_DIM)


def causal_conv(u, prev, w, b):
    T = u.shape[1]
    up = jnp.concatenate([prev.astype(u.dtype), u], axis=1)
    y = b
    for j in range(CONV_W):
        y = y + w[j] * up[:, j:j + T]
    return y, up[:, -(CONV_W - 1):]


def finish(h, o_attn, sub_g, li, ga, conv_y, bb, gc, w_out):
    B, T = h.shape[0], h.shape[1]
    oa = (rms_norm(o_attn, sub_g) * (1.0 - li)).reshape(B, T, ATTN_WIDTH)
    oa = jax.nn.silu(ga) * oa
    oc = jax.nn.silu(gc) * (bb * conv_y)
    return h + jnp.concatenate([oa, oc], axis=-1) @ w_out


def setup_inputs(seed: int = 0) -> dict:
    key = jax.random.key(seed)
    ks = jax.random.split(key, 18)
    nrm = jax.random.normal
    f = jnp.float32
    return {
        'x_prompt': nrm(ks[0], (BATCH, SEQ, D_MODEL), f),
        'x_sample': nrm(ks[1], (DEC_BATCH, DEC_SEQ, D_MODEL), f),
        'cache_k': nrm(ks[2], (DEPTH, DEC_BATCH, PAST_LEN, N_HEADS_A, 2 * HEAD_DIM), f),
        'cache_v': nrm(ks[3], (DEPTH, DEC_BATCH, PAST_LEN, N_HEADS_A, V_DIM), f),
        'state_conv': nrm(ks[4], (DEPTH, DEC_BATCH, CONV_W - 1, CONV_WIDTH), f),
        'meta_tokens': nrm(ks[5], (N_META, D_MODEL), f),
        'ln_g': 1.0 + 0.01 * nrm(ks[6], (DEPTH, D_MODEL), f),
        'w_in': nrm(ks[7], (DEPTH, D_MODEL, IN_COLS), f) * D_MODEL ** -0.5,
        'q_norm_g': 1.0 + 0.01 * nrm(ks[8], (DEPTH, HEAD_DIM), f),
        'k_norm_g': 1.0 + 0.01 * nrm(ks[9], (DEPTH, HEAD_DIM), f),
        'lam_q1': 0.1 * nrm(ks[10], (DEPTH, HEAD_DIM), f),
        'lam_k1': 0.1 * nrm(ks[11], (DEPTH, HEAD_DIM), f),
        'lam_q2': 0.1 * nrm(ks[12], (DEPTH, HEAD_DIM), f),
        'lam_k2': 0.1 * nrm(ks[13], (DEPTH, HEAD_DIM), f),
        'subln_g': 1.0 + 0.01 * nrm(ks[14], (DEPTH, V_DIM), f),
        'conv_w': nrm(ks[15], (DEPTH, CONV_W, CONV_WIDTH), f) * CONV_W ** -0.5,
        'conv_b': 0.01 * nrm(ks[16], (DEPTH, CONV_WIDTH), f),
        'w_out': nrm(ks[17], (DEPTH, MIX_WIDTH, D_MODEL), f) * MIX_WIDTH ** -0.5,
    }


def reference(x_prompt, x_sample, cache_k, cache_v, state_conv, meta_tokens, ln_g, w_in,
              q_norm_g, k_norm_g, lam_q1, lam_k1, lam_q2, lam_k2, subln_g, conv_w,
              conv_b, w_out):
    B, S = x_prompt.shape[0], x_prompt.shape[1]
    DB, T = x_sample.shape[0], x_sample.shape[1]
    P = cache_k.shape[2]
    pos_m = jnp.arange(N_META, dtype=jnp.int32)
    pos_p = N_META + jnp.arange(S, dtype=jnp.int32)
    pos_s = N_META + P + jnp.arange(T, dtype=jnp.int32)

    hm = meta_tokens[None].astype(x_prompt.dtype)
    hp = x_prompt
    hs = x_sample
    kp_l, vp_l, cp_l, ks_l, vs_l, cs_l = [], [], [], [], [], []
    for l in range(DEPTH):
        li = lambda_init(l)
        lam = (jnp.exp(jnp.sum(lam_q1[l].astype(jnp.float32) * lam_k1[l].astype(jnp.float32)))
               - jnp.exp(jnp.sum(lam_q2[l].astype(jnp.float32) * lam_k2[l].astype(jnp.float32)))
               + li)
        qm, km, vm, gam, bm, um, gcm = project(hm, ln_g[l], w_in[l], q_norm_g[l], k_norm_g[l], pos_m)
        qp, kp, vp, gap, bp, up, gcp = project(hp, ln_g[l], w_in[l], q_norm_g[l], k_norm_g[l], pos_p)
        qs, ks, vs, gas, bs, us, gcs = project(hs, ln_g[l], w_in[l], q_norm_g[l], k_norm_g[l], pos_s)
        meta_tail = um[:, -(CONV_W - 1):]

        kp_all = jnp.concatenate([jnp.broadcast_to(km, (B,) + km.shape[1:]), kp], axis=1)
        vp_all = jnp.concatenate([jnp.broadcast_to(vm, (B,) + vm.shape[1:]), vp], axis=1)
        op = prompt_attention(qp, kp_all, vp_all, lam)
        yp, tail_p = causal_conv(up, jnp.broadcast_to(meta_tail, (B, CONV_W - 1, CONV_WIDTH)),
                                 conv_w[l], conv_b[l])

        ck = cache_k[l].reshape(DB, P, N_HEADS_A, 2, HEAD_DIM).astype(ks.dtype)
        ks_all = jnp.concatenate([jnp.broadcast_to(km, (DB,) + km.shape[1:]), ck, ks], axis=1)
        vs_all = jnp.concatenate([jnp.broadcast_to(vm, (DB,) + vm.shape[1:]),
                                  cache_v[l].astype(vs.dtype), vs], axis=1)
        os_ = diff_attn(qs, ks_all, vs_all, None, lam)
        ys, tail_s = causal_conv(us, state_conv[l], conv_w[l], conv_b[l])

        kp_l.append(kp_all.reshape(B, N_META + S, N_HEADS_A, 2 * HEAD_DIM))
        vp_l.append(vp_all)
        cp_l.append(tail_p)
        ks_l.append(ks.reshape(DB, T, N_HEADS_A, 2 * HEAD_DIM))
        vs_l.append(vs)
        cs_l.append(tail_s)

        if l < DEPTH - 1:
            om = diff_attn(qm, km, vm, None, lam)
            ym, _ = causal_conv(um, jnp.zeros((1, CONV_W - 1, CONV_WIDTH), um.dtype),
                                conv_w[l], conv_b[l])
            hm = finish(hm, om, subln_g[l], li, gam, ym, bm, gcm, w_out[l])
        hp = finish(hp, op, subln_g[l], li, gap, yp, bp, gcp, w_out[l])
        hs = finish(hs, os_, subln_g[l], li, gas, ys, bs, gcs, w_out[l])

    return (hp, hs, jnp.stack(kp_l), jnp.stack(vp_l), jnp.stack(cp_l),
            jnp.stack(ks_l), jnp.stack(vs_l), jnp.stack(cs_l))
```

```python
import functools
import math

import jax
import jax.numpy as jnp
from jax import lax
from jax.experimental import pallas as pl
from jax.experimental.pallas import tpu as pltpu

F32 = jnp.float32
BF16 = jnp.bfloat16

CHUNK = 64
N_META = 16
N_HEADS = 4
HEAD_DIM = 64
V_DIM = 2 * HEAD_DIM
WIDTH = N_HEADS * V_DIM
CONV_W = 3
ROT_DIM = HEAD_DIM // 4
ROPE_THETA = 500000.0
EPS = 1e-6
LAMBDA_INIT = 0.8 - 0.6 * math.exp(-0.3 * 0)
Q_SCALE = HEAD_DIM ** -0.5 * math.log2(math.e)
NEG_BIG = -1e30
STATE_ROWS = 8

V7X_VMEM_LIMIT = 56 * 1024 * 1024


def _silu(x):
    return x / (1.0 + jnp.exp(-x))


def _proj_kernel(x_ref, st_ref, cos_ref, sin_ref, lng_ref, win_ref, qg_ref, kg_ref, cw_ref, cb_ref,
                 gmat_ref, q_ref, kf_ref, vf_ref, kb_ref, sga_ref, oc_ref, tail_ref, *rest,
                 tm, tkv, emit_vt):
    if emit_vt:
        vt_ref, ubuf = rest
    else:
        (ubuf,) = rest
    j = pl.program_id(1)

    x = x_ref[0]
    ms = jnp.mean(x * x, axis=-1, keepdims=True)
    xn = (x * lax.rsqrt(ms + EPS) * lng_ref[...]).astype(BF16)

    def proj(i):
        return jnp.dot(xn, win_ref[:, i * WIDTH:(i + 1) * WIDTH], preferred_element_type=F32)

    cos = cos_ref[...]
    sin = sin_ref[...]
    lane = lax.broadcasted_iota(jnp.int32, (1, V_DIM), 1) % HEAD_DIM
    take_lower = lane >= ROT_DIM // 2

    def norm_rope(t, g):
        ss = jnp.dot((t * t).astype(BF16), gmat_ref[...], preferred_element_type=F32)
        tn = t * lax.rsqrt(ss * (1.0 / HEAD_DIM) + EPS) * g
        outs = []
        for h in range(N_HEADS):
            th = tn[:, h * V_DIM:(h + 1) * V_DIM]
            lower = pltpu.roll(th, ROT_DIM // 2, axis=1)
            upper = pltpu.roll(th, V_DIM - ROT_DIM // 2, axis=1)
            outs.append(th * cos + jnp.where(take_lower, lower, upper) * sin)
        return jnp.concatenate(outs, axis=1)

    q = norm_rope(proj(0), qg_ref[...])
    q_ref[0] = (q * Q_SCALE).astype(BF16)
    k = norm_rope(proj(1), kg_ref[...])
    kf_ref[0] = k
    kb_ref[0] = k.astype(BF16)
    v = proj(2)
    vf_ref[0] = v
    if emit_vt:
        for s in range(tm // tkv):
            vt_ref[0, s] = v[s * tkv:(s + 1) * tkv, :].T.astype(BF16)
    sga_ref[0] = _silu(proj(3)).astype(BF16)

    bb = proj(4)
    u = proj(5) * proj(6)

    @pl.when(j == 0)
    def _():
        ubuf[0:STATE_ROWS, :] = st_ref[0]

    @pl.when(j > 0)
    def _():
        ubuf[0:STATE_ROWS, :] = ubuf[tm:tm + STATE_ROWS, :]

    ubuf[STATE_ROWS:STATE_ROWS + tm, :] = u
    y = cb_ref[...]
    for tap in range(CONV_W):
        off = STATE_ROWS - (CONV_W - 1) + tap
        y = y + cw_ref[tap:tap + 1, :] * ubuf[off:off + tm, :]
    oc_ref[0] = (_silu(proj(7)) * (bb * y)).astype(BF16)
    tail_ref[0] = ubuf[tm:tm + STATE_ROWS, :]


def _proj_call(x, state8, cos_t, sin_t, ln_g, w_in_bf, qg, kg, conv_w, conv_b, gmat, *, tm, tkv,
               emit_vt):
    nb, t, d = x.shape
    nt = t // tm
    assert t % tm == 0 and tm % tkv == 0
    row = lambda b, j: (b, j, 0)
    const2 = lambda b, j: (0, 0)
    in_specs = [
        pl.BlockSpec((1, tm, d), row),
        pl.BlockSpec((1, STATE_ROWS, WIDTH), lambda b, j: (b, 0, 0)),
        pl.BlockSpec((tm, V_DIM), lambda b, j: (j, 0)),
        pl.BlockSpec((tm, V_DIM), lambda b, j: (j, 0)),
        pl.BlockSpec((1, d), const2),
        pl.BlockSpec(w_in_bf.shape, const2),
        pl.BlockSpec((1, WIDTH), const2),
        pl.BlockSpec((1, WIDTH), const2),
        pl.BlockSpec((CONV_W, WIDTH), const2),
        pl.BlockSpec((1, WIDTH), const2),
        pl.BlockSpec((WIDTH, WIDTH), const2),
    ]
    tile = pl.BlockSpec((1, tm, WIDTH), row)
    out_shape = [
        jax.ShapeDtypeStruct((nb, t, WIDTH), BF16),
        jax.ShapeDtypeStruct((nb, t, WIDTH), F32),
        jax.ShapeDtypeStruct((nb, t, WIDTH), F32),
        jax.ShapeDtypeStruct((nb, t, WIDTH), BF16),
        jax.ShapeDtypeStruct((nb, t, WIDTH), BF16),
        jax.ShapeDtypeStruct((nb, t, WIDTH), BF16),
        jax.ShapeDtypeStruct((nb, STATE_ROWS, WIDTH), F32),
    ]
    out_specs = [tile, tile, tile, tile, tile, tile,
                 pl.BlockSpec((1, STATE_ROWS, WIDTH), lambda b, j: (b, 0, 0))]
    if emit_vt:
        out_shape.append(jax.ShapeDtypeStruct((nb, t // tkv, WIDTH, tkv), BF16))
        out_specs.append(pl.BlockSpec((1, tm // tkv, WIDTH, tkv), lambda b, j: (b, j, 0, 0)))
    return pl.pallas_call(
        functools.partial(_proj_kernel, tm=tm, tkv=tkv, emit_vt=emit_vt),
        grid=(nb, nt),
        in_specs=in_specs,
        out_specs=out_specs,
        out_shape=out_shape,
        scratch_shapes=[pltpu.VMEM((tm + STATE_ROWS, WIDTH), F32)],
        compiler_params=pltpu.CompilerParams(
            dimension_semantics=("arbitrary", "arbitrary"),
            vmem_limit_bytes=V7X_VMEM_LIMIT),
        name="proj",
    )(x, state8, cos_t, sin_t, ln_g, w_in_bf, qg, kg, conv_w, conv_b, gmat)


def _lambda(lq1_ref, lk1_ref, lq2_ref, lk2_ref):
    a = jnp.sum(lq1_ref[...] * lk1_ref[...], axis=-1, keepdims=True)
    b = jnp.sum(lq2_ref[...] * lk2_ref[...], axis=-1, keepdims=True)
    return jnp.exp(a) - jnp.exp(b) + LAMBDA_INIT


def _finish(x, o_heads, sga, oc, subg, wout_ref):
    parts = []
    for h, o in enumerate(o_heads):
        ms = jnp.mean(o * o, axis=-1, keepdims=True)
        on = o * lax.rsqrt(ms + EPS) * subg * (1.0 - LAMBDA_INIT)
        parts.append((sga[:, h * V_DIM:(h + 1) * V_DIM].astype(F32) * on).astype(BF16))
    parts.append(oc)
    mix = jnp.concatenate(parts, axis=1)
    return x + jnp.dot(mix, wout_ref[...], preferred_element_type=F32)


def _prompt_attn_kernel(x_ref, q_ref, kb_ref, vt_ref, km_ref, vmt_ref, sga_ref, oc_ref, subg_ref,
                        lq1_ref, lk1_ref, lq2_ref, lk2_ref, wout_ref, y_ref,
                        qz_ref, m_ref, l_ref, acc_ref, *, tq):
    j = pl.program_id(1)
    n_chain = 2 * N_HEADS

    lane = lax.broadcasted_iota(jnp.int32, (1, V_DIM), 1)
    for h in range(N_HEADS):
        qh = q_ref[0, :, h * V_DIM:(h + 1) * V_DIM]
        qz_ref[2 * h] = jnp.where(lane < HEAD_DIM, qh, jnp.zeros_like(qh))
        qz_ref[2 * h + 1] = jnp.where(lane >= HEAD_DIM, qh, jnp.zeros_like(qh))

    def scores(k_h, c):
        return lax.dot_general(k_h, qz_ref[c], (((1,), (1,)), ((), ())), preferred_element_type=F32)

    for c in range(n_chain):
        h = c // 2
        s = scores(km_ref[:, h * V_DIM:(h + 1) * V_DIM], c)
        m = jnp.max(s, axis=0, keepdims=True)
        e = jnp.exp2(s - m)
        m_ref[c:c + 1, :] = m
        l_ref[c:c + 1, :] = jnp.sum(e, axis=0, keepdims=True)
        acc_ref[c] = jnp.dot(vmt_ref[h * V_DIM:(h + 1) * V_DIM, :], e.astype(BF16),
                             preferred_element_type=F32)

    def block(i, mask):
        kblk = kb_ref[0, i]
        vtblk = vt_ref[0, i]
        for c in range(n_chain):
            h = c // 2
            s = scores(kblk[:, h * V_DIM:(h + 1) * V_DIM], c)
            if mask is not None:
                s = jnp.where(mask, s, NEG_BIG)
            m_old = m_ref[c:c + 1, :]
            m_new = jnp.maximum(m_old, jnp.max(s, axis=0, keepdims=True))
            alpha = jnp.exp2(m_old - m_new)
            e = jnp.exp2(s - m_new)
            l_ref[c:c + 1, :] = alpha * l_ref[c:c + 1, :] + jnp.sum(e, axis=0, keepdims=True)
            pv = jnp.dot(vtblk[h * V_DIM:(h + 1) * V_DIM, :], e.astype(BF16),
                         preferred_element_type=F32)
            acc_ref[c] = alpha * acc_ref[c] + pv
            m_ref[c:c + 1, :] = m_new

    def full_block(i, carry):
        block(i, None)
        return carry

    lax.fori_loop(0, j, full_block, 0)

    kchunk = lax.broadcasted_iota(jnp.int32, (tq, 1), 0) // CHUNK
    qchunk = lax.broadcasted_iota(jnp.int32, (1, tq), 1) // CHUNK
    block(j, kchunk <= qchunk)

    lam = _lambda(lq1_ref, lk1_ref, lq2_ref, lk2_ref)
    o_heads = []
    for h in range(N_HEADS):
        o_t = (acc_ref[2 * h] / l_ref[2 * h:2 * h + 1, :]
               - lam * (acc_ref[2 * h + 1] / l_ref[2 * h + 1:2 * h + 2, :]))
        o_heads.append(o_t.T)
    y_ref[0] = _finish(x_ref[0], o_heads, sga_ref[0], oc_ref[0], subg_ref[...], wout_ref)


def _prompt_attn(x, q, kb4, vt4, km_bf, vmt_bf, sga, oc, subg, lq1, lk1, lq2, lk2, w_out_bf, *, tq):
    nb, s, d = x.shape
    nq = s // tq
    row = lambda b, j: (b, j, 0)
    const2 = lambda b, j: (0, 0)
    whole = lambda b, j: (b, 0, 0, 0)
    in_specs = [
        pl.BlockSpec((1, tq, d), row),
        pl.BlockSpec((1, tq, WIDTH), row),
        pl.BlockSpec((1, nq, tq, WIDTH), whole),
        pl.BlockSpec((1, nq, WIDTH, tq), whole),
        pl.BlockSpec((N_META, WIDTH), const2),
        pl.BlockSpec((WIDTH, N_META), const2),
        pl.BlockSpec((1, tq, WIDTH), row),
        pl.BlockSpec((1, tq, WIDTH), row),
        pl.BlockSpec((1, V_DIM), const2),
        pl.BlockSpec((1, HEAD_DIM), const2),
        pl.BlockSpec((1, HEAD_DIM), const2),
        pl.BlockSpec((1, HEAD_DIM), const2),
        pl.BlockSpec((1, HEAD_DIM), const2),
        pl.BlockSpec(w_out_bf.shape, const2),
    ]
    n_chain = 2 * N_HEADS
    return pl.pallas_call(
        functools.partial(_prompt_attn_kernel, tq=tq),
        grid=(nb, nq),
        in_specs=in_specs,
        out_specs=pl.BlockSpec((1, tq, d), row),
        out_shape=jax.ShapeDtypeStruct((nb, s, d), F32),
        scratch_shapes=[
            pltpu.VMEM((n_chain, tq, V_DIM), BF16),
            pltpu.VMEM((n_chain, tq), F32),
            pltpu.VMEM((n_chain, tq), F32),
            pltpu.VMEM((n_chain, V_DIM, tq), F32),
        ],
        compiler_params=pltpu.CompilerParams(
            dimension_semantics=("arbitrary", "arbitrary"),
            vmem_limit_bytes=V7X_VMEM_LIMIT),
        name="prompt_attn",
    )(x, q, kb4, vt4, km_bf, vmt_bf, sga, oc, subg, lq1, lk1, lq2, lk2, w_out_bf)


def _sample_attn_kernel(x_ref, q_ref, kn_ref, vn_ref, ck_ref, cv_ref, km_ref, vm_ref, sga_ref,
                        oc_ref, subg_ref, lq1_ref, lk1_ref, lq2_ref, lk2_ref, wout_ref, y_ref,
                        kcat, vcat, *, past, t_new, n_keys_pad):
    n_keys = past + N_META + t_new
    kcat[0:past, :] = ck_ref[0].astype(BF16)
    kcat[past:past + N_META, :] = km_ref[...]
    kcat[past + N_META:n_keys, :] = kn_ref[0]
    kcat[n_keys:n_keys_pad, :] = jnp.zeros((n_keys_pad - n_keys, WIDTH), BF16)
    vcat[0:past, :] = cv_ref[0].astype(BF16)
    vcat[past:past + N_META, :] = vm_ref[...].astype(BF16)
    vcat[past + N_META:n_keys, :] = vn_ref[0].astype(BF16)
    vcat[n_keys:n_keys_pad, :] = jnp.zeros((n_keys_pad - n_keys, WIDTH), BF16)

    lane = lax.broadcasted_iota(jnp.int32, (1, V_DIM), 1)
    valid = lax.broadcasted_iota(jnp.int32, (1, n_keys_pad), 1) < n_keys
    lam = _lambda(lq1_ref, lk1_ref, lq2_ref, lk2_ref)
    o_heads = []
    for h in range(N_HEADS):
        qh = q_ref[0, :, h * V_DIM:(h + 1) * V_DIM]
        k_h = kcat[:, h * V_DIM:(h + 1) * V_DIM]
        v_h = vcat[:, h * V_DIM:(h + 1) * V_DIM]
        outs = []
        for comp in range(2):
            keep = (lane < HEAD_DIM) if comp == 0 else (lane >= HEAD_DIM)
            qc = jnp.where(keep, qh, jnp.zeros_like(qh))
            s = lax.dot_general(qc, k_h, (((1,), (1,)), ((), ())), preferred_element_type=F32)
            s = jnp.where(valid, s, NEG_BIG)
            m = jnp.max(s, axis=-1, keepdims=True)
            e = jnp.exp2(s - m)
            l = jnp.sum(e, axis=-1, keepdims=True)
            outs.append(jnp.dot(e.astype(BF16), v_h, preferred_element_type=F32) / l)
        o_heads.append(outs[0] - lam * outs[1])
    y_ref[0] = _finish(x_ref[0], o_heads, sga_ref[0], oc_ref[0], subg_ref[...], wout_ref)


def _sample_attn(x, q, kn_bf, vn, ck, cv, km_bf, vm, sga, oc, subg, lq1, lk1, lq2, lk2, w_out_bf):
    nb, t_new, d = x.shape
    past = ck.shape[1]
    n_keys_pad = -(-(past + N_META + t_new) // V_DIM) * V_DIM
    row = lambda b: (b, 0, 0)
    const2 = lambda b: (0, 0)
    in_specs = [
        pl.BlockSpec((1, t_new, d), row),
        pl.BlockSpec((1, t_new, WIDTH), row),
        pl.BlockSpec((1, t_new, WIDTH), row),
        pl.BlockSpec((1, t_new, WIDTH), row),
        pl.BlockSpec((1, past, WIDTH), row),
        pl.BlockSpec((1, past, WIDTH), row),
        pl.BlockSpec((N_META, WIDTH), const2),
        pl.BlockSpec((N_META, WIDTH), const2),
        pl.BlockSpec((1, t_new, WIDTH), row),
        pl.BlockSpec((1, t_new, WIDTH), row),
        pl.BlockSpec((1, V_DIM), const2),
        pl.BlockSpec((1, HEAD_DIM), const2),
        pl.BlockSpec((1, HEAD_DIM), const2),
        pl.BlockSpec((1, HEAD_DIM), const2),
        pl.BlockSpec((1, HEAD_DIM), const2),
        pl.BlockSpec(w_out_bf.shape, const2),
    ]
    return pl.pallas_call(
        functools.partial(_sample_attn_kernel, past=past, t_new=t_new, n_keys_pad=n_keys_pad),
        grid=(nb,),
        in_specs=in_specs,
        out_specs=pl.BlockSpec((1, t_new, d), row),
        out_shape=jax.ShapeDtypeStruct((nb, t_new, d), F32),
        scratch_shapes=[pltpu.VMEM((n_keys_pad, WIDTH), BF16),
                        pltpu.VMEM((n_keys_pad, WIDTH), BF16)],
        compiler_params=pltpu.CompilerParams(
            dimension_semantics=("arbitrary",),
            vmem_limit_bytes=V7X_VMEM_LIMIT),
        name="sample_attn",
    )(x, q, kn_bf, vn, ck, cv, km_bf, vm, sga, oc, subg, lq1, lk1, lq2, lk2, w_out_bf)


def _rope_tables(pos):
    half = ROT_DIM // 2
    inv = ROPE_THETA ** (-jnp.arange(0, ROT_DIM, 2, dtype=F32) / ROT_DIM)
    ang = pos.astype(F32)[:, None] * inv[None, :]
    cos, sin = jnp.cos(ang), jnp.sin(ang)
    t = pos.shape[0]
    rest = HEAD_DIM - 2 * half
    c64 = jnp.concatenate([cos, cos, jnp.ones((t, rest), F32)], axis=1)
    s64 = jnp.concatenate([-sin, sin, jnp.zeros((t, rest), F32)], axis=1)
    return jnp.tile(c64, (1, 2)), jnp.tile(s64, (1, 2))


def _pad_state(state):
    return jnp.pad(state, ((0, 0), (STATE_ROWS - (CONV_W - 1), 0), (0, 0)))


def _tile_sizes(seq):
    tq = 256 if seq % 256 == 0 else seq
    return tq, tq


def kernel(x_prompt, x_sample, cache_k, cache_v, state_conv, meta_tokens, ln_g, w_in, q_norm_g,
           k_norm_g, lam_q1, lam_k1, lam_q2, lam_k2, subln_g, conv_w, conv_b, w_out):
    assert ln_g.shape[0] == 1, "single-layer stack"
    nb, seq, d = x_prompt.shape
    db, t_new, _ = x_sample.shape
    past = cache_k.shape[2]

    w_in_bf = w_in[0].astype(BF16)
    w_out_bf = w_out[0].astype(BF16)
    lng = ln_g[0][None, :]
    qg = jnp.tile(q_norm_g[0], 2 * N_HEADS)[None, :]
    kg = jnp.tile(k_norm_g[0], 2 * N_HEADS)[None, :]
    subg = subln_g[0][None, :]
    cw, cb = conv_w[0], conv_b[0][None, :]
    lams = [p[0][None, :] for p in (lam_q1, lam_k1, lam_q2, lam_k2)]
    grp = jnp.arange(WIDTH, dtype=jnp.int32) // HEAD_DIM
    gmat = (grp[:, None] == grp[None, :]).astype(BF16)

    pos_m = jnp.arange(N_META, dtype=jnp.int32)
    pos_p = N_META + jnp.arange(seq, dtype=jnp.int32)
    pos_s = N_META + past + jnp.arange(t_new, dtype=jnp.int32)
    weights = (lng, w_in_bf, qg, kg, cw, cb, gmat)
    tm, tq = _tile_sizes(seq)

    zero_state = jnp.zeros((1, STATE_ROWS, WIDTH), F32)
    _, km, vm, km_bf, _, _, tail_m = _proj_call(
        meta_tokens[None], zero_state, *_rope_tables(pos_m), *weights,
        tm=N_META, tkv=N_META, emit_vt=False)
    km, vm, km_bf = km[0], vm[0], km_bf[0]

    state_p = jnp.broadcast_to(tail_m, (nb, STATE_ROWS, WIDTH))
    qp, kp, vp, kp_bf, sga_p, oc_p, tail_p, vtp = _proj_call(
        x_prompt, state_p, *_rope_tables(pos_p), *weights, tm=tm, tkv=tq, emit_vt=True)
    y_prompt = _prompt_attn(
        x_prompt, qp, kp_bf.reshape(nb, seq // tq, tq, WIDTH), vtp, km_bf, vm.T.astype(BF16),
        sga_p, oc_p, subg, *lams, w_out_bf, tq=tq)

    qs, ks, vs, ks_bf, sga_s, oc_s, tail_s = _proj_call(
        x_sample, _pad_state(state_conv[0]), *_rope_tables(pos_s), *weights,
        tm=t_new, tkv=t_new, emit_vt=False)
    y_sample = _sample_attn(
        x_sample, qs, ks_bf, vs, cache_k[0].reshape(db, past, WIDTH),
        cache_v[0].reshape(db, past, WIDTH), km_bf, vm, sga_s, oc_s, subg, *lams, w_out_bf)

    def with_meta(meta, body):
        full = jnp.concatenate([jnp.broadcast_to(meta[None], (nb,) + meta.shape), body], axis=1)
        return full.reshape(1, nb, N_META + seq, N_HEADS, V_DIM)

    tail_rows = slice(STATE_ROWS - (CONV_W - 1), STATE_ROWS)
    return (y_prompt, y_sample, with_meta(km, kp), with_meta(vm, vp), tail_p[None, :, tail_rows],
            ks.reshape(1, db, t_new, N_HEADS, V_DIM), vs.reshape(1, db, t_new, N_HEADS, V_DIM),
            tail_s[None, :, tail_rows])
```

```python
import functools
import math

import jax
import jax.numpy as jnp
from jax import lax
from jax.experimental import pallas as pl
from jax.experimental.pallas import tpu as pltpu

F32 = jnp.float32
BF16 = jnp.bfloat16

CHUNK = 64
N_META = 16
N_HEADS = 4
HEAD_DIM = 64
V_DIM = 2 * HEAD_DIM
WIDTH = N_HEADS * V_DIM
CONV_W = 3
ROT_DIM = HEAD_DIM // 4
ROPE_THETA = 500000.0
EPS = 1e-6
LAMBDA_INIT = 0.8 - 0.6 * math.exp(-0.3 * 0)
Q_SCALE = HEAD_DIM ** -0.5 * math.log2(math.e)
NEG_BIG = -1e30
SAFE_SCORE_BOUND = 40.0
BIG_KEY_BLOCKS = 4
STATE_ROWS = 8

V7X_VMEM_LIMIT = 56 * 1024 * 1024


def _silu(x):
    return x / (1.0 + jnp.exp(-x))


def _proj_kernel(x_ref, st_ref, cos_ref, sin_ref, lng_ref, win_ref, qg_ref, kg_ref, cw_ref, cb_ref,
                 gmat_ref, q_ref, kf_ref, vf_ref, kb_ref, sga_ref, oc_ref, tail_ref, *rest,
                 tm, tkv, emit_vt):
    if emit_vt:
        vt_ref, ubuf = rest
    else:
        (ubuf,) = rest
    j = pl.program_id(1)

    x = x_ref[0]
    ms = jnp.mean(x * x, axis=-1, keepdims=True)
    xn = (x * lax.rsqrt(ms + EPS) * lng_ref[...]).astype(BF16)

    def proj(i):
        return jnp.dot(xn, win_ref[:, i * WIDTH:(i + 1) * WIDTH], preferred_element_type=F32)

    cos = cos_ref[...]
    sin = sin_ref[...]
    lane = lax.broadcasted_iota(jnp.int32, (1, V_DIM), 1) % HEAD_DIM
    take_lower = lane >= ROT_DIM // 2

    def norm_rope(t, g):
        ss = jnp.dot((t * t).astype(BF16), gmat_ref[...], preferred_element_type=F32)
        tn = t * lax.rsqrt(ss * (1.0 / HEAD_DIM) + EPS) * g
        outs = []
        for h in range(N_HEADS):
            th = tn[:, h * V_DIM:(h + 1) * V_DIM]
            lower = pltpu.roll(th, ROT_DIM // 2, axis=1)
            upper = pltpu.roll(th, V_DIM - ROT_DIM // 2, axis=1)
            outs.append(th * cos + jnp.where(take_lower, lower, upper) * sin)
        return jnp.concatenate(outs, axis=1)

    q = norm_rope(proj(0), qg_ref[...])
    q_ref[0] = (q * Q_SCALE).astype(BF16)
    k = norm_rope(proj(1), kg_ref[...])
    kf_ref[0] = k
    kb_ref[0] = k.astype(BF16)
    v = proj(2)
    vf_ref[0] = v
    if emit_vt:
        for s in range(tm // tkv):
            vt_ref[0, s] = v[s * tkv:(s + 1) * tkv, :].T.astype(BF16)
    sga_ref[0] = _silu(proj(3)).astype(BF16)

    bb = proj(4)
    u = proj(5) * proj(6)

    @pl.when(j == 0)
    def _():
        ubuf[0:STATE_ROWS, :] = st_ref[0]

    @pl.when(j > 0)
    def _():
        ubuf[0:STATE_ROWS, :] = ubuf[tm:tm + STATE_ROWS, :]

    ubuf[STATE_ROWS:STATE_ROWS + tm, :] = u
    y = cb_ref[...]
    for tap in range(CONV_W):
        off = STATE_ROWS - (CONV_W - 1) + tap
        y = y + cw_ref[tap:tap + 1, :] * ubuf[off:off + tm, :]
    oc_ref[0] = (_silu(proj(7)) * (bb * y)).astype(BF16)
    tail_ref[0] = ubuf[tm:tm + STATE_ROWS, :]


def _proj_call(x, state8, cos_t, sin_t, ln_g, w_in_bf, qg, kg, conv_w, conv_b, gmat, *, tm, tkv,
               emit_vt):
    nb, t, d = x.shape
    nt = t // tm
    assert t % tm == 0 and tm % tkv == 0
    row = lambda b, j: (b, j, 0)
    const2 = lambda b, j: (0, 0)
    in_specs = [
        pl.BlockSpec((1, tm, d), row),
        pl.BlockSpec((1, STATE_ROWS, WIDTH), lambda b, j: (b, 0, 0)),
        pl.BlockSpec((tm, V_DIM), lambda b, j: (j, 0)),
        pl.BlockSpec((tm, V_DIM), lambda b, j: (j, 0)),
        pl.BlockSpec((1, d), const2),
        pl.BlockSpec(w_in_bf.shape, const2),
        pl.BlockSpec((1, WIDTH), const2),
        pl.BlockSpec((1, WIDTH), const2),
        pl.BlockSpec((CONV_W, WIDTH), const2),
        pl.BlockSpec((1, WIDTH), const2),
        pl.BlockSpec((WIDTH, WIDTH), const2),
    ]
    tile = pl.BlockSpec((1, tm, WIDTH), row)
    out_shape = [
        jax.ShapeDtypeStruct((nb, t, WIDTH), BF16),
        jax.ShapeDtypeStruct((nb, t, WIDTH), F32),
        jax.ShapeDtypeStruct((nb, t, WIDTH), F32),
        jax.ShapeDtypeStruct((nb, t, WIDTH), BF16),
        jax.ShapeDtypeStruct((nb, t, WIDTH), BF16),
        jax.ShapeDtypeStruct((nb, t, WIDTH), BF16),
        jax.ShapeDtypeStruct((nb, STATE_ROWS, WIDTH), F32),
    ]
    out_specs = [tile, tile, tile, tile, tile, tile,
                 pl.BlockSpec((1, STATE_ROWS, WIDTH), lambda b, j: (b, 0, 0))]
    if emit_vt:
        out_shape.append(jax.ShapeDtypeStruct((nb, t // tkv, WIDTH, tkv), BF16))
        out_specs.append(pl.BlockSpec((1, tm // tkv, WIDTH, tkv), lambda b, j: (b, j, 0, 0)))
    return pl.pallas_call(
        functools.partial(_proj_kernel, tm=tm, tkv=tkv, emit_vt=emit_vt),
        grid=(nb, nt),
        in_specs=in_specs,
        out_specs=out_specs,
        out_shape=out_shape,
        scratch_shapes=[pltpu.VMEM((tm + STATE_ROWS, WIDTH), F32)],
        compiler_params=pltpu.CompilerParams(
            dimension_semantics=("arbitrary", "arbitrary"),
            vmem_limit_bytes=V7X_VMEM_LIMIT),
        name="proj",
    )(x, state8, cos_t, sin_t, ln_g, w_in_bf, qg, kg, conv_w, conv_b, gmat)


def _lambda(lq1_ref, lk1_ref, lq2_ref, lk2_ref):
    a = jnp.sum(lq1_ref[...] * lk1_ref[...], axis=-1, keepdims=True)
    b = jnp.sum(lq2_ref[...] * lk2_ref[...], axis=-1, keepdims=True)
    return jnp.exp(a) - jnp.exp(b) + LAMBDA_INIT


def _finish(x, o_heads, sga, oc, subg, wout_ref):
    parts = []
    for h, o in enumerate(o_heads):
        ms = jnp.mean(o * o, axis=-1, keepdims=True)
        on = o * lax.rsqrt(ms + EPS) * subg * (1.0 - LAMBDA_INIT)
        parts.append((sga[:, h * V_DIM:(h + 1) * V_DIM].astype(F32) * on).astype(BF16))
    parts.append(oc)
    mix = jnp.concatenate(parts, axis=1)
    return x + jnp.dot(mix, wout_ref[...], preferred_element_type=F32)


def _prompt_attn_kernel(bound_ref, x_ref, q_ref, kb_ref, vt_ref, km_ref, vmt_ref, sga_ref, oc_ref,
                        subg_ref, lq1_ref, lk1_ref, lq2_ref, lk2_ref, wout_ref, y_ref,
                        qz_ref, m_ref, l_ref, acc_ref, *, tq, big):
    j = pl.program_id(1)
    lanes = 2 * tq

    lane = lax.broadcasted_iota(jnp.int32, (1, V_DIM), 1)
    for h in range(N_HEADS):
        qh = q_ref[0, :, h * V_DIM:(h + 1) * V_DIM]
        qz_ref[h, 0:tq, :] = jnp.where(lane < HEAD_DIM, qh, jnp.zeros_like(qh))
        qz_ref[h, tq:lanes, :] = jnp.where(lane >= HEAD_DIM, qh, jnp.zeros_like(qh))

    def scores(k_h, h):
        return lax.dot_general(k_h, qz_ref[h], (((1,), (1,)), ((), ())), preferred_element_type=F32)

    def k_slab(k, h):
        return k[:, h * V_DIM:(h + 1) * V_DIM]

    def pv(vt_blocks, h, e):
        out = None
        for t, vt in enumerate(vt_blocks):
            part = jnp.dot(vt[h * V_DIM:(h + 1) * V_DIM, :], e[t * tq:(t + 1) * tq, :],
                           preferred_element_type=F32)
            out = part if out is None else out + part
        return out

    def sublane_partial(e):
        return jnp.sum(e.reshape(e.shape[0] // 8, 8, lanes), axis=0)

    kchunk = lax.broadcasted_iota(jnp.int32, (tq, 1), 0) // CHUNK
    qchunk = (lax.broadcasted_iota(jnp.int32, (1, lanes), 1) % tq) // CHUNK
    diag_mask = kchunk <= qchunk

    @pl.when(bound_ref[0] <= SAFE_SCORE_BOUND)
    def _():
        for h in range(N_HEADS):
            e = jnp.exp2(scores(k_slab(km_ref, h), h))
            l_ref[h] = sublane_partial(e)
            acc_ref[h] = jnp.dot(vmt_ref[h * V_DIM:(h + 1) * V_DIM, :], e.astype(BF16),
                                 preferred_element_type=F32)

        def block(first, n, mask):
            k = kb_ref[0, pl.ds(first, n)].reshape(n * tq, WIDTH)
            vts = [vt_ref[0, first + t] for t in range(n)]
            for h in range(N_HEADS):
                e = jnp.exp2(scores(k_slab(k, h), h))
                if mask is not None:
                    e = jnp.where(mask, e, 0.0)
                l_ref[h] += sublane_partial(e)
                acc_ref[h] += pv(vts, h, e.astype(BF16))

        def big_block(i, carry):
            block(i * big, big, None)
            return carry

        def small_block(i, carry):
            block(i, 1, None)
            return carry

        n_big = j // big
        lax.fori_loop(0, n_big, big_block, 0)
        lax.fori_loop(n_big * big, j, small_block, 0)
        block(j, 1, diag_mask)

    @pl.when(bound_ref[0] > SAFE_SCORE_BOUND)
    def _():
        for h in range(N_HEADS):
            s = scores(k_slab(km_ref, h), h)
            m = jnp.max(s, axis=0, keepdims=True)
            e = jnp.exp2(s - m)
            m_ref[h:h + 1, :] = m
            l_ref[h] = jnp.zeros((8, lanes), F32)
            l_ref[h, 0:1, :] = jnp.sum(e, axis=0, keepdims=True)
            acc_ref[h] = jnp.dot(vmt_ref[h * V_DIM:(h + 1) * V_DIM, :], e.astype(BF16),
                                 preferred_element_type=F32)

        def block(i, mask):
            k = kb_ref[0, i]
            vt = vt_ref[0, i]
            for h in range(N_HEADS):
                s = scores(k_slab(k, h), h)
                if mask is not None:
                    s = jnp.where(mask, s, NEG_BIG)
                m_old = m_ref[h:h + 1, :]
                m_new = jnp.maximum(m_old, jnp.max(s, axis=0, keepdims=True))
                alpha = jnp.exp2(m_old - m_new)
                e = jnp.exp2(s - m_new)
                l_ref[h, 0:1, :] = alpha * l_ref[h, 0:1, :] + jnp.sum(e, axis=0, keepdims=True)
                acc_ref[h] = alpha * acc_ref[h] + pv([vt], h, e.astype(BF16))
                m_ref[h:h + 1, :] = m_new

        def full_block(i, carry):
            block(i, None)
            return carry

        lax.fori_loop(0, j, full_block, 0)
        block(j, diag_mask)

    lam = _lambda(lq1_ref, lk1_ref, lq2_ref, lk2_ref)
    o_heads = []
    for h in range(N_HEADS):
        o2 = acc_ref[h] / jnp.sum(l_ref[h], axis=0, keepdims=True)
        o_heads.append((o2[:, 0:tq] - lam * o2[:, tq:lanes]).T)
    y_ref[0] = _finish(x_ref[0], o_heads, sga_ref[0], oc_ref[0], subg_ref[...], wout_ref)


def _prompt_attn(bound, x, q, kb4, vt4, km_bf, vmt_bf, sga, oc, subg, lq1, lk1, lq2, lk2, w_out_bf,
                 *, tq, big):
    nb, s, d = x.shape
    nq = s // tq
    row = lambda b, j: (b, j, 0)
    const2 = lambda b, j: (0, 0)
    whole = lambda b, j: (b, 0, 0, 0)
    in_specs = [
        pl.BlockSpec(memory_space=pltpu.SMEM),
        pl.BlockSpec((1, tq, d), row),
        pl.BlockSpec((1, tq, WIDTH), row),
        pl.BlockSpec((1, nq, tq, WIDTH), whole),
        pl.BlockSpec((1, nq, WIDTH, tq), whole),
        pl.BlockSpec((N_META, WIDTH), const2),
        pl.BlockSpec((WIDTH, N_META), const2),
        pl.BlockSpec((1, tq, WIDTH), row),
        pl.BlockSpec((1, tq, WIDTH), row),
        pl.BlockSpec((1, V_DIM), const2),
        pl.BlockSpec((1, HEAD_DIM), const2),
        pl.BlockSpec((1, HEAD_DIM), const2),
        pl.BlockSpec((1, HEAD_DIM), const2),
        pl.BlockSpec((1, HEAD_DIM), const2),
        pl.BlockSpec(w_out_bf.shape, const2),
    ]
    return pl.pallas_call(
        functools.partial(_prompt_attn_kernel, tq=tq, big=big),
        grid=(nb, nq),
        in_specs=in_specs,
        out_specs=pl.BlockSpec((1, tq, d), row),
        out_shape=jax.ShapeDtypeStruct((nb, s, d), F32),
        scratch_shapes=[
            pltpu.VMEM((N_HEADS, 2 * tq, V_DIM), BF16),
            pltpu.VMEM((N_HEADS, 2 * tq), F32),
            pltpu.VMEM((N_HEADS, 8, 2 * tq), F32),
            pltpu.VMEM((N_HEADS, V_DIM, 2 * tq), F32),
        ],
        compiler_params=pltpu.CompilerParams(
            dimension_semantics=("arbitrary", "arbitrary"),
            vmem_limit_bytes=V7X_VMEM_LIMIT),
        name="prompt_attn",
    )(bound, x, q, kb4, vt4, km_bf, vmt_bf, sga, oc, subg, lq1, lk1, lq2, lk2, w_out_bf)


def _sample_attn_kernel(x_ref, q_ref, kn_ref, vn_ref, ck_ref, cv_ref, km_ref, vm_ref, sga_ref,
                        oc_ref, subg_ref, lq1_ref, lk1_ref, lq2_ref, lk2_ref, wout_ref, y_ref,
                        kcat, vcat, *, past, t_new, n_keys_pad):
    n_keys = past + N_META + t_new
    kcat[0:past, :] = ck_ref[0].astype(BF16)
    kcat[past:past + N_META, :] = km_ref[...]
    kcat[past + N_META:n_keys, :] = kn_ref[0]
    kcat[n_keys:n_keys_pad, :] = jnp.zeros((n_keys_pad - n_keys, WIDTH), BF16)
    vcat[0:past, :] = cv_ref[0].astype(BF16)
    vcat[past:past + N_META, :] = vm_ref[...].astype(BF16)
    vcat[past + N_META:n_keys, :] = vn_ref[0].astype(BF16)
    vcat[n_keys:n_keys_pad, :] = jnp.zeros((n_keys_pad - n_keys, WIDTH), BF16)

    lane = lax.broadcasted_iota(jnp.int32, (1, V_DIM), 1)
    valid = lax.broadcasted_iota(jnp.int32, (1, n_keys_pad), 1) < n_keys
    lam = _lambda(lq1_ref, lk1_ref, lq2_ref, lk2_ref)
    o_heads = []
    for h in range(N_HEADS):
        qh = q_ref[0, :, h * V_DIM:(h + 1) * V_DIM]
        k_h = kcat[:, h * V_DIM:(h + 1) * V_DIM]
        v_h = vcat[:, h * V_DIM:(h + 1) * V_DIM]
        outs = []
        for comp in range(2):
            keep = (lane < HEAD_DIM) if comp == 0 else (lane >= HEAD_DIM)
            qc = jnp.where(keep, qh, jnp.zeros_like(qh))
            s = lax.dot_general(qc, k_h, (((1,), (1,)), ((), ())), preferred_element_type=F32)
            s = jnp.where(valid, s, NEG_BIG)
            m = jnp.max(s, axis=-1, keepdims=True)
            e = jnp.exp2(s - m)
            l = jnp.sum(e, axis=-1, keepdims=True)
            outs.append(jnp.dot(e.astype(BF16), v_h, preferred_element_type=F32) / l)
        o_heads.append(outs[0] - lam * outs[1])
    y_ref[0] = _finish(x_ref[0], o_heads, sga_ref[0], oc_ref[0], subg_ref[...], wout_ref)


def _sample_attn(x, q, kn_bf, vn, ck, cv, km_bf, vm, sga, oc, subg, lq1, lk1, lq2, lk2, w_out_bf):
    nb, t_new, d = x.shape
    past = ck.shape[1]
    n_keys_pad = -(-(past + N_META + t_new) // V_DIM) * V_DIM
    row = lambda b: (b, 0, 0)
    const2 = lambda b: (0, 0)
    in_specs = [
        pl.BlockSpec((1, t_new, d), row),
        pl.BlockSpec((1, t_new, WIDTH), row),
        pl.BlockSpec((1, t_new, WIDTH), row),
        pl.BlockSpec((1, t_new, WIDTH), row),
        pl.BlockSpec((1, past, WIDTH), row),
        pl.BlockSpec((1, past, WIDTH), row),
        pl.BlockSpec((N_META, WIDTH), const2),
        pl.BlockSpec((N_META, WIDTH), const2),
        pl.BlockSpec((1, t_new, WIDTH), row),
        pl.BlockSpec((1, t_new, WIDTH), row),
        pl.BlockSpec((1, V_DIM), const2),
        pl.BlockSpec((1, HEAD_DIM), const2),
        pl.BlockSpec((1, HEAD_DIM), const2),
        pl.BlockSpec((1, HEAD_DIM), const2),
        pl.BlockSpec((1, HEAD_DIM), const2),
        pl.BlockSpec(w_out_bf.shape, const2),
    ]
    return pl.pallas_call(
        functools.partial(_sample_attn_kernel, past=past, t_new=t_new, n_keys_pad=n_keys_pad),
        grid=(nb,),
        in_specs=in_specs,
        out_specs=pl.BlockSpec((1, t_new, d), row),
        out_shape=jax.ShapeDtypeStruct((nb, t_new, d), F32),
        scratch_shapes=[pltpu.VMEM((n_keys_pad, WIDTH), BF16),
                        pltpu.VMEM((n_keys_pad, WIDTH), BF16)],
        compiler_params=pltpu.CompilerParams(
            dimension_semantics=("arbitrary",),
            vmem_limit_bytes=V7X_VMEM_LIMIT),
        name="sample_attn",
    )(x, q, kn_bf, vn, ck, cv, km_bf, vm, sga, oc, subg, lq1, lk1, lq2, lk2, w_out_bf)


def _rope_tables(pos):
    half = ROT_DIM // 2
    inv = ROPE_THETA ** (-jnp.arange(0, ROT_DIM, 2, dtype=F32) / ROT_DIM)
    ang = pos.astype(F32)[:, None] * inv[None, :]
    cos, sin = jnp.cos(ang), jnp.sin(ang)
    t = pos.shape[0]
    rest = HEAD_DIM - 2 * half
    c64 = jnp.concatenate([cos, cos, jnp.ones((t, rest), F32)], axis=1)
    s64 = jnp.concatenate([-sin, sin, jnp.zeros((t, rest), F32)], axis=1)
    return jnp.tile(c64, (1, 2)), jnp.tile(s64, (1, 2))


def _score_bound(qg, kg):
    bound = Q_SCALE * HEAD_DIM * jnp.max(jnp.abs(qg)) * jnp.max(jnp.abs(kg)) * (1.0 + 2.0 ** -6)
    return bound.reshape(1).astype(F32)


def _pad_state(state):
    return jnp.pad(state, ((0, 0), (STATE_ROWS - (CONV_W - 1), 0), (0, 0)))


def _tile_sizes(seq):
    tq = 256 if seq % 256 == 0 else seq
    return tq, tq


def kernel(x_prompt, x_sample, cache_k, cache_v, state_conv, meta_tokens, ln_g, w_in, q_norm_g,
           k_norm_g, lam_q1, lam_k1, lam_q2, lam_k2, subln_g, conv_w, conv_b, w_out):
    assert ln_g.shape[0] == 1, "single-layer stack"
    nb, seq, d = x_prompt.shape
    db, t_new, _ = x_sample.shape
    past = cache_k.shape[2]

    w_in_bf = w_in[0].astype(BF16)
    w_out_bf = w_out[0].astype(BF16)
    lng = ln_g[0][None, :]
    qg = jnp.tile(q_norm_g[0], 2 * N_HEADS)[None, :]
    kg = jnp.tile(k_norm_g[0], 2 * N_HEADS)[None, :]
    subg = subln_g[0][None, :]
    cw, cb = conv_w[0], conv_b[0][None, :]
    lams = [p[0][None, :] for p in (lam_q1, lam_k1, lam_q2, lam_k2)]
    grp = jnp.arange(WIDTH, dtype=jnp.int32) // HEAD_DIM
    gmat = (grp[:, None] == grp[None, :]).astype(BF16)

    pos_m = jnp.arange(N_META, dtype=jnp.int32)
    pos_p = N_META + jnp.arange(seq, dtype=jnp.int32)
    pos_s = N_META + past + jnp.arange(t_new, dtype=jnp.int32)
    weights = (lng, w_in_bf, qg, kg, cw, cb, gmat)
    tm, tq = _tile_sizes(seq)

    zero_state = jnp.zeros((1, STATE_ROWS, WIDTH), F32)
    _, km, vm, km_bf, _, _, tail_m = _proj_call(
        meta_tokens[None], zero_state, *_rope_tables(pos_m), *weights,
        tm=N_META, tkv=N_META, emit_vt=False)
    km, vm, km_bf = km[0], vm[0], km_bf[0]

    state_p = jnp.broadcast_to(tail_m, (nb, STATE_ROWS, WIDTH))
    qp, kp, vp, kp_bf, sga_p, oc_p, tail_p, vtp = _proj_call(
        x_prompt, state_p, *_rope_tables(pos_p), *weights, tm=tm, tkv=tq, emit_vt=True)
    y_prompt = _prompt_attn(
        _score_bound(q_norm_g[0], k_norm_g[0]), x_prompt, qp,
        kp_bf.reshape(nb, seq // tq, tq, WIDTH), vtp, km_bf, vm.T.astype(BF16),
        sga_p, oc_p, subg, *lams, w_out_bf, tq=tq, big=BIG_KEY_BLOCKS)

    qs, ks, vs, ks_bf, sga_s, oc_s, tail_s = _proj_call(
        x_sample, _pad_state(state_conv[0]), *_rope_tables(pos_s), *weights,
        tm=t_new, tkv=t_new, emit_vt=False)
    y_sample = _sample_attn(
        x_sample, qs, ks_bf, vs, cache_k[0].reshape(db, past, WIDTH),
        cache_v[0].reshape(db, past, WIDTH), km_bf, vm, sga_s, oc_s, subg, *lams, w_out_bf)

    def with_meta(meta, body):
        full = jnp.concatenate([jnp.broadcast_to(meta[None], (nb,) + meta.shape), body], axis=1)
        return full.reshape(1, nb, N_META + seq, N_HEADS, V_DIM)

    tail_rows = slice(STATE_ROWS - (CONV_W - 1), STATE_ROWS)
    return (y_prompt, y_sample, with_meta(km, kp), with_meta(vm, vp), tail_p[None, :, tail_rows],
            ks.reshape(1, db, t_new, N_HEADS, V_DIM), vs.reshape(1, db, t_new, N_HEADS, V_DIM),
            tail_s[None, :, tail_rows])
```

```python
import functools
import math

import jax
import jax.numpy as jnp
from jax import lax
from jax.experimental import pallas as pl
from jax.experimental.pallas import tpu as pltpu

F32 = jnp.float32
BF16 = jnp.bfloat16

CHUNK = 64
N_META = 16
N_HEADS = 4
HEAD_DIM = 64
V_DIM = 2 * HEAD_DIM
WIDTH = N_HEADS * V_DIM
CONV_W = 3
ROT_DIM = HEAD_DIM // 4
ROPE_THETA = 500000.0
EPS = 1e-6
LAMBDA_INIT = 0.8 - 0.6 * math.exp(-0.3 * 0)
Q_SCALE = HEAD_DIM ** -0.5 * math.log2(math.e)
NEG_BIG = -1e30
SAFE_SCORE_BOUND = 40.0
BIG_KEY_BLOCKS = 4
STATE_ROWS = 8

V7X_VMEM_LIMIT = 56 * 1024 * 1024


def _silu(x):
    return x / (1.0 + jnp.exp(-x))


def _proj_kernel(x_ref, st_ref, cos_ref, sin_ref, lng_ref, win_ref, qg_ref, kg_ref, cw_ref, cb_ref,
                 gmat_ref, q_ref, kf_ref, vf_ref, kb_ref, sga_ref, oc_ref, tail_ref, *rest,
                 tm, tkv, emit_vt):
    if emit_vt:
        vt_ref, ubuf = rest
    else:
        (ubuf,) = rest
    j = pl.program_id(1)

    x = x_ref[0]
    ms = jnp.mean(x * x, axis=-1, keepdims=True)
    xn = (x * lax.rsqrt(ms + EPS) * lng_ref[...]).astype(BF16)

    def proj(i):
        return jnp.dot(xn, win_ref[:, i * WIDTH:(i + 1) * WIDTH], preferred_element_type=F32)

    cos = cos_ref[...]
    sin = sin_ref[...]
    lane = lax.broadcasted_iota(jnp.int32, (1, V_DIM), 1) % HEAD_DIM
    take_lower = lane >= ROT_DIM // 2

    def norm_rope(t, g):
        ss = jnp.dot((t * t).astype(BF16), gmat_ref[...], preferred_element_type=F32)
        tn = t * lax.rsqrt(ss * (1.0 / HEAD_DIM) + EPS) * g
        outs = []
        for h in range(N_HEADS):
            th = tn[:, h * V_DIM:(h + 1) * V_DIM]
            lower = pltpu.roll(th, ROT_DIM // 2, axis=1)
            upper = pltpu.roll(th, V_DIM - ROT_DIM // 2, axis=1)
            outs.append(th * cos + jnp.where(take_lower, lower, upper) * sin)
        return jnp.concatenate(outs, axis=1)

    q = norm_rope(proj(0), qg_ref[...])
    q_ref[0] = (q * Q_SCALE).astype(BF16)
    k = norm_rope(proj(1), kg_ref[...])
    kb_ref[0] = k.astype(BF16)
    v = proj(2)
    for h in range(N_HEADS):
        kf_ref[0, pl.ds(h, tm, stride=N_HEADS), :] = k[:, h * V_DIM:(h + 1) * V_DIM]
        vf_ref[0, pl.ds(h, tm, stride=N_HEADS), :] = v[:, h * V_DIM:(h + 1) * V_DIM]
    if emit_vt:
        for s in range(tm // tkv):
            vt_ref[0, s] = v[s * tkv:(s + 1) * tkv, :].T.astype(BF16)
    sga_ref[0] = _silu(proj(3)).astype(BF16)

    bb = proj(4)
    u = proj(5) * proj(6)

    @pl.when(j == 0)
    def _():
        ubuf[0:STATE_ROWS, :] = st_ref[0]

    @pl.when(j > 0)
    def _():
        ubuf[0:STATE_ROWS, :] = ubuf[tm:tm + STATE_ROWS, :]

    ubuf[STATE_ROWS:STATE_ROWS + tm, :] = u
    y = cb_ref[...]
    for tap in range(CONV_W):
        off = STATE_ROWS - (CONV_W - 1) + tap
        y = y + cw_ref[tap:tap + 1, :] * ubuf[off:off + tm, :]
    oc_ref[0] = (_silu(proj(7)) * (bb * y)).astype(BF16)
    tail_ref[0] = ubuf[tm:tm + STATE_ROWS, :]


def _proj_call(x, state8, cos_t, sin_t, ln_g, w_in_bf, qg, kg, conv_w, conv_b, gmat, *, tm, tkv,
               emit_vt, row_offset=0):
    nb, t, d = x.shape
    nt = t // tm
    assert t % tm == 0 and tm % tkv == 0
    row = lambda b, j: (b, j, 0)
    const2 = lambda b, j: (0, 0)
    in_specs = [
        pl.BlockSpec((1, tm, d), row),
        pl.BlockSpec((1, STATE_ROWS, WIDTH), lambda b, j: (b, 0, 0)),
        pl.BlockSpec((tm, V_DIM), lambda b, j: (j, 0)),
        pl.BlockSpec((tm, V_DIM), lambda b, j: (j, 0)),
        pl.BlockSpec((1, d), const2),
        pl.BlockSpec(w_in_bf.shape, const2),
        pl.BlockSpec((1, WIDTH), const2),
        pl.BlockSpec((1, WIDTH), const2),
        pl.BlockSpec((CONV_W, WIDTH), const2),
        pl.BlockSpec((1, WIDTH), const2),
        pl.BlockSpec((WIDTH, WIDTH), const2),
    ]
    tile = pl.BlockSpec((1, tm, WIDTH), row)
    heads_tile = pl.BlockSpec(
        (pl.Element(1), pl.Element(tm * N_HEADS), pl.Element(V_DIM)),
        lambda b, j: (b, pl.multiple_of((row_offset + j * tm) * N_HEADS, 8), 0))
    out_shape = [
        jax.ShapeDtypeStruct((nb, t, WIDTH), BF16),
        jax.ShapeDtypeStruct((nb, (row_offset + t) * N_HEADS, V_DIM), F32),
        jax.ShapeDtypeStruct((nb, (row_offset + t) * N_HEADS, V_DIM), F32),
        jax.ShapeDtypeStruct((nb, t, WIDTH), BF16),
        jax.ShapeDtypeStruct((nb, t, WIDTH), BF16),
        jax.ShapeDtypeStruct((nb, t, WIDTH), BF16),
        jax.ShapeDtypeStruct((nb, STATE_ROWS, WIDTH), F32),
    ]
    out_specs = [tile, heads_tile, heads_tile, tile, tile, tile,
                 pl.BlockSpec((1, STATE_ROWS, WIDTH), lambda b, j: (b, 0, 0))]
    if emit_vt:
        out_shape.append(jax.ShapeDtypeStruct((nb, t // tkv, WIDTH, tkv), BF16))
        out_specs.append(pl.BlockSpec((1, tm // tkv, WIDTH, tkv), lambda b, j: (b, j, 0, 0)))
    return pl.pallas_call(
        functools.partial(_proj_kernel, tm=tm, tkv=tkv, emit_vt=emit_vt),
        grid=(nb, nt),
        in_specs=in_specs,
        out_specs=out_specs,
        out_shape=out_shape,
        scratch_shapes=[pltpu.VMEM((tm + STATE_ROWS, WIDTH), F32)],
        compiler_params=pltpu.CompilerParams(
            dimension_semantics=("arbitrary", "arbitrary"),
            vmem_limit_bytes=V7X_VMEM_LIMIT),
        name="proj",
    )(x, state8, cos_t, sin_t, ln_g, w_in_bf, qg, kg, conv_w, conv_b, gmat)


def _lambda(lq1_ref, lk1_ref, lq2_ref, lk2_ref):
    a = jnp.sum(lq1_ref[...] * lk1_ref[...], axis=-1, keepdims=True)
    b = jnp.sum(lq2_ref[...] * lk2_ref[...], axis=-1, keepdims=True)
    return jnp.exp(a) - jnp.exp(b) + LAMBDA_INIT


def _finish(x, o_heads, sga, oc, subg, wout_ref):
    parts = []
    for h, o in enumerate(o_heads):
        ms = jnp.mean(o * o, axis=-1, keepdims=True)
        on = o * lax.rsqrt(ms + EPS) * subg * (1.0 - LAMBDA_INIT)
        parts.append((sga[:, h * V_DIM:(h + 1) * V_DIM].astype(F32) * on).astype(BF16))
    parts.append(oc)
    mix = jnp.concatenate(parts, axis=1)
    return x + jnp.dot(mix, wout_ref[...], preferred_element_type=F32)


def _prompt_attn_kernel(bound_ref, x_ref, q_ref, kb_ref, vt_ref, km_ref, vmt_ref, sga_ref, oc_ref,
                        subg_ref, lq1_ref, lk1_ref, lq2_ref, lk2_ref, wout_ref, y_ref,
                        qz_ref, m_ref, l_ref, acc_ref, *, tq, big):
    j = pl.program_id(1)
    lanes = 2 * tq

    lane = lax.broadcasted_iota(jnp.int32, (1, V_DIM), 1)
    for h in range(N_HEADS):
        qh = q_ref[0, :, h * V_DIM:(h + 1) * V_DIM]
        qz_ref[h, 0:tq, :] = jnp.where(lane < HEAD_DIM, qh, jnp.zeros_like(qh))
        qz_ref[h, tq:lanes, :] = jnp.where(lane >= HEAD_DIM, qh, jnp.zeros_like(qh))

    def scores(k_h, h):
        return lax.dot_general(k_h, qz_ref[h], (((1,), (1,)), ((), ())), preferred_element_type=F32)

    def k_slab(k, h):
        return k[:, h * V_DIM:(h + 1) * V_DIM]

    def sublane_partial(e):
        return jnp.sum(e.reshape(e.shape[0] // 8, 8, lanes), axis=0)

    kchunk = lax.broadcasted_iota(jnp.int32, (tq, 1), 0) // CHUNK
    qchunk = (lax.broadcasted_iota(jnp.int32, (1, lanes), 1) % tq) // CHUNK
    diag_mask = kchunk <= qchunk

    @pl.when(bound_ref[0] <= SAFE_SCORE_BOUND)
    def _():
        def key_pass(k, pieces, assign):
            for h in range(N_HEADS):
                e = jnp.exp2(scores(k_slab(k, h), h))
                l_new = acc_new = None
                for row0, rows, vt, mask in pieces:
                    ep = e[row0:row0 + rows, :]
                    if mask is not None:
                        ep = jnp.where(mask, ep, 0.0)
                    lp = sublane_partial(ep)
                    ap = jnp.dot(vt[h * V_DIM:(h + 1) * V_DIM, :], ep.astype(BF16),
                                 preferred_element_type=F32)
                    l_new = lp if l_new is None else l_new + lp
                    acc_new = ap if acc_new is None else acc_new + ap
                if assign:
                    l_ref[h] = l_new
                    acc_ref[h] = acc_new
                else:
                    l_ref[h] += l_new
                    acc_ref[h] += acc_new

        n_big = j // big
        rem = j - n_big * big
        for r in range(big):
            @pl.when(rem == r)
            def _():
                first = j - r
                kblocks = kb_ref[0, pl.ds(first, r + 1)].reshape((r + 1) * tq, WIDTH)
                k = jnp.concatenate([km_ref[...], kblocks], axis=0)
                pieces = [(0, N_META, vmt_ref[...], None)]
                pieces += [(N_META + t * tq, tq, vt_ref[0, first + t], None) for t in range(r)]
                pieces.append((N_META + r * tq, tq, vt_ref[0, j], diag_mask))
                key_pass(k, pieces, assign=True)

        def big_block(i, carry):
            first = i * big
            k = kb_ref[0, pl.ds(first, big)].reshape(big * tq, WIDTH)
            key_pass(k, [(t * tq, tq, vt_ref[0, first + t], None) for t in range(big)], assign=False)
            return carry

        lax.fori_loop(0, n_big, big_block, 0)

    @pl.when(bound_ref[0] > SAFE_SCORE_BOUND)
    def _():
        for h in range(N_HEADS):
            s = scores(k_slab(km_ref, h), h)
            m = jnp.max(s, axis=0, keepdims=True)
            e = jnp.exp2(s - m)
            m_ref[h:h + 1, :] = m
            l_ref[h] = jnp.zeros((8, lanes), F32)
            l_ref[h, 0:1, :] = jnp.sum(e, axis=0, keepdims=True)
            acc_ref[h] = jnp.dot(vmt_ref[h * V_DIM:(h + 1) * V_DIM, :], e.astype(BF16),
                                 preferred_element_type=F32)

        def block(i, mask):
            k = kb_ref[0, i]
            vt = vt_ref[0, i]
            for h in range(N_HEADS):
                s = scores(k_slab(k, h), h)
                if mask is not None:
                    s = jnp.where(mask, s, NEG_BIG)
                m_old = m_ref[h:h + 1, :]
                m_new = jnp.maximum(m_old, jnp.max(s, axis=0, keepdims=True))
                alpha = jnp.exp2(m_old - m_new)
                e = jnp.exp2(s - m_new)
                l_ref[h, 0:1, :] = alpha * l_ref[h, 0:1, :] + jnp.sum(e, axis=0, keepdims=True)
                acc_ref[h] = alpha * acc_ref[h] + jnp.dot(
                    vt[h * V_DIM:(h + 1) * V_DIM, :], e.astype(BF16), preferred_element_type=F32)
                m_ref[h:h + 1, :] = m_new

        def full_block(i, carry):
            block(i, None)
            return carry

        lax.fori_loop(0, j, full_block, 0)
        block(j, diag_mask)

    lam = _lambda(lq1_ref, lk1_ref, lq2_ref, lk2_ref)
    o_heads = []
    for h in range(N_HEADS):
        o2 = acc_ref[h] / jnp.sum(l_ref[h], axis=0, keepdims=True)
        o_heads.append((o2[:, 0:tq] - lam * o2[:, tq:lanes]).T)
    y_ref[0] = _finish(x_ref[0], o_heads, sga_ref[0], oc_ref[0], subg_ref[...], wout_ref)


def _prompt_attn(bound, x, q, kb4, vt4, km_bf, vmt_bf, sga, oc, subg, lq1, lk1, lq2, lk2, w_out_bf,
                 *, tq, big):
    nb, s, d = x.shape
    nq = s // tq
    row = lambda b, j: (b, j, 0)
    const2 = lambda b, j: (0, 0)
    whole = lambda b, j: (b, 0, 0, 0)
    in_specs = [
        pl.BlockSpec(memory_space=pltpu.SMEM),
        pl.BlockSpec((1, tq, d), row),
        pl.BlockSpec((1, tq, WIDTH), row),
        pl.BlockSpec((1, nq, tq, WIDTH), whole),
        pl.BlockSpec((1, nq, WIDTH, tq), whole),
        pl.BlockSpec((N_META, WIDTH), const2),
        pl.BlockSpec((WIDTH, N_META), const2),
        pl.BlockSpec((1, tq, WIDTH), row),
        pl.BlockSpec((1, tq, WIDTH), row),
        pl.BlockSpec((1, V_DIM), const2),
        pl.BlockSpec((1, HEAD_DIM), const2),
        pl.BlockSpec((1, HEAD_DIM), const2),
        pl.BlockSpec((1, HEAD_DIM), const2),
        pl.BlockSpec((1, HEAD_DIM), const2),
        pl.BlockSpec(w_out_bf.shape, const2),
    ]
    return pl.pallas_call(
        functools.partial(_prompt_attn_kernel, tq=tq, big=big),
        grid=(nb, nq),
        in_specs=in_specs,
        out_specs=pl.BlockSpec((1, tq, d), row),
        out_shape=jax.ShapeDtypeStruct((nb, s, d), F32),
        scratch_shapes=[
            pltpu.VMEM((N_HEADS, 2 * tq, V_DIM), BF16),
            pltpu.VMEM((N_HEADS, 2 * tq), F32),
            pltpu.VMEM((N_HEADS, 8, 2 * tq), F32),
            pltpu.VMEM((N_HEADS, V_DIM, 2 * tq), F32),
        ],
        compiler_params=pltpu.CompilerParams(
            dimension_semantics=("arbitrary", "arbitrary"),
            vmem_limit_bytes=V7X_VMEM_LIMIT),
        name="prompt_attn",
    )(bound, x, q, kb4, vt4, km_bf, vmt_bf, sga, oc, subg, lq1, lk1, lq2, lk2, w_out_bf)


def _sample_attn_kernel(x_ref, q_ref, kn_ref, vn_ref, ck_ref, cv_ref, km_ref, vm_ref, sga_ref,
                        oc_ref, subg_ref, lq1_ref, lk1_ref, lq2_ref, lk2_ref, wout_ref, y_ref,
                        kcat, vcat, *, past, t_new, n_keys_pad):
    n_keys = past + N_META + t_new
    for h in range(N_HEADS):
        slab = slice(h * V_DIM, (h + 1) * V_DIM)
        kcat[0:past, slab] = ck_ref[0, pl.ds(h, past, stride=N_HEADS), :].astype(BF16)
        kcat[past + N_META:n_keys, slab] = kn_ref[0, pl.ds(h, t_new, stride=N_HEADS), :].astype(BF16)
        vcat[0:past, slab] = cv_ref[0, pl.ds(h, past, stride=N_HEADS), :].astype(BF16)
        vcat[past:past + N_META, slab] = vm_ref[pl.ds(h, N_META, stride=N_HEADS), :].astype(BF16)
        vcat[past + N_META:n_keys, slab] = vn_ref[0, pl.ds(h, t_new, stride=N_HEADS), :].astype(BF16)
    kcat[past:past + N_META, :] = km_ref[...]
    kcat[n_keys:n_keys_pad, :] = jnp.zeros((n_keys_pad - n_keys, WIDTH), BF16)
    vcat[n_keys:n_keys_pad, :] = jnp.zeros((n_keys_pad - n_keys, WIDTH), BF16)

    lane = lax.broadcasted_iota(jnp.int32, (1, V_DIM), 1)
    valid = lax.broadcasted_iota(jnp.int32, (1, n_keys_pad), 1) < n_keys
    lam = _lambda(lq1_ref, lk1_ref, lq2_ref, lk2_ref)
    o_heads = []
    for h in range(N_HEADS):
        qh = q_ref[0, :, h * V_DIM:(h + 1) * V_DIM]
        k_h = kcat[:, h * V_DIM:(h + 1) * V_DIM]
        v_h = vcat[:, h * V_DIM:(h + 1) * V_DIM]
        outs = []
        for comp in range(2):
            keep = (lane < HEAD_DIM) if comp == 0 else (lane >= HEAD_DIM)
            qc = jnp.where(keep, qh, jnp.zeros_like(qh))
            s = lax.dot_general(qc, k_h, (((1,), (1,)), ((), ())), preferred_element_type=F32)
            s = jnp.where(valid, s, NEG_BIG)
            m = jnp.max(s, axis=-1, keepdims=True)
            e = jnp.exp2(s - m)
            l = jnp.sum(e, axis=-1, keepdims=True)
            outs.append(jnp.dot(e.astype(BF16), v_h, preferred_element_type=F32) / l)
        o_heads.append(outs[0] - lam * outs[1])
    y_ref[0] = _finish(x_ref[0], o_heads, sga_ref[0], oc_ref[0], subg_ref[...], wout_ref)


def _sample_attn(x, q, kn, vn, ck, cv, km_bf, vm, sga, oc, subg, lq1, lk1, lq2, lk2, w_out_bf):
    nb, t_new, d = x.shape
    past = ck.shape[1] // N_HEADS
    n_keys_pad = -(-(past + N_META + t_new) // V_DIM) * V_DIM
    row = lambda b: (b, 0, 0)
    const2 = lambda b: (0, 0)
    in_specs = [
        pl.BlockSpec((1, t_new, d), row),
        pl.BlockSpec((1, t_new, WIDTH), row),
        pl.BlockSpec((1, t_new * N_HEADS, V_DIM), row),
        pl.BlockSpec((1, t_new * N_HEADS, V_DIM), row),
        pl.BlockSpec((1, past * N_HEADS, V_DIM), row),
        pl.BlockSpec((1, past * N_HEADS, V_DIM), row),
        pl.BlockSpec((N_META, WIDTH), const2),
        pl.BlockSpec((N_META * N_HEADS, V_DIM), const2),
        pl.BlockSpec((1, t_new, WIDTH), row),
        pl.BlockSpec((1, t_new, WIDTH), row),
        pl.BlockSpec((1, V_DIM), const2),
        pl.BlockSpec((1, HEAD_DIM), const2),
        pl.BlockSpec((1, HEAD_DIM), const2),
        pl.BlockSpec((1, HEAD_DIM), const2),
        pl.BlockSpec((1, HEAD_DIM), const2),
        pl.BlockSpec(w_out_bf.shape, const2),
    ]
    return pl.pallas_call(
        functools.partial(_sample_attn_kernel, past=past, t_new=t_new, n_keys_pad=n_keys_pad),
        grid=(nb,),
        in_specs=in_specs,
        out_specs=pl.BlockSpec((1, t_new, d), row),
        out_shape=jax.ShapeDtypeStruct((nb, t_new, d), F32),
        scratch_shapes=[pltpu.VMEM((n_keys_pad, WIDTH), BF16),
                        pltpu.VMEM((n_keys_pad, WIDTH), BF16)],
        compiler_params=pltpu.CompilerParams(
            dimension_semantics=("arbitrary",),
            vmem_limit_bytes=V7X_VMEM_LIMIT),
        name="sample_attn",
    )(x, q, kn, vn, ck, cv, km_bf, vm, sga, oc, subg, lq1, lk1, lq2, lk2, w_out_bf)


def _rope_tables(pos):
    half = ROT_DIM // 2
    inv = ROPE_THETA ** (-jnp.arange(0, ROT_DIM, 2, dtype=F32) / ROT_DIM)
    ang = pos.astype(F32)[:, None] * inv[None, :]
    cos, sin = lax.optimization_barrier((jnp.cos(ang), jnp.sin(ang)))
    t = pos.shape[0]
    rest = HEAD_DIM - 2 * half
    c64 = jnp.concatenate([cos, cos, jnp.ones((t, rest), F32)], axis=1)
    s64 = jnp.concatenate([-sin, sin, jnp.zeros((t, rest), F32)], axis=1)
    return jnp.tile(c64, (1, 2)), jnp.tile(s64, (1, 2))


def _score_bound(qg, kg):
    bound = Q_SCALE * HEAD_DIM * jnp.max(jnp.abs(qg)) * jnp.max(jnp.abs(kg)) * (1.0 + 2.0 ** -6)
    return bound.reshape(1).astype(F32)


def _pad_state(state):
    return jnp.pad(state, ((0, 0), (STATE_ROWS - (CONV_W - 1), 0), (0, 0)))


def _tile_sizes(seq):
    tq = 256 if seq % 256 == 0 else seq
    tm = 512 if seq % 512 == 0 else tq
    return tm, tq


def kernel(x_prompt, x_sample, cache_k, cache_v, state_conv, meta_tokens, ln_g, w_in, q_norm_g,
           k_norm_g, lam_q1, lam_k1, lam_q2, lam_k2, subln_g, conv_w, conv_b, w_out):
    assert ln_g.shape[0] == 1, "single-layer stack"
    nb, seq, d = x_prompt.shape
    db, t_new, _ = x_sample.shape
    past = cache_k.shape[2]

    w_in_bf = w_in[0].astype(BF16)
    w_out_bf = w_out[0].astype(BF16)
    lng = ln_g[0][None, :]
    qg = jnp.tile(q_norm_g[0], 2 * N_HEADS)[None, :]
    kg = jnp.tile(k_norm_g[0], 2 * N_HEADS)[None, :]
    subg = subln_g[0][None, :]
    cw, cb = conv_w[0], conv_b[0][None, :]
    lams = [p[0][None, :] for p in (lam_q1, lam_k1, lam_q2, lam_k2)]
    grp = jnp.arange(WIDTH, dtype=jnp.int32) // HEAD_DIM
    gmat = (grp[:, None] == grp[None, :]).astype(BF16)

    pos_m = jnp.arange(N_META, dtype=jnp.int32)
    pos_p = N_META + jnp.arange(seq, dtype=jnp.int32)
    pos_s = N_META + past + jnp.arange(t_new, dtype=jnp.int32)
    weights = (lng, w_in_bf, qg, kg, cw, cb, gmat)
    tm, tq = _tile_sizes(seq)

    zero_state = jnp.zeros((1, STATE_ROWS, WIDTH), F32)
    _, km, vm, km_bf, _, _, tail_m = _proj_call(
        meta_tokens[None], zero_state, *_rope_tables(pos_m), *weights,
        tm=N_META, tkv=N_META, emit_vt=False)
    km, vm, km_bf = km[0], vm[0], km_bf[0]

    state_p = jnp.broadcast_to(tail_m, (nb, STATE_ROWS, WIDTH))
    qp, kp, vp, kp_bf, sga_p, oc_p, tail_p, vtp = _proj_call(
        x_prompt, state_p, *_rope_tables(pos_p), *weights, tm=tm, tkv=tq, emit_vt=True,
        row_offset=N_META)
    y_prompt = _prompt_attn(
        _score_bound(q_norm_g[0], k_norm_g[0]), x_prompt, qp,
        kp_bf.reshape(nb, seq // tq, tq, WIDTH), vtp, km_bf,
        vm.reshape(N_META, WIDTH).T.astype(BF16),
        sga_p, oc_p, subg, *lams, w_out_bf, tq=tq, big=BIG_KEY_BLOCKS)
    kp = kp.at[:, :N_META * N_HEADS].set(jnp.broadcast_to(km[None], (nb,) + km.shape))
    vp = vp.at[:, :N_META * N_HEADS].set(jnp.broadcast_to(vm[None], (nb,) + vm.shape))

    qs, ks, vs, _, sga_s, oc_s, tail_s = _proj_call(
        x_sample, _pad_state(state_conv[0]), *_rope_tables(pos_s), *weights,
        tm=t_new, tkv=t_new, emit_vt=False)
    y_sample = _sample_attn(
        x_sample, qs, ks, vs, cache_k[0].reshape(db, past * N_HEADS, V_DIM),
        cache_v[0].reshape(db, past * N_HEADS, V_DIM), km_bf, vm, sga_s, oc_s, subg, *lams, w_out_bf)

    def heads_form(a, rows):
        return a.reshape(1, a.shape[0], rows, N_HEADS, V_DIM)

    tail_rows = slice(STATE_ROWS - (CONV_W - 1), STATE_ROWS)
    return (y_prompt, y_sample, heads_form(kp, N_META + seq), heads_form(vp, N_META + seq),
            tail_p[None, :, tail_rows], heads_form(ks, t_new), heads_form(vs, t_new),
            tail_s[None, :, tail_rows])
```

```python
import functools
import math

import jax
import jax.numpy as jnp
from jax import lax
from jax.experimental import pallas as pl
from jax.experimental.pallas import tpu as pltpu

F32 = jnp.float32
BF16 = jnp.bfloat16

CHUNK = 64
N_META = 16
N_HEADS = 4
HEAD_DIM = 64
V_DIM = 2 * HEAD_DIM
WIDTH = N_HEADS * V_DIM
CONV_W = 3
ROT_DIM = HEAD_DIM // 4
ROPE_THETA = 500000.0
EPS = 1e-6
LAMBDA_INIT = 0.8 - 0.6 * math.exp(-0.3 * 0)
Q_SCALE = HEAD_DIM ** -0.5 * math.log2(math.e)
NEG_BIG = -1e30
SAFE_SCORE_BOUND = 40.0
BIG_KEY_BLOCKS = 4
GROUP_SUM_WIDTH = 256
STATE_ROWS = 8

V7X_VMEM_LIMIT = 56 * 1024 * 1024


def _silu(x):
    return x / (1.0 + jnp.exp(-x))


def _proj_kernel(x_ref, st_ref, cos_ref, sin_ref, lng_ref, win_ref, qg_ref, kg_ref, cw_ref, cb_ref,
                 gmat_ref, q_ref, kf_ref, vf_ref, kb_ref, sga_ref, oc_ref, tail_ref, *rest,
                 tm, tkv, emit_vt):
    if emit_vt:
        vt_ref, ubuf = rest
    else:
        (ubuf,) = rest
    j = pl.program_id(1)

    @pl.when(j == 0)
    def _():
        ubuf[0:STATE_ROWS, :] = st_ref[0]

    @pl.when(j > 0)
    def _():
        ubuf[0:STATE_ROWS, :] = ubuf[tm:tm + STATE_ROWS, :]

    x = x_ref[0]
    ms = jnp.mean(x * x, axis=-1, keepdims=True)
    xn = (x * lax.rsqrt(ms + EPS) * lng_ref[...]).astype(BF16)

    def proj(i):
        return jnp.dot(xn, win_ref[:, i * WIDTH:(i + 1) * WIDTH], preferred_element_type=F32)

    def group_mean_sq(t):
        tt = (t * t).astype(BF16)
        halves = [jnp.dot(tt[:, i * GROUP_SUM_WIDTH:(i + 1) * GROUP_SUM_WIDTH], gmat_ref[...],
                          preferred_element_type=F32) for i in range(WIDTH // GROUP_SUM_WIDTH)]
        return jnp.concatenate(halves, axis=1) * (1.0 / HEAD_DIM)

    cos = cos_ref[...]
    sin = sin_ref[...]
    lane = lax.broadcasted_iota(jnp.int32, (1, V_DIM), 1) % HEAD_DIM
    take_lower = lane >= ROT_DIM // 2

    def norm_rope(t, mean_sq, g):
        tn = t * lax.rsqrt(mean_sq + EPS) * g
        outs = []
        for h in range(N_HEADS):
            th = tn[:, h * V_DIM:(h + 1) * V_DIM]
            lower = pltpu.roll(th, ROT_DIM // 2, axis=1)
            upper = pltpu.roll(th, V_DIM - ROT_DIM // 2, axis=1)
            outs.append(th * cos + jnp.where(take_lower, lower, upper) * sin)
        return jnp.concatenate(outs, axis=1)

    pq = proj(0)
    pk = proj(1)
    msq = group_mean_sq(pq)
    v = proj(2)
    msk = group_mean_sq(pk)
    ga = proj(3)
    q_ref[0] = (norm_rope(pq, msq, qg_ref[...]) * Q_SCALE).astype(BF16)
    bb = proj(4)
    k = norm_rope(pk, msk, kg_ref[...])
    kb_ref[0] = k.astype(BF16)
    cc = proj(5)
    for h in range(N_HEADS):
        kf_ref[0, pl.ds(h, tm, stride=N_HEADS), :] = k[:, h * V_DIM:(h + 1) * V_DIM]
        vf_ref[0, pl.ds(h, tm, stride=N_HEADS), :] = v[:, h * V_DIM:(h + 1) * V_DIM]
    hh = proj(6)
    if emit_vt:
        for s in range(tm // tkv):
            vt_ref[0, s] = v[s * tkv:(s + 1) * tkv, :].T.astype(BF16)
    gc = proj(7)
    sga_ref[0] = _silu(ga).astype(BF16)

    ubuf[STATE_ROWS:STATE_ROWS + tm, :] = cc * hh
    y = cb_ref[...]
    for tap in range(CONV_W):
        off = STATE_ROWS - (CONV_W - 1) + tap
        y = y + cw_ref[tap:tap + 1, :] * ubuf[off:off + tm, :]
    oc_ref[0] = (_silu(gc) * (bb * y)).astype(BF16)
    tail_ref[0] = ubuf[tm:tm + STATE_ROWS, :]


def _proj_call(x, state8, cos_t, sin_t, ln_g, w_in_bf, qg, kg, conv_w, conv_b, gmat, *, tm, tkv,
               emit_vt, row_offset=0):
    nb, t, d = x.shape
    nt = t // tm
    assert t % tm == 0 and tm % tkv == 0
    row = lambda b, j: (b, j, 0)
    const2 = lambda b, j: (0, 0)
    in_specs = [
        pl.BlockSpec((1, tm, d), row),
        pl.BlockSpec((1, STATE_ROWS, WIDTH), lambda b, j: (b, 0, 0)),
        pl.BlockSpec((tm, V_DIM), lambda b, j: (j, 0)),
        pl.BlockSpec((tm, V_DIM), lambda b, j: (j, 0)),
        pl.BlockSpec((1, d), const2),
        pl.BlockSpec(w_in_bf.shape, const2),
        pl.BlockSpec((1, WIDTH), const2),
        pl.BlockSpec((1, WIDTH), const2),
        pl.BlockSpec((CONV_W, WIDTH), const2),
        pl.BlockSpec((1, WIDTH), const2),
        pl.BlockSpec((GROUP_SUM_WIDTH, GROUP_SUM_WIDTH), const2),
    ]
    tile = pl.BlockSpec((1, tm, WIDTH), row)
    heads_tile = pl.BlockSpec(
        (pl.Element(1), pl.Element(tm * N_HEADS), pl.Element(V_DIM)),
        lambda b, j: (b, pl.multiple_of((row_offset + j * tm) * N_HEADS, 8), 0))
    out_shape = [
        jax.ShapeDtypeStruct((nb, t, WIDTH), BF16),
        jax.ShapeDtypeStruct((nb, (row_offset + t) * N_HEADS, V_DIM), F32),
        jax.ShapeDtypeStruct((nb, (row_offset + t) * N_HEADS, V_DIM), F32),
        jax.ShapeDtypeStruct((nb, t, WIDTH), BF16),
        jax.ShapeDtypeStruct((nb, t, WIDTH), BF16),
        jax.ShapeDtypeStruct((nb, t, WIDTH), BF16),
        jax.ShapeDtypeStruct((nb, STATE_ROWS, WIDTH), F32),
    ]
    out_specs = [tile, heads_tile, heads_tile, tile, tile, tile,
                 pl.BlockSpec((1, STATE_ROWS, WIDTH), lambda b, j: (b, 0, 0))]
    if emit_vt:
        out_shape.append(jax.ShapeDtypeStruct((nb, t // tkv, WIDTH, tkv), BF16))
        out_specs.append(pl.BlockSpec((1, tm // tkv, WIDTH, tkv), lambda b, j: (b, j, 0, 0)))
    return pl.pallas_call(
        functools.partial(_proj_kernel, tm=tm, tkv=tkv, emit_vt=emit_vt),
        grid=(nb, nt),
        in_specs=in_specs,
        out_specs=out_specs,
        out_shape=out_shape,
        scratch_shapes=[pltpu.VMEM((tm + STATE_ROWS, WIDTH), F32)],
        compiler_params=pltpu.CompilerParams(
            dimension_semantics=("arbitrary", "arbitrary"),
            vmem_limit_bytes=V7X_VMEM_LIMIT),
        name="proj",
    )(x, state8, cos_t, sin_t, ln_g, w_in_bf, qg, kg, conv_w, conv_b, gmat)


def _lambda(lq1_ref, lk1_ref, lq2_ref, lk2_ref):
    a = jnp.sum(lq1_ref[...] * lk1_ref[...], axis=-1, keepdims=True)
    b = jnp.sum(lq2_ref[...] * lk2_ref[...], axis=-1, keepdims=True)
    return jnp.exp(a) - jnp.exp(b) + LAMBDA_INIT


def _finish(x, o_heads, sga, oc, subg, wout_ref):
    parts = []
    for h, o in enumerate(o_heads):
        ms = jnp.mean(o * o, axis=-1, keepdims=True)
        on = o * lax.rsqrt(ms + EPS) * subg * (1.0 - LAMBDA_INIT)
        parts.append((sga[:, h * V_DIM:(h + 1) * V_DIM].astype(F32) * on).astype(BF16))
    parts.append(oc)
    mix = jnp.concatenate(parts, axis=1)
    return x + jnp.dot(mix, wout_ref[...], preferred_element_type=F32)


def _prompt_attn_kernel(bound_ref, x_ref, q_ref, kb_ref, vt_ref, km_ref, vmt_ref, sga_ref, oc_ref,
                        subg_ref, lq1_ref, lk1_ref, lq2_ref, lk2_ref, wout_ref, y_ref,
                        qz_ref, m_ref, l_ref, acc_ref, *, tq, big):
    j = pl.program_id(1)
    lanes = 2 * tq

    lane = lax.broadcasted_iota(jnp.int32, (1, V_DIM), 1)
    for h in range(N_HEADS):
        qh = q_ref[0, :, h * V_DIM:(h + 1) * V_DIM]
        qz_ref[h, 0:tq, :] = jnp.where(lane < HEAD_DIM, qh, jnp.zeros_like(qh))
        qz_ref[h, tq:lanes, :] = jnp.where(lane >= HEAD_DIM, qh, jnp.zeros_like(qh))

    def scores(k_h, h):
        return lax.dot_general(k_h, qz_ref[h], (((1,), (1,)), ((), ())), preferred_element_type=F32)

    def k_slab(k, h):
        return k[:, h * V_DIM:(h + 1) * V_DIM]

    def sublane_partial(e):
        return jnp.sum(e.reshape(e.shape[0] // 8, 8, lanes), axis=0)

    kchunk = lax.broadcasted_iota(jnp.int32, (tq, 1), 0) // CHUNK
    qchunk = (lax.broadcasted_iota(jnp.int32, (1, lanes), 1) % tq) // CHUNK
    diag_mask = kchunk <= qchunk

    @pl.when(bound_ref[0] <= SAFE_SCORE_BOUND)
    def _():
        def key_pass(k, pieces, assign):
            s_next = scores(k_slab(k, 0), 0)
            for h in range(N_HEADS):
                s = s_next
                if h + 1 < N_HEADS:
                    s_next = scores(k_slab(k, h + 1), h + 1)
                e = jnp.exp2(s)
                l_new = acc_new = None
                for row0, rows, vt, mask in pieces:
                    ep = e[row0:row0 + rows, :]
                    if mask is not None:
                        ep = jnp.where(mask, ep, 0.0)
                    lp = sublane_partial(ep)
                    ap = jnp.dot(vt[h * V_DIM:(h + 1) * V_DIM, :], ep.astype(BF16),
                                 preferred_element_type=F32)
                    l_new = lp if l_new is None else l_new + lp
                    acc_new = ap if acc_new is None else acc_new + ap
                if assign:
                    l_ref[h] = l_new
                    acc_ref[h] = acc_new
                else:
                    l_ref[h] += l_new
                    acc_ref[h] += acc_new

        n_big = j // big
        rem = j - n_big * big
        for r in range(big):
            @pl.when(rem == r)
            def _():
                first = j - r
                kblocks = kb_ref[0, pl.ds(first, r + 1)].reshape((r + 1) * tq, WIDTH)
                k = jnp.concatenate([km_ref[...], kblocks], axis=0)
                pieces = [(0, N_META, vmt_ref[...], None)]
                pieces += [(N_META + t * tq, tq, vt_ref[0, first + t], None) for t in range(r)]
                pieces.append((N_META + r * tq, tq, vt_ref[0, j], diag_mask))
                key_pass(k, pieces, assign=True)

        def big_block(i, carry):
            first = i * big
            k = kb_ref[0, pl.ds(first, big)].reshape(big * tq, WIDTH)
            key_pass(k, [(t * tq, tq, vt_ref[0, first + t], None) for t in range(big)], assign=False)
            return carry

        lax.fori_loop(0, n_big, big_block, 0)

    @pl.when(bound_ref[0] > SAFE_SCORE_BOUND)
    def _():
        for h in range(N_HEADS):
            s = scores(k_slab(km_ref, h), h)
            m = jnp.max(s, axis=0, keepdims=True)
            e = jnp.exp2(s - m)
            m_ref[h:h + 1, :] = m
            l_ref[h] = jnp.zeros((8, lanes), F32)
            l_ref[h, 0:1, :] = jnp.sum(e, axis=0, keepdims=True)
            acc_ref[h] = jnp.dot(vmt_ref[h * V_DIM:(h + 1) * V_DIM, :], e.astype(BF16),
                                 preferred_element_type=F32)

        def block(i, mask):
            k = kb_ref[0, i]
            vt = vt_ref[0, i]
            for h in range(N_HEADS):
                s = scores(k_slab(k, h), h)
                if mask is not None:
                    s = jnp.where(mask, s, NEG_BIG)
                m_old = m_ref[h:h + 1, :]
                m_new = jnp.maximum(m_old, jnp.max(s, axis=0, keepdims=True))
                alpha = jnp.exp2(m_old - m_new)
                e = jnp.exp2(s - m_new)
                l_ref[h, 0:1, :] = alpha * l_ref[h, 0:1, :] + jnp.sum(e, axis=0, keepdims=True)
                acc_ref[h] = alpha * acc_ref[h] + jnp.dot(
                    vt[h * V_DIM:(h + 1) * V_DIM, :], e.astype(BF16), preferred_element_type=F32)
                m_ref[h:h + 1, :] = m_new

        def full_block(i, carry):
            block(i, None)
            return carry

        lax.fori_loop(0, j, full_block, 0)
        block(j, diag_mask)

    lam = _lambda(lq1_ref, lk1_ref, lq2_ref, lk2_ref)
    o_heads = []
    for h in range(N_HEADS):
        o2 = acc_ref[h] / jnp.sum(l_ref[h], axis=0, keepdims=True)
        o_heads.append((o2[:, 0:tq] - lam * o2[:, tq:lanes]).T)
    y_ref[0] = _finish(x_ref[0], o_heads, sga_ref[0], oc_ref[0], subg_ref[...], wout_ref)


def _prompt_attn(bound, x, q, kb4, vt4, km_bf, vmt_bf, sga, oc, subg, lq1, lk1, lq2, lk2, w_out_bf,
                 *, tq, big):
    nb, s, d = x.shape
    nq = s // tq
    row = lambda b, j: (b, j, 0)
    const2 = lambda b, j: (0, 0)
    whole = lambda b, j: (b, 0, 0, 0)
    in_specs = [
        pl.BlockSpec(memory_space=pltpu.SMEM),
        pl.BlockSpec((1, tq, d), row),
        pl.BlockSpec((1, tq, WIDTH), row),
        pl.BlockSpec((1, nq, tq, WIDTH), whole),
        pl.BlockSpec((1, nq, WIDTH, tq), whole),
        pl.BlockSpec((N_META, WIDTH), const2),
        pl.BlockSpec((WIDTH, N_META), const2),
        pl.BlockSpec((1, tq, WIDTH), row),
        pl.BlockSpec((1, tq, WIDTH), row),
        pl.BlockSpec((1, V_DIM), const2),
        pl.BlockSpec((1, HEAD_DIM), const2),
        pl.BlockSpec((1, HEAD_DIM), const2),
        pl.BlockSpec((1, HEAD_DIM), const2),
        pl.BlockSpec((1, HEAD_DIM), const2),
        pl.BlockSpec(w_out_bf.shape, const2),
    ]
    return pl.pallas_call(
        functools.partial(_prompt_attn_kernel, tq=tq, big=big),
        grid=(nb, nq),
        in_specs=in_specs,
        out_specs=pl.BlockSpec((1, tq, d), row),
        out_shape=jax.ShapeDtypeStruct((nb, s, d), F32),
        scratch_shapes=[
            pltpu.VMEM((N_HEADS, 2 * tq, V_DIM), BF16),
            pltpu.VMEM((N_HEADS, 2 * tq), F32),
            pltpu.VMEM((N_HEADS, 8, 2 * tq), F32),
            pltpu.VMEM((N_HEADS, V_DIM, 2 * tq), F32),
        ],
        compiler_params=pltpu.CompilerParams(
            dimension_semantics=("arbitrary", "arbitrary"),
            vmem_limit_bytes=V7X_VMEM_LIMIT),
        name="prompt_attn",
    )(bound, x, q, kb4, vt4, km_bf, vmt_bf, sga, oc, subg, lq1, lk1, lq2, lk2, w_out_bf)


def _sample_attn_kernel(x_ref, q_ref, kn_ref, vn_ref, ck_ref, cv_ref, km_ref, vm_ref, sga_ref,
                        oc_ref, subg_ref, lq1_ref, lk1_ref, lq2_ref, lk2_ref, wout_ref, y_ref,
                        kcat, vcat, *, past, t_new, n_keys_pad):
    n_keys = past + N_META + t_new
    for h in range(N_HEADS):
        slab = slice(h * V_DIM, (h + 1) * V_DIM)
        kcat[0:past, slab] = ck_ref[0, pl.ds(h, past, stride=N_HEADS), :].astype(BF16)
        kcat[past + N_META:n_keys, slab] = kn_ref[0, pl.ds(h, t_new, stride=N_HEADS), :].astype(BF16)
        vcat[0:past, slab] = cv_ref[0, pl.ds(h, past, stride=N_HEADS), :].astype(BF16)
        vcat[past:past + N_META, slab] = vm_ref[pl.ds(h, N_META, stride=N_HEADS), :].astype(BF16)
        vcat[past + N_META:n_keys, slab] = vn_ref[0, pl.ds(h, t_new, stride=N_HEADS), :].astype(BF16)
    kcat[past:past + N_META, :] = km_ref[...]
    kcat[n_keys:n_keys_pad, :] = jnp.zeros((n_keys_pad - n_keys, WIDTH), BF16)
    vcat[n_keys:n_keys_pad, :] = jnp.zeros((n_keys_pad - n_keys, WIDTH), BF16)

    lane = lax.broadcasted_iota(jnp.int32, (1, V_DIM), 1)
    valid = lax.broadcasted_iota(jnp.int32, (1, n_keys_pad), 1) < n_keys
    lam = _lambda(lq1_ref, lk1_ref, lq2_ref, lk2_ref)
    o_heads = []
    for h in range(N_HEADS):
        qh = q_ref[0, :, h * V_DIM:(h + 1) * V_DIM]
        k_h = kcat[:, h * V_DIM:(h + 1) * V_DIM]
        v_h = vcat[:, h * V_DIM:(h + 1) * V_DIM]
        outs = []
        for comp in range(2):
            keep = (lane < HEAD_DIM) if comp == 0 else (lane >= HEAD_DIM)
            qc = jnp.where(keep, qh, jnp.zeros_like(qh))
            s = lax.dot_general(qc, k_h, (((1,), (1,)), ((), ())), preferred_element_type=F32)
            s = jnp.where(valid, s, NEG_BIG)
            m = jnp.max(s, axis=-1, keepdims=True)
            e = jnp.exp2(s - m)
            l = jnp.sum(e, axis=-1, keepdims=True)
            outs.append(jnp.dot(e.astype(BF16), v_h, preferred_element_type=F32) / l)
        o_heads.append(outs[0] - lam * outs[1])
    y_ref[0] = _finish(x_ref[0], o_heads, sga_ref[0], oc_ref[0], subg_ref[...], wout_ref)


def _sample_attn(x, q, kn, vn, ck, cv, km_bf, vm, sga, oc, subg, lq1, lk1, lq2, lk2, w_out_bf):
    nb, t_new, d = x.shape
    past = ck.shape[1] // N_HEADS
    n_keys_pad = -(-(past + N_META + t_new) // V_DIM) * V_DIM
    row = lambda b: (b, 0, 0)
    const2 = lambda b: (0, 0)
    in_specs = [
        pl.BlockSpec((1, t_new, d), row),
        pl.BlockSpec((1, t_new, WIDTH), row),
        pl.BlockSpec((1, t_new * N_HEADS, V_DIM), row),
        pl.BlockSpec((1, t_new * N_HEADS, V_DIM), row),
        pl.BlockSpec((1, past * N_HEADS, V_DIM), row),
        pl.BlockSpec((1, past * N_HEADS, V_DIM), row),
        pl.BlockSpec((N_META, WIDTH), const2),
        pl.BlockSpec((N_META * N_HEADS, V_DIM), const2),
        pl.BlockSpec((1, t_new, WIDTH), row),
        pl.BlockSpec((1, t_new, WIDTH), row),
        pl.BlockSpec((1, V_DIM), const2),
        pl.BlockSpec((1, HEAD_DIM), const2),
        pl.BlockSpec((1, HEAD_DIM), const2),
        pl.BlockSpec((1, HEAD_DIM), const2),
        pl.BlockSpec((1, HEAD_DIM), const2),
        pl.BlockSpec(w_out_bf.shape, const2),
    ]
    return pl.pallas_call(
        functools.partial(_sample_attn_kernel, past=past, t_new=t_new, n_keys_pad=n_keys_pad),
        grid=(nb,),
        in_specs=in_specs,
        out_specs=pl.BlockSpec((1, t_new, d), row),
        out_shape=jax.ShapeDtypeStruct((nb, t_new, d), F32),
        scratch_shapes=[pltpu.VMEM((n_keys_pad, WIDTH), BF16),
                        pltpu.VMEM((n_keys_pad, WIDTH), BF16)],
        compiler_params=pltpu.CompilerParams(
            dimension_semantics=("arbitrary",),
            vmem_limit_bytes=V7X_VMEM_LIMIT),
        name="sample_attn",
    )(x, q, kn, vn, ck, cv, km_bf, vm, sga, oc, subg, lq1, lk1, lq2, lk2, w_out_bf)


def _rope_tables(pos):
    half = ROT_DIM // 2
    inv = ROPE_THETA ** (-jnp.arange(0, ROT_DIM, 2, dtype=F32) / ROT_DIM)
    ang = pos.astype(F32)[:, None] * inv[None, :]
    cos, sin = lax.optimization_barrier((jnp.cos(ang), jnp.sin(ang)))
    t = pos.shape[0]
    rest = HEAD_DIM - 2 * half
    c64 = jnp.concatenate([cos, cos, jnp.ones((t, rest), F32)], axis=1)
    s64 = jnp.concatenate([-sin, sin, jnp.zeros((t, rest), F32)], axis=1)
    return jnp.tile(c64, (1, 2)), jnp.tile(s64, (1, 2))


def _score_bound(qg, kg):
    bound = Q_SCALE * HEAD_DIM * jnp.max(jnp.abs(qg)) * jnp.max(jnp.abs(kg)) * (1.0 + 2.0 ** -6)
    return bound.reshape(1).astype(F32)


def _pad_state(state):
    return jnp.pad(state, ((0, 0), (STATE_ROWS - (CONV_W - 1), 0), (0, 0)))


def _tile_sizes(seq):
    tq = 256 if seq % 256 == 0 else seq
    tm = 512 if seq % 512 == 0 else tq
    return tm, tq


def kernel(x_prompt, x_sample, cache_k, cache_v, state_conv, meta_tokens, ln_g, w_in, q_norm_g,
           k_norm_g, lam_q1, lam_k1, lam_q2, lam_k2, subln_g, conv_w, conv_b, w_out):
    assert ln_g.shape[0] == 1, "single-layer stack"
    nb, seq, d = x_prompt.shape
    db, t_new, _ = x_sample.shape
    past = cache_k.shape[2]

    w_in_bf = w_in[0].astype(BF16)
    w_out_bf = w_out[0].astype(BF16)
    lng = ln_g[0][None, :]
    qg = jnp.tile(q_norm_g[0], 2 * N_HEADS)[None, :]
    kg = jnp.tile(k_norm_g[0], 2 * N_HEADS)[None, :]
    subg = subln_g[0][None, :]
    cw, cb = conv_w[0], conv_b[0][None, :]
    lams = [p[0][None, :] for p in (lam_q1, lam_k1, lam_q2, lam_k2)]
    grp = jnp.arange(GROUP_SUM_WIDTH, dtype=jnp.int32) // HEAD_DIM
    gmat = (grp[:, None] == grp[None, :]).astype(BF16)

    pos_m = jnp.arange(N_META, dtype=jnp.int32)
    pos_p = N_META + jnp.arange(seq, dtype=jnp.int32)
    pos_s = N_META + past + jnp.arange(t_new, dtype=jnp.int32)
    weights = (lng, w_in_bf, qg, kg, cw, cb, gmat)
    tm, tq = _tile_sizes(seq)

    zero_state = jnp.zeros((1, STATE_ROWS, WIDTH), F32)
    _, km, vm, km_bf, _, _, tail_m = _proj_call(
        meta_tokens[None], zero_state, *_rope_tables(pos_m), *weights,
        tm=N_META, tkv=N_META, emit_vt=False)
    km, vm, km_bf = km[0], vm[0], km_bf[0]

    state_p = jnp.broadcast_to(tail_m, (nb, STATE_ROWS, WIDTH))
    qp, kp, vp, kp_bf, sga_p, oc_p, tail_p, vtp = _proj_call(
        x_prompt, state_p, *_rope_tables(pos_p), *weights, tm=tm, tkv=tq, emit_vt=True,
        row_offset=N_META)
    y_prompt = _prompt_attn(
        _score_bound(q_norm_g[0], k_norm_g[0]), x_prompt, qp,
        kp_bf.reshape(nb, seq // tq, tq, WIDTH), vtp, km_bf,
        vm.reshape(N_META, WIDTH).T.astype(BF16),
        sga_p, oc_p, subg, *lams, w_out_bf, tq=tq, big=BIG_KEY_BLOCKS)
    kp = kp.at[:, :N_META * N_HEADS].set(jnp.broadcast_to(km[None], (nb,) + km.shape))
    vp = vp.at[:, :N_META * N_HEADS].set(jnp.broadcast_to(vm[None], (nb,) + vm.shape))

    qs, ks, vs, _, sga_s, oc_s, tail_s = _proj_call(
        x_sample, _pad_state(state_conv[0]), *_rope_tables(pos_s), *weights,
        tm=t_new, tkv=t_new, emit_vt=False)
    y_sample = _sample_attn(
        x_sample, qs, ks, vs, cache_k[0].reshape(db, past * N_HEADS, V_DIM),
        cache_v[0].reshape(db, past * N_HEADS, V_DIM), km_bf, vm, sga_s, oc_s, subg, *lams, w_out_bf)

    def heads_form(a, rows):
        return a.reshape(1, a.shape[0], rows, N_HEADS, V_DIM)

    tail_rows = slice(STATE_ROWS - (CONV_W - 1), STATE_ROWS)
    return (y_prompt, y_sample, heads_form(kp, N_META + seq), heads_form(vp, N_META + seq),
            tail_p[None, :, tail_rows], heads_form(ks, t_new), heads_form(vs, t_new),
            tail_s[None, :, tail_rows])
```

```python
import functools
import math

import jax
import jax.numpy as jnp
from jax import lax
from jax.experimental import pallas as pl
from jax.experimental.pallas import tpu as pltpu

F32 = jnp.float32
BF16 = jnp.bfloat16

CHUNK = 64
N_META = 16
N_HEADS = 4
HEAD_DIM = 64
V_DIM = 2 * HEAD_DIM
WIDTH = N_HEADS * V_DIM
CONV_W = 3
ROT_DIM = HEAD_DIM // 4
ROPE_THETA = 500000.0
EPS = 1e-6
LAMBDA_INIT = 0.8 - 0.6 * math.exp(-0.3 * 0)
Q_SCALE = HEAD_DIM ** -0.5 * math.log2(math.e)
NEG_BIG = -1e30
SAFE_SCORE_BOUND = 40.0
BIG_KEY_BLOCKS = 8
GROUP_SUM_WIDTH = 256
STATE_ROWS = 8

V7X_VMEM_LIMIT = 56 * 1024 * 1024


def _silu(x):
    return x / (1.0 + jnp.exp(-x))


def _proj_kernel(x_ref, st_ref, cos_ref, sin_ref, lng_ref, win_ref, qg_ref, kg_ref, cw_ref, cb_ref,
                 gmat_ref, q_ref, kf_ref, vf_ref, kb_ref, sga_ref, oc_ref, tail_ref, *rest,
                 tm, tkv, emit_vt):
    if emit_vt:
        vt_ref, ubuf = rest
    else:
        (ubuf,) = rest
    j = pl.program_id(1)

    @pl.when(j == 0)
    def _():
        ubuf[0:STATE_ROWS, :] = st_ref[0]

    @pl.when(j > 0)
    def _():
        ubuf[0:STATE_ROWS, :] = ubuf[tm:tm + STATE_ROWS, :]

    x = x_ref[0]
    ms = jnp.mean(x * x, axis=-1, keepdims=True)
    xn = (x * lax.rsqrt(ms + EPS) * lng_ref[...]).astype(BF16)

    def proj(i):
        return jnp.dot(xn, win_ref[:, i * WIDTH:(i + 1) * WIDTH], preferred_element_type=F32)

    def group_mean_sq(t):
        tt = (t * t).astype(BF16)
        halves = [jnp.dot(tt[:, i * GROUP_SUM_WIDTH:(i + 1) * GROUP_SUM_WIDTH], gmat_ref[...],
                          preferred_element_type=F32) for i in range(WIDTH // GROUP_SUM_WIDTH)]
        return jnp.concatenate(halves, axis=1) * (1.0 / HEAD_DIM)

    cos = cos_ref[...]
    sin = sin_ref[...]
    lane = lax.broadcasted_iota(jnp.int32, (1, V_DIM), 1) % HEAD_DIM
    take_lower = lane >= ROT_DIM // 2

    def norm_rope(t, mean_sq, g):
        tn = t * lax.rsqrt(mean_sq + EPS) * g
        outs = []
        for h in range(N_HEADS):
            th = tn[:, h * V_DIM:(h + 1) * V_DIM]
            lower = pltpu.roll(th, ROT_DIM // 2, axis=1)
            upper = pltpu.roll(th, V_DIM - ROT_DIM // 2, axis=1)
            outs.append(th * cos + jnp.where(take_lower, lower, upper) * sin)
        return jnp.concatenate(outs, axis=1)

    pq = proj(0)
    pk = proj(1)
    msq = group_mean_sq(pq)
    v = proj(2)
    msk = group_mean_sq(pk)
    ga = proj(3)
    q_ref[0] = (norm_rope(pq, msq, qg_ref[...]) * Q_SCALE).astype(BF16)
    bb = proj(4)
    k = norm_rope(pk, msk, kg_ref[...])
    kb_ref[0] = k.astype(BF16)
    cc = proj(5)
    for h in range(N_HEADS):
        kf_ref[0, pl.ds(h, tm, stride=N_HEADS), :] = k[:, h * V_DIM:(h + 1) * V_DIM]
        vf_ref[0, pl.ds(h, tm, stride=N_HEADS), :] = v[:, h * V_DIM:(h + 1) * V_DIM]
    hh = proj(6)
    if emit_vt:
        for s in range(tm // tkv):
            vt_ref[0, s] = v[s * tkv:(s + 1) * tkv, :].T.astype(BF16)
    gc = proj(7)
    sga_ref[0] = _silu(ga).astype(BF16)

    ubuf[STATE_ROWS:STATE_ROWS + tm, :] = cc * hh
    y = cb_ref[...]
    for tap in range(CONV_W):
        off = STATE_ROWS - (CONV_W - 1) + tap
        y = y + cw_ref[tap:tap + 1, :] * ubuf[off:off + tm, :]
    oc_ref[0] = (_silu(gc) * (bb * y)).astype(BF16)
    tail_ref[0] = ubuf[tm:tm + STATE_ROWS, :]


def _proj_call(x, state8, cos_t, sin_t, ln_g, w_in_bf, qg, kg, conv_w, conv_b, gmat, *, tm, tkv,
               emit_vt, row_offset=0):
    nb, t, d = x.shape
    nt = t // tm
    assert t % tm == 0 and tm % tkv == 0
    row = lambda b, j: (b, j, 0)
    const2 = lambda b, j: (0, 0)
    in_specs = [
        pl.BlockSpec((1, tm, d), row),
        pl.BlockSpec((1, STATE_ROWS, WIDTH), lambda b, j: (b, 0, 0)),
        pl.BlockSpec((tm, V_DIM), lambda b, j: (j, 0)),
        pl.BlockSpec((tm, V_DIM), lambda b, j: (j, 0)),
        pl.BlockSpec((1, d), const2),
        pl.BlockSpec(w_in_bf.shape, const2),
        pl.BlockSpec((1, WIDTH), const2),
        pl.BlockSpec((1, WIDTH), const2),
        pl.BlockSpec((CONV_W, WIDTH), const2),
        pl.BlockSpec((1, WIDTH), const2),
        pl.BlockSpec((GROUP_SUM_WIDTH, GROUP_SUM_WIDTH), const2),
    ]
    tile = pl.BlockSpec((1, tm, WIDTH), row)
    heads_tile = pl.BlockSpec(
        (pl.Element(1), pl.Element(tm * N_HEADS), pl.Element(V_DIM)),
        lambda b, j: (b, pl.multiple_of((row_offset + j * tm) * N_HEADS, 8), 0))
    out_shape = [
        jax.ShapeDtypeStruct((nb, t, WIDTH), BF16),
        jax.ShapeDtypeStruct((nb, (row_offset + t) * N_HEADS, V_DIM), F32),
        jax.ShapeDtypeStruct((nb, (row_offset + t) * N_HEADS, V_DIM), F32),
        jax.ShapeDtypeStruct((nb, t, WIDTH), BF16),
        jax.ShapeDtypeStruct((nb, t, WIDTH), BF16),
        jax.ShapeDtypeStruct((nb, t, WIDTH), BF16),
        jax.ShapeDtypeStruct((nb, STATE_ROWS, WIDTH), F32),
    ]
    out_specs = [tile, heads_tile, heads_tile, tile, tile, tile,
                 pl.BlockSpec((1, STATE_ROWS, WIDTH), lambda b, j: (b, 0, 0))]
    if emit_vt:
        out_shape.append(jax.ShapeDtypeStruct((nb, t // tkv, WIDTH, tkv), BF16))
        out_specs.append(pl.BlockSpec((1, tm // tkv, WIDTH, tkv), lambda b, j: (b, j, 0, 0)))
    return pl.pallas_call(
        functools.partial(_proj_kernel, tm=tm, tkv=tkv, emit_vt=emit_vt),
        grid=(nb, nt),
        in_specs=in_specs,
        out_specs=out_specs,
        out_shape=out_shape,
        scratch_shapes=[pltpu.VMEM((tm + STATE_ROWS, WIDTH), F32)],
        compiler_params=pltpu.CompilerParams(
            dimension_semantics=("arbitrary", "arbitrary"),
            vmem_limit_bytes=V7X_VMEM_LIMIT),
        name="proj",
    )(x, state8, cos_t, sin_t, ln_g, w_in_bf, qg, kg, conv_w, conv_b, gmat)


def _lambda(lq1_ref, lk1_ref, lq2_ref, lk2_ref):
    a = jnp.sum(lq1_ref[...] * lk1_ref[...], axis=-1, keepdims=True)
    b = jnp.sum(lq2_ref[...] * lk2_ref[...], axis=-1, keepdims=True)
    return jnp.exp(a) - jnp.exp(b) + LAMBDA_INIT


def _conv_branch_out(x, oc, wout_ref):
    return x + jnp.dot(oc, wout_ref[WIDTH:, :], preferred_element_type=F32)


def _attn_branch_out(base, o_heads, sga, subg, wout_ref):
    parts = []
    for h, o in enumerate(o_heads):
        ms = jnp.mean(o * o, axis=-1, keepdims=True)
        on = o * lax.rsqrt(ms + EPS) * subg * (1.0 - LAMBDA_INIT)
        parts.append((sga[:, h * V_DIM:(h + 1) * V_DIM].astype(F32) * on).astype(BF16))
    mix = jnp.concatenate(parts, axis=1)
    return base + jnp.dot(mix, wout_ref[:WIDTH, :], preferred_element_type=F32)


def _prompt_attn_kernel(bound_ref, x_ref, q_ref, kb_ref, vt_ref, km_ref, vmt_ref, sga_ref, oc_ref,
                        subg_ref, lq1_ref, lk1_ref, lq2_ref, lk2_ref, wout_ref, y_ref,
                        qz_ref, m_ref, l_ref, acc_ref, *, tq, kb, big):
    j = pl.program_id(1)
    lanes = 2 * tq
    n_diag = tq // kb
    n_full = j * n_diag

    lane = lax.broadcasted_iota(jnp.int32, (1, V_DIM), 1)
    for h in range(N_HEADS):
        qh = q_ref[0, :, h * V_DIM:(h + 1) * V_DIM]
        qz_ref[h, 0:tq, :] = jnp.where(lane < HEAD_DIM, qh, jnp.zeros_like(qh))
        qz_ref[h, tq:lanes, :] = jnp.where(lane >= HEAD_DIM, qh, jnp.zeros_like(qh))

    def scores(k_h, h):
        return lax.dot_general(k_h, qz_ref[h], (((1,), (1,)), ((), ())), preferred_element_type=F32)

    def k_slab(k, h):
        return k[:, h * V_DIM:(h + 1) * V_DIM]

    def sublane_partial(e):
        return jnp.sum(e.reshape(e.shape[0] // 8, 8, lanes), axis=0)

    kchunk = lax.broadcasted_iota(jnp.int32, (kb, 1), 0) // CHUNK
    qchunk = (lax.broadcasted_iota(jnp.int32, (1, lanes), 1) % tq) // CHUNK
    diag_masks = [kchunk + d * (kb // CHUNK) <= qchunk for d in range(n_diag)]

    @pl.when(bound_ref[0] <= SAFE_SCORE_BOUND)
    def _():
        def key_pass(k, pieces, assign):
            s_next = scores(k_slab(k, 0), 0)
            for h in range(N_HEADS):
                s = s_next
                if h + 1 < N_HEADS:
                    s_next = scores(k_slab(k, h + 1), h + 1)
                e = jnp.exp2(s)
                l_new = acc_new = None
                for row0, rows, vt, mask in pieces:
                    ep = e[row0:row0 + rows, :]
                    if mask is not None:
                        ep = jnp.where(mask, ep, 0.0)
                    lp = sublane_partial(ep)
                    ap = jnp.dot(vt[h * V_DIM:(h + 1) * V_DIM, :], ep.astype(BF16),
                                 preferred_element_type=F32)
                    l_new = lp if l_new is None else l_new + lp
                    acc_new = ap if acc_new is None else acc_new + ap
                if assign:
                    l_ref[h] = l_new
                    acc_ref[h] = acc_new
                else:
                    l_ref[h] += l_new
                    acc_ref[h] += acc_new

        n_big = n_full // big
        rem = n_full - n_big * big
        for r in range(0, big, n_diag):
            @pl.when(rem == r)
            def _():
                first = n_full - r
                n_blocks = r + n_diag
                kblocks = kb_ref[0, pl.ds(first, n_blocks)].reshape(n_blocks * kb, WIDTH)
                k = jnp.concatenate([km_ref[...], kblocks], axis=0)
                pieces = [(0, N_META, vmt_ref[...], None)]
                pieces += [(N_META + t * kb, kb, vt_ref[0, first + t],
                            None if t < r else diag_masks[t - r]) for t in range(n_blocks)]
                key_pass(k, pieces, assign=True)

        def big_block(i, carry):
            first = i * big
            k = kb_ref[0, pl.ds(first, big)].reshape(big * kb, WIDTH)
            key_pass(k, [(t * kb, kb, vt_ref[0, first + t], None) for t in range(big)], assign=False)
            return carry

        lax.fori_loop(0, n_big, big_block, 0)

    @pl.when(bound_ref[0] > SAFE_SCORE_BOUND)
    def _():
        for h in range(N_HEADS):
            s = scores(k_slab(km_ref, h), h)
            m = jnp.max(s, axis=0, keepdims=True)
            e = jnp.exp2(s - m)
            m_ref[h:h + 1, :] = m
            l_ref[h] = jnp.zeros((8, lanes), F32)
            l_ref[h, 0:1, :] = jnp.sum(e, axis=0, keepdims=True)
            acc_ref[h] = jnp.dot(vmt_ref[h * V_DIM:(h + 1) * V_DIM, :], e.astype(BF16),
                                 preferred_element_type=F32)

        def block(i, mask):
            k = kb_ref[0, i]
            vt = vt_ref[0, i]
            for h in range(N_HEADS):
                s = scores(k_slab(k, h), h)
                if mask is not None:
                    s = jnp.where(mask, s, NEG_BIG)
                m_old = m_ref[h:h + 1, :]
                m_new = jnp.maximum(m_old, jnp.max(s, axis=0, keepdims=True))
                alpha = jnp.exp2(m_old - m_new)
                e = jnp.exp2(s - m_new)
                l_ref[h, 0:1, :] = alpha * l_ref[h, 0:1, :] + jnp.sum(e, axis=0, keepdims=True)
                acc_ref[h] = alpha * acc_ref[h] + jnp.dot(
                    vt[h * V_DIM:(h + 1) * V_DIM, :], e.astype(BF16), preferred_element_type=F32)
                m_ref[h:h + 1, :] = m_new

        def full_block(i, carry):
            block(i, None)
            return carry

        lax.fori_loop(0, n_full, full_block, 0)
        for d in range(n_diag):
            block(n_full + d, diag_masks[d])

    base = _conv_branch_out(x_ref[0], oc_ref[0], wout_ref)
    lam = _lambda(lq1_ref, lk1_ref, lq2_ref, lk2_ref)
    o_heads = []
    for h in range(N_HEADS):
        o2 = acc_ref[h] / jnp.sum(l_ref[h], axis=0, keepdims=True)
        o_heads.append((o2[:, 0:tq] - lam * o2[:, tq:lanes]).T)
    y_ref[0] = _attn_branch_out(base, o_heads, sga_ref[0], subg_ref[...], wout_ref)


def _prompt_attn(bound, x, q, kb4, vt4, km_bf, vmt_bf, sga, oc, subg, lq1, lk1, lq2, lk2, w_out_bf,
                 *, tq, kb, big):
    nb, s, d = x.shape
    nq = s // tq
    assert tq % kb == 0 and big % (tq // kb) == 0
    resident = dict(pipeline_mode=pl.Buffered(1))
    row = lambda b, j: (b, j, 0)
    const2 = lambda b, j: (0, 0)
    whole = lambda b, j: (b, 0, 0, 0)
    in_specs = [
        pl.BlockSpec(memory_space=pltpu.SMEM),
        pl.BlockSpec((1, tq, d), row),
        pl.BlockSpec((1, tq, WIDTH), row),
        pl.BlockSpec((1, s // kb, kb, WIDTH), whole, **resident),
        pl.BlockSpec((1, s // kb, WIDTH, kb), whole, **resident),
        pl.BlockSpec((N_META, WIDTH), const2),
        pl.BlockSpec((WIDTH, N_META), const2),
        pl.BlockSpec((1, tq, WIDTH), row),
        pl.BlockSpec((1, tq, WIDTH), row),
        pl.BlockSpec((1, V_DIM), const2),
        pl.BlockSpec((1, HEAD_DIM), const2),
        pl.BlockSpec((1, HEAD_DIM), const2),
        pl.BlockSpec((1, HEAD_DIM), const2),
        pl.BlockSpec((1, HEAD_DIM), const2),
        pl.BlockSpec(w_out_bf.shape, const2),
    ]
    return pl.pallas_call(
        functools.partial(_prompt_attn_kernel, tq=tq, kb=kb, big=big),
        grid=(nb, nq),
        in_specs=in_specs,
        out_specs=pl.BlockSpec((1, tq, d), row),
        out_shape=jax.ShapeDtypeStruct((nb, s, d), F32),
        scratch_shapes=[
            pltpu.VMEM((N_HEADS, 2 * tq, V_DIM), BF16),
            pltpu.VMEM((N_HEADS, 2 * tq), F32),
            pltpu.VMEM((N_HEADS, 8, 2 * tq), F32),
            pltpu.VMEM((N_HEADS, V_DIM, 2 * tq), F32),
        ],
        compiler_params=pltpu.CompilerParams(
            dimension_semantics=("arbitrary", "arbitrary"),
            vmem_limit_bytes=V7X_VMEM_LIMIT),
        name="prompt_attn",
    )(bound, x, q, kb4, vt4, km_bf, vmt_bf, sga, oc, subg, lq1, lk1, lq2, lk2, w_out_bf)


def _sample_attn_kernel(x_ref, q_ref, kn_ref, vn_ref, ck_ref, cv_ref, km_ref, vm_ref, sga_ref,
                        oc_ref, subg_ref, lq1_ref, lk1_ref, lq2_ref, lk2_ref, wout_ref, y_ref,
                        kcat, vcat, *, past, t_new, n_keys_pad):
    n_keys = past + N_META + t_new
    for h in range(N_HEADS):
        slab = slice(h * V_DIM, (h + 1) * V_DIM)
        kcat[0:past, slab] = ck_ref[0, pl.ds(h, past, stride=N_HEADS), :].astype(BF16)
        kcat[past + N_META:n_keys, slab] = kn_ref[0, pl.ds(h, t_new, stride=N_HEADS), :].astype(BF16)
        vcat[0:past, slab] = cv_ref[0, pl.ds(h, past, stride=N_HEADS), :].astype(BF16)
        vcat[past:past + N_META, slab] = vm_ref[pl.ds(h, N_META, stride=N_HEADS), :].astype(BF16)
        vcat[past + N_META:n_keys, slab] = vn_ref[0, pl.ds(h, t_new, stride=N_HEADS), :].astype(BF16)
    kcat[past:past + N_META, :] = km_ref[...]
    kcat[n_keys:n_keys_pad, :] = jnp.zeros((n_keys_pad - n_keys, WIDTH), BF16)
    vcat[n_keys:n_keys_pad, :] = jnp.zeros((n_keys_pad - n_keys, WIDTH), BF16)

    lane = lax.broadcasted_iota(jnp.int32, (1, V_DIM), 1)
    valid = lax.broadcasted_iota(jnp.int32, (1, n_keys_pad), 1) < n_keys
    lam = _lambda(lq1_ref, lk1_ref, lq2_ref, lk2_ref)
    o_heads = []
    for h in range(N_HEADS):
        qh = q_ref[0, :, h * V_DIM:(h + 1) * V_DIM]
        k_h = kcat[:, h * V_DIM:(h + 1) * V_DIM]
        v_h = vcat[:, h * V_DIM:(h + 1) * V_DIM]
        outs = []
        for comp in range(2):
            keep = (lane < HEAD_DIM) if comp == 0 else (lane >= HEAD_DIM)
            qc = jnp.where(keep, qh, jnp.zeros_like(qh))
            s = lax.dot_general(qc, k_h, (((1,), (1,)), ((), ())), preferred_element_type=F32)
            s = jnp.where(valid, s, NEG_BIG)
            m = jnp.max(s, axis=-1, keepdims=True)
            e = jnp.exp2(s - m)
            l = jnp.sum(e, axis=-1, keepdims=True)
            outs.append(jnp.dot(e.astype(BF16), v_h, preferred_element_type=F32) / l)
        o_heads.append(outs[0] - lam * outs[1])
    base = _conv_branch_out(x_ref[0], oc_ref[0], wout_ref)
    y_ref[0] = _attn_branch_out(base, o_heads, sga_ref[0], subg_ref[...], wout_ref)


def _sample_attn(x, q, kn, vn, ck, cv, km_bf, vm, sga, oc, subg, lq1, lk1, lq2, lk2, w_out_bf):
    nb, t_new, d = x.shape
    past = ck.shape[1] // N_HEADS
    n_keys_pad = -(-(past + N_META + t_new) // V_DIM) * V_DIM
    row = lambda b: (b, 0, 0)
    const2 = lambda b: (0, 0)
    in_specs = [
        pl.BlockSpec((1, t_new, d), row),
        pl.BlockSpec((1, t_new, WIDTH), row),
        pl.BlockSpec((1, t_new * N_HEADS, V_DIM), row),
        pl.BlockSpec((1, t_new * N_HEADS, V_DIM), row),
        pl.BlockSpec((1, past * N_HEADS, V_DIM), row),
        pl.BlockSpec((1, past * N_HEADS, V_DIM), row),
        pl.BlockSpec((N_META, WIDTH), const2),
        pl.BlockSpec((N_META * N_HEADS, V_DIM), const2),
        pl.BlockSpec((1, t_new, WIDTH), row),
        pl.BlockSpec((1, t_new, WIDTH), row),
        pl.BlockSpec((1, V_DIM), const2),
        pl.BlockSpec((1, HEAD_DIM), const2),
        pl.BlockSpec((1, HEAD_DIM), const2),
        pl.BlockSpec((1, HEAD_DIM), const2),
        pl.BlockSpec((1, HEAD_DIM), const2),
        pl.BlockSpec(w_out_bf.shape, const2),
    ]
    return pl.pallas_call(
        functools.partial(_sample_attn_kernel, past=past, t_new=t_new, n_keys_pad=n_keys_pad),
        grid=(nb,),
        in_specs=in_specs,
        out_specs=pl.BlockSpec((1, t_new, d), row),
        out_shape=jax.ShapeDtypeStruct((nb, t_new, d), F32),
        scratch_shapes=[pltpu.VMEM((n_keys_pad, WIDTH), BF16),
                        pltpu.VMEM((n_keys_pad, WIDTH), BF16)],
        compiler_params=pltpu.CompilerParams(
            dimension_semantics=("arbitrary",),
            vmem_limit_bytes=V7X_VMEM_LIMIT),
        name="sample_attn",
    )(x, q, kn, vn, ck, cv, km_bf, vm, sga, oc, subg, lq1, lk1, lq2, lk2, w_out_bf)


def _rope_tables(pos):
    half = ROT_DIM // 2
    inv = ROPE_THETA ** (-jnp.arange(0, ROT_DIM, 2, dtype=F32) / ROT_DIM)
    ang = pos.astype(F32)[:, None] * inv[None, :]
    cos, sin = lax.optimization_barrier((jnp.cos(ang), jnp.sin(ang)))
    t = pos.shape[0]
    rest = HEAD_DIM - 2 * half
    c64 = jnp.concatenate([cos, cos, jnp.ones((t, rest), F32)], axis=1)
    s64 = jnp.concatenate([-sin, sin, jnp.zeros((t, rest), F32)], axis=1)
    return jnp.tile(c64, (1, 2)), jnp.tile(s64, (1, 2))


def _score_bound(qg, kg):
    bound = Q_SCALE * HEAD_DIM * jnp.max(jnp.abs(qg)) * jnp.max(jnp.abs(kg)) * (1.0 + 2.0 ** -6)
    return bound.reshape(1).astype(F32)


def _pad_state(state):
    return jnp.pad(state, ((0, 0), (STATE_ROWS - (CONV_W - 1), 0), (0, 0)))


def _tile_sizes(seq):
    kb = 256 if seq % 256 == 0 else seq
    tq = 512 if seq % 512 == 0 else kb
    return tq, tq, kb


def kernel(x_prompt, x_sample, cache_k, cache_v, state_conv, meta_tokens, ln_g, w_in, q_norm_g,
           k_norm_g, lam_q1, lam_k1, lam_q2, lam_k2, subln_g, conv_w, conv_b, w_out):
    assert ln_g.shape[0] == 1, "single-layer stack"
    nb, seq, d = x_prompt.shape
    db, t_new, _ = x_sample.shape
    past = cache_k.shape[2]

    w_in_bf = w_in[0].astype(BF16)
    w_out_bf = w_out[0].astype(BF16)
    lng = ln_g[0][None, :]
    qg = jnp.tile(q_norm_g[0], 2 * N_HEADS)[None, :]
    kg = jnp.tile(k_norm_g[0], 2 * N_HEADS)[None, :]
    subg = subln_g[0][None, :]
    cw, cb = conv_w[0], conv_b[0][None, :]
    lams = [p[0][None, :] for p in (lam_q1, lam_k1, lam_q2, lam_k2)]
    grp = jnp.arange(GROUP_SUM_WIDTH, dtype=jnp.int32) // HEAD_DIM
    gmat = (grp[:, None] == grp[None, :]).astype(BF16)

    pos_m = jnp.arange(N_META, dtype=jnp.int32)
    pos_p = N_META + jnp.arange(seq, dtype=jnp.int32)
    pos_s = N_META + past + jnp.arange(t_new, dtype=jnp.int32)
    weights = (lng, w_in_bf, qg, kg, cw, cb, gmat)
    tm, tq, kb = _tile_sizes(seq)

    zero_state = jnp.zeros((1, STATE_ROWS, WIDTH), F32)
    _, km, vm, km_bf, _, _, tail_m = _proj_call(
        meta_tokens[None], zero_state, *_rope_tables(pos_m), *weights,
        tm=N_META, tkv=N_META, emit_vt=False)
    km, vm, km_bf = km[0], vm[0], km_bf[0]

    state_p = jnp.broadcast_to(tail_m, (nb, STATE_ROWS, WIDTH))
    qp, kp, vp, kp_bf, sga_p, oc_p, tail_p, vtp = _proj_call(
        x_prompt, state_p, *_rope_tables(pos_p), *weights, tm=tm, tkv=kb, emit_vt=True,
        row_offset=N_META)
    y_prompt = _prompt_attn(
        _score_bound(q_norm_g[0], k_norm_g[0]), x_prompt, qp,
        kp_bf.reshape(nb, seq // kb, kb, WIDTH), vtp, km_bf,
        vm.reshape(N_META, WIDTH).T.astype(BF16),
        sga_p, oc_p, subg, *lams, w_out_bf, tq=tq, kb=kb, big=BIG_KEY_BLOCKS)
    kp = kp.at[:, :N_META * N_HEADS].set(jnp.broadcast_to(km[None], (nb,) + km.shape))
    vp = vp.at[:, :N_META * N_HEADS].set(jnp.broadcast_to(vm[None], (nb,) + vm.shape))

    qs, ks, vs, _, sga_s, oc_s, tail_s = _proj_call(
        x_sample, _pad_state(state_conv[0]), *_rope_tables(pos_s), *weights,
        tm=t_new, tkv=t_new, emit_vt=False)
    y_sample = _sample_attn(
        x_sample, qs, ks, vs, cache_k[0].reshape(db, past * N_HEADS, V_DIM),
        cache_v[0].reshape(db, past * N_HEADS, V_DIM), km_bf, vm, sga_s, oc_s, subg, *lams, w_out_bf)

    def heads_form(a, rows):
        return a.reshape(1, a.shape[0], rows, N_HEADS, V_DIM)

    tail_rows = slice(STATE_ROWS - (CONV_W - 1), STATE_ROWS)
    return (y_prompt, y_sample, heads_form(kp, N_META + seq), heads_form(vp, N_META + seq),
            tail_p[None, :, tail_rows], heads_form(ks, t_new), heads_form(vs, t_new),
            tail_s[None, :, tail_rows])
```

```python
import functools
import math

import jax
import jax.numpy as jnp
from jax import lax
from jax.experimental import pallas as pl
from jax.experimental.pallas import tpu as pltpu

F32 = jnp.float32
BF16 = jnp.bfloat16

CHUNK = 64
N_META = 16
N_HEADS = 4
HEAD_DIM = 64
V_DIM = 2 * HEAD_DIM
WIDTH = N_HEADS * V_DIM
CONV_W = 3
ROT_DIM = HEAD_DIM // 4
ROPE_THETA = 500000.0
EPS = 1e-6
LAMBDA_INIT = 0.8 - 0.6 * math.exp(-0.3 * 0)
Q_SCALE = HEAD_DIM ** -0.5 * math.log2(math.e)
NEG_BIG = -1e30
SAFE_SCORE_BOUND = 40.0
BIG_KEY_BLOCKS = 8
GROUP_SUM_WIDTH = 256
STATE_ROWS = 8

V7X_VMEM_LIMIT = 56 * 1024 * 1024


def _silu(x):
    return x / (1.0 + jnp.exp(-x))


def _proj_kernel(x_ref, st_ref, cos_ref, sin_ref, lng_ref, win_ref, qg_ref, kg_ref, cw_ref, cb_ref,
                 gmat_ref, q_ref, kf_ref, vf_ref, kb_ref, sga_ref, oc_ref, tail_ref, *rest,
                 sb, tm, tkv, emit_vt):
    if emit_vt:
        vt_ref, ubuf = rest
    else:
        (ubuf,) = rest
    j = pl.program_id(1)
    rows = sb * tm

    @pl.when(j == 0)
    def _():
        ubuf[:, 0:STATE_ROWS, :] = st_ref[...]

    @pl.when(j > 0)
    def _():
        ubuf[:, 0:STATE_ROWS, :] = ubuf[:, tm:tm + STATE_ROWS, :]

    x = x_ref[...].reshape(rows, x_ref.shape[-1])
    xg = (x * lng_ref[...]).astype(BF16)
    inv_rms = lax.rsqrt(jnp.mean(x * x, axis=-1, keepdims=True) + EPS)

    def proj(i):
        return inv_rms * jnp.dot(xg, win_ref[:, i * WIDTH:(i + 1) * WIDTH],
                                 preferred_element_type=F32)

    def group_mean_sq(t):
        tt = (t * t).astype(BF16)
        halves = [jnp.dot(tt[:, i * GROUP_SUM_WIDTH:(i + 1) * GROUP_SUM_WIDTH], gmat_ref[...],
                          preferred_element_type=F32) for i in range(WIDTH // GROUP_SUM_WIDTH)]
        return jnp.concatenate(halves, axis=1) * (1.0 / HEAD_DIM)

    cos = cos_ref[...]
    sin = sin_ref[...]
    lane = lax.broadcasted_iota(jnp.int32, (1, V_DIM), 1) % HEAD_DIM
    take_lower = lane >= ROT_DIM // 2

    def norm_rope(t, mean_sq, g):
        tn = t * lax.rsqrt(mean_sq + EPS) * g
        outs = []
        for h in range(N_HEADS):
            th = tn[:, h * V_DIM:(h + 1) * V_DIM]
            lower = pltpu.roll(th, ROT_DIM // 2, axis=1)
            upper = pltpu.roll(th, V_DIM - ROT_DIM // 2, axis=1)
            outs.append(th * cos + jnp.where(take_lower, lower, upper) * sin)
        return jnp.concatenate(outs, axis=1)

    def per_stream(a):
        return a.reshape(sb, tm, a.shape[-1])

    pq = proj(0)
    pk = proj(1)
    msq = group_mean_sq(pq)
    v = proj(2)
    msk = group_mean_sq(pk)
    ga = proj(3)
    q_ref[...] = per_stream((norm_rope(pq, msq, qg_ref[...]) * Q_SCALE).astype(BF16))
    bb = proj(4)
    k = norm_rope(pk, msk, kg_ref[...])
    kb_ref[...] = per_stream(k.astype(BF16))
    cc = proj(5)
    for b in range(sb):
        for h in range(N_HEADS):
            src = (slice(b * tm, (b + 1) * tm), slice(h * V_DIM, (h + 1) * V_DIM))
            kf_ref[b, pl.ds(h, tm, stride=N_HEADS), :] = k[src]
            vf_ref[b, pl.ds(h, tm, stride=N_HEADS), :] = v[src]
    hh = proj(6)
    if emit_vt:
        for s in range(tm // tkv):
            vt_ref[0, s] = v[s * tkv:(s + 1) * tkv, :].T.astype(BF16)
    gc = proj(7)
    sga_ref[...] = per_stream(_silu(ga).astype(BF16))

    u = cc * hh
    ys = []
    for b in range(sb):
        ubuf[b, STATE_ROWS:STATE_ROWS + tm, :] = u[b * tm:(b + 1) * tm, :]
        y = cb_ref[...]
        for tap in range(CONV_W):
            off = STATE_ROWS - (CONV_W - 1) + tap
            y = y + cw_ref[tap:tap + 1, :] * ubuf[b, off:off + tm, :]
        ys.append(y)
    y = ys[0] if sb == 1 else jnp.concatenate(ys, axis=0)
    oc_ref[...] = per_stream((_silu(gc) * (bb * y)).astype(BF16))
    tail_ref[...] = ubuf[:, tm:tm + STATE_ROWS, :]


def _proj_call(x, state8, cos_t, sin_t, ln_g, w_in_bf, qg, kg, conv_w, conv_b, gmat, *, sb, tm, tkv,
               emit_vt, row_offset=0):
    nb, t, d = x.shape
    nt = t // tm
    assert t % tm == 0 and tm % tkv == 0 and nb % sb == 0 and not (emit_vt and sb > 1)
    assert cos_t.shape[0] == sb * t
    row = lambda b, j: (b, j, 0)
    const2 = lambda b, j: (0, 0)
    in_specs = [
        pl.BlockSpec((sb, tm, d), row),
        pl.BlockSpec((sb, STATE_ROWS, WIDTH), lambda b, j: (b, 0, 0)),
        pl.BlockSpec((sb * tm, V_DIM), lambda b, j: (j, 0)),
        pl.BlockSpec((sb * tm, V_DIM), lambda b, j: (j, 0)),
        pl.BlockSpec((1, d), const2),
        pl.BlockSpec(w_in_bf.shape, const2),
        pl.BlockSpec((1, WIDTH), const2),
        pl.BlockSpec((1, WIDTH), const2),
        pl.BlockSpec((CONV_W, WIDTH), const2),
        pl.BlockSpec((1, WIDTH), const2),
        pl.BlockSpec((GROUP_SUM_WIDTH, GROUP_SUM_WIDTH), const2),
    ]
    tile = pl.BlockSpec((sb, tm, WIDTH), row)
    heads_tile = pl.BlockSpec(
        (pl.Element(sb), pl.Element(tm * N_HEADS), pl.Element(V_DIM)),
        lambda b, j: (b * sb, pl.multiple_of((row_offset + j * tm) * N_HEADS, 8), 0))
    out_shape = [
        jax.ShapeDtypeStruct((nb, t, WIDTH), BF16),
        jax.ShapeDtypeStruct((nb, (row_offset + t) * N_HEADS, V_DIM), F32),
        jax.ShapeDtypeStruct((nb, (row_offset + t) * N_HEADS, V_DIM), F32),
        jax.ShapeDtypeStruct((nb, t, WIDTH), BF16),
        jax.ShapeDtypeStruct((nb, t, WIDTH), BF16),
        jax.ShapeDtypeStruct((nb, t, WIDTH), BF16),
        jax.ShapeDtypeStruct((nb, STATE_ROWS, WIDTH), F32),
    ]
    out_specs = [tile, heads_tile, heads_tile, tile, tile, tile,
                 pl.BlockSpec((sb, STATE_ROWS, WIDTH), lambda b, j: (b, 0, 0))]
    if emit_vt:
        out_shape.append(jax.ShapeDtypeStruct((nb, t // tkv, WIDTH, tkv), BF16))
        out_specs.append(pl.BlockSpec((1, tm // tkv, WIDTH, tkv), lambda b, j: (b, j, 0, 0)))
    return pl.pallas_call(
        functools.partial(_proj_kernel, sb=sb, tm=tm, tkv=tkv, emit_vt=emit_vt),
        grid=(nb // sb, nt),
        in_specs=in_specs,
        out_specs=out_specs,
        out_shape=out_shape,
        scratch_shapes=[pltpu.VMEM((sb, tm + STATE_ROWS, WIDTH), F32)],
        compiler_params=pltpu.CompilerParams(
            dimension_semantics=("arbitrary", "arbitrary"),
            vmem_limit_bytes=V7X_VMEM_LIMIT),
        name="proj",
    )(x, state8, cos_t, sin_t, ln_g, w_in_bf, qg, kg, conv_w, conv_b, gmat)


def _lambda(lq1_ref, lk1_ref, lq2_ref, lk2_ref):
    a = jnp.sum(lq1_ref[...] * lk1_ref[...], axis=-1, keepdims=True)
    b = jnp.sum(lq2_ref[...] * lk2_ref[...], axis=-1, keepdims=True)
    return jnp.exp(a) - jnp.exp(b) + LAMBDA_INIT


def _conv_branch_out(x, oc, wout_ref):
    return x + jnp.dot(oc, wout_ref[WIDTH:, :], preferred_element_type=F32)


def _attn_branch_out(base, o_heads, sga, subg, wout_ref):
    parts = []
    for h, o in enumerate(o_heads):
        ms = jnp.mean(o * o, axis=-1, keepdims=True)
        on = o * lax.rsqrt(ms + EPS) * subg * (1.0 - LAMBDA_INIT)
        parts.append((sga[:, h * V_DIM:(h + 1) * V_DIM].astype(F32) * on).astype(BF16))
    mix = jnp.concatenate(parts, axis=1)
    return base + jnp.dot(mix, wout_ref[:WIDTH, :], preferred_element_type=F32)


def _prompt_attn_kernel(bound_ref, x_ref, q_ref, kb_ref, vt_ref, km_ref, vmt_ref, sga_ref, oc_ref,
                        subg_ref, lq1_ref, lk1_ref, lq2_ref, lk2_ref, wout_ref, y_ref,
                        qz_ref, m_ref, l_ref, acc_ref, *, tq, kb, big):
    j = pl.program_id(1)
    lanes = 2 * tq
    n_diag = tq // kb
    n_full = j * n_diag

    lane = lax.broadcasted_iota(jnp.int32, (1, V_DIM), 1)
    for h in range(N_HEADS):
        qh = q_ref[0, :, h * V_DIM:(h + 1) * V_DIM]
        qz_ref[h, 0:tq, :] = jnp.where(lane < HEAD_DIM, qh, jnp.zeros_like(qh))
        qz_ref[h, tq:lanes, :] = jnp.where(lane >= HEAD_DIM, qh, jnp.zeros_like(qh))

    def scores(k_h, h):
        return lax.dot_general(k_h, qz_ref[h], (((1,), (1,)), ((), ())), preferred_element_type=F32)

    def k_slab(k, h):
        return k[:, h * V_DIM:(h + 1) * V_DIM]

    def sublane_partial(e):
        return jnp.sum(e.reshape(e.shape[0] // 8, 8, lanes), axis=0)

    kchunk = lax.broadcasted_iota(jnp.int32, (kb, 1), 0) // CHUNK
    qchunk = (lax.broadcasted_iota(jnp.int32, (1, lanes), 1) % tq) // CHUNK
    diag_masks = [kchunk + d * (kb // CHUNK) <= qchunk for d in range(n_diag)]

    def late_queries(a, d, axis):
        lo = d * kb
        if axis == 0:
            return jnp.concatenate([a[lo:tq], a[tq + lo:lanes]], axis=0)
        return jnp.concatenate([a[:, lo:tq], a[:, tq + lo:lanes]], axis=1)

    def add_late(full, part, d):
        lo, w = d * kb, tq - d * kb
        return jnp.concatenate([full[:, 0:lo], full[:, lo:tq] + part[:, 0:w],
                                full[:, tq:tq + lo], full[:, tq + lo:lanes] + part[:, w:2 * w]],
                               axis=1)

    @pl.when(bound_ref[0] <= SAFE_SCORE_BOUND)
    def _():
        def key_pass(k, pieces, assign, late_blocks=()):
            s_next = scores(k_slab(k, 0), 0)
            for h in range(N_HEADS):
                s = s_next
                if h + 1 < N_HEADS:
                    s_next = scores(k_slab(k, h + 1), h + 1)
                s_late = [lax.dot_general(k_slab(kl, h), late_queries(qz_ref[h], d, 0),
                                          (((1,), (1,)), ((), ())), preferred_element_type=F32)
                          for d, kl, _ in late_blocks]
                e = jnp.exp2(s)
                l_new = acc_new = None
                for row0, rows, vt, mask in pieces:
                    ep = e[row0:row0 + rows, :]
                    if mask is not None:
                        ep = jnp.where(mask, ep, 0.0)
                    lp = sublane_partial(ep)
                    ap = jnp.dot(vt[h * V_DIM:(h + 1) * V_DIM, :], ep.astype(BF16),
                                 preferred_element_type=F32)
                    l_new = lp if l_new is None else l_new + lp
                    acc_new = ap if acc_new is None else acc_new + ap
                for (d, _, vt), sl in zip(late_blocks, s_late):
                    ep = jnp.where(late_queries(diag_masks[d], d, 1), jnp.exp2(sl), 0.0)
                    lp = jnp.sum(ep.reshape(kb // 8, 8, ep.shape[1]), axis=0)
                    ap = jnp.dot(vt[h * V_DIM:(h + 1) * V_DIM, :], ep.astype(BF16),
                                 preferred_element_type=F32)
                    l_new = add_late(l_new, lp, d)
                    acc_new = add_late(acc_new, ap, d)
                if assign:
                    l_ref[h] = l_new
                    acc_ref[h] = acc_new
                else:
                    l_ref[h] += l_new
                    acc_ref[h] += acc_new

        n_big = n_full // big
        rem = n_full - n_big * big
        for r in range(0, big, n_diag):
            @pl.when(rem == r)
            def _():
                first = n_full - r
                n_blocks = r + 1
                kblocks = kb_ref[0, pl.ds(first, n_blocks)].reshape(n_blocks * kb, WIDTH)
                k = jnp.concatenate([km_ref[...], kblocks], axis=0)
                pieces = [(0, N_META, vmt_ref[...], None)]
                pieces += [(N_META + t * kb, kb, vt_ref[0, first + t],
                            None if t < r else diag_masks[0]) for t in range(n_blocks)]
                late = [(d, kb_ref[0, n_full + d], vt_ref[0, n_full + d])
                        for d in range(1, n_diag)]
                key_pass(k, pieces, assign=True, late_blocks=late)

        def big_block(i, carry):
            first = i * big
            k = kb_ref[0, pl.ds(first, big)].reshape(big * kb, WIDTH)
            key_pass(k, [(t * kb, kb, vt_ref[0, first + t], None) for t in range(big)], assign=False)
            return carry

        lax.fori_loop(0, n_big, big_block, 0)

    @pl.when(bound_ref[0] > SAFE_SCORE_BOUND)
    def _():
        for h in range(N_HEADS):
            s = scores(k_slab(km_ref, h), h)
            m = jnp.max(s, axis=0, keepdims=True)
            e = jnp.exp2(s - m)
            m_ref[h:h + 1, :] = m
            l_ref[h] = jnp.zeros((8, lanes), F32)
            l_ref[h, 0:1, :] = jnp.sum(e, axis=0, keepdims=True)
            acc_ref[h] = jnp.dot(vmt_ref[h * V_DIM:(h + 1) * V_DIM, :], e.astype(BF16),
                                 preferred_element_type=F32)

        def block(i, mask):
            k = kb_ref[0, i]
            vt = vt_ref[0, i]
            for h in range(N_HEADS):
                s = scores(k_slab(k, h), h)
                if mask is not None:
                    s = jnp.where(mask, s, NEG_BIG)
                m_old = m_ref[h:h + 1, :]
                m_new = jnp.maximum(m_old, jnp.max(s, axis=0, keepdims=True))
                alpha = jnp.exp2(m_old - m_new)
                e = jnp.exp2(s - m_new)
                l_ref[h, 0:1, :] = alpha * l_ref[h, 0:1, :] + jnp.sum(e, axis=0, keepdims=True)
                acc_ref[h] = alpha * acc_ref[h] + jnp.dot(
                    vt[h * V_DIM:(h + 1) * V_DIM, :], e.astype(BF16), preferred_element_type=F32)
                m_ref[h:h + 1, :] = m_new

        def full_block(i, carry):
            block(i, None)
            return carry

        lax.fori_loop(0, n_full, full_block, 0)
        for d in range(n_diag):
            block(n_full + d, diag_masks[d])

    base = _conv_branch_out(x_ref[0], oc_ref[0], wout_ref)
    lam = _lambda(lq1_ref, lk1_ref, lq2_ref, lk2_ref)
    o_heads = []
    for h in range(N_HEADS):
        o2 = acc_ref[h] / jnp.sum(l_ref[h], axis=0, keepdims=True)
        o_heads.append((o2[:, 0:tq] - lam * o2[:, tq:lanes]).T)
    y_ref[0] = _attn_branch_out(base, o_heads, sga_ref[0], subg_ref[...], wout_ref)


def _prompt_attn(bound, x, q, kb4, vt4, km_bf, vmt_bf, sga, oc, subg, lq1, lk1, lq2, lk2, w_out_bf,
                 *, tq, kb, big):
    nb, s, d = x.shape
    nq = s // tq
    assert tq % kb == 0 and big % (tq // kb) == 0
    resident = dict(pipeline_mode=pl.Buffered(1))
    row = lambda b, j: (b, j, 0)
    const2 = lambda b, j: (0, 0)
    whole = lambda b, j: (b, 0, 0, 0)
    in_specs = [
        pl.BlockSpec(memory_space=pltpu.SMEM),
        pl.BlockSpec((1, tq, d), row),
        pl.BlockSpec((1, tq, WIDTH), row),
        pl.BlockSpec((1, s // kb, kb, WIDTH), whole, **resident),
        pl.BlockSpec((1, s // kb, WIDTH, kb), whole, **resident),
        pl.BlockSpec((N_META, WIDTH), const2),
        pl.BlockSpec((WIDTH, N_META), const2),
        pl.BlockSpec((1, tq, WIDTH), row),
        pl.BlockSpec((1, tq, WIDTH), row),
        pl.BlockSpec((1, V_DIM), const2),
        pl.BlockSpec((1, HEAD_DIM), const2),
        pl.BlockSpec((1, HEAD_DIM), const2),
        pl.BlockSpec((1, HEAD_DIM), const2),
        pl.BlockSpec((1, HEAD_DIM), const2),
        pl.BlockSpec(w_out_bf.shape, const2),
    ]
    return pl.pallas_call(
        functools.partial(_prompt_attn_kernel, tq=tq, kb=kb, big=big),
        grid=(nb, nq),
        in_specs=in_specs,
        out_specs=pl.BlockSpec((1, tq, d), row),
        out_shape=jax.ShapeDtypeStruct((nb, s, d), F32),
        scratch_shapes=[
            pltpu.VMEM((N_HEADS, 2 * tq, V_DIM), BF16),
            pltpu.VMEM((N_HEADS, 2 * tq), F32),
            pltpu.VMEM((N_HEADS, 8, 2 * tq), F32),
            pltpu.VMEM((N_HEADS, V_DIM, 2 * tq), F32),
        ],
        compiler_params=pltpu.CompilerParams(
            dimension_semantics=("arbitrary", "arbitrary"),
            vmem_limit_bytes=V7X_VMEM_LIMIT),
        name="prompt_attn",
    )(bound, x, q, kb4, vt4, km_bf, vmt_bf, sga, oc, subg, lq1, lk1, lq2, lk2, w_out_bf)


def _sample_attn_kernel(x_ref, q_ref, kn_ref, vn_ref, ck_ref, cv_ref, km_ref, vm_ref, sga_ref,
                        oc_ref, subg_ref, lq1_ref, lk1_ref, lq2_ref, lk2_ref, wout_ref, y_ref,
                        kcat, vcat, *, past, t_new, n_keys_pad):
    n_keys = past + N_META + t_new
    for h in range(N_HEADS):
        slab = slice(h * V_DIM, (h + 1) * V_DIM)
        kcat[0:past, slab] = ck_ref[0, pl.ds(h, past, stride=N_HEADS), :].astype(BF16)
        kcat[past + N_META:n_keys, slab] = kn_ref[0, pl.ds(h, t_new, stride=N_HEADS), :].astype(BF16)
        vcat[0:past, slab] = cv_ref[0, pl.ds(h, past, stride=N_HEADS), :].astype(BF16)
        vcat[past:past + N_META, slab] = vm_ref[pl.ds(h, N_META, stride=N_HEADS), :].astype(BF16)
        vcat[past + N_META:n_keys, slab] = vn_ref[0, pl.ds(h, t_new, stride=N_HEADS), :].astype(BF16)
    kcat[past:past + N_META, :] = km_ref[...]
    kcat[n_keys:n_keys_pad, :] = jnp.zeros((n_keys_pad - n_keys, WIDTH), BF16)
    vcat[n_keys:n_keys_pad, :] = jnp.zeros((n_keys_pad - n_keys, WIDTH), BF16)

    lane = lax.broadcasted_iota(jnp.int32, (1, V_DIM), 1)
    valid = lax.broadcasted_iota(jnp.int32, (1, n_keys_pad), 1) < n_keys
    lam = _lambda(lq1_ref, lk1_ref, lq2_ref, lk2_ref)
    o_heads = []
    for h in range(N_HEADS):
        qh = q_ref[0, :, h * V_DIM:(h + 1) * V_DIM]
        k_h = kcat[:, h * V_DIM:(h + 1) * V_DIM]
        v_h = vcat[:, h * V_DIM:(h + 1) * V_DIM]
        outs = []
        for comp in range(2):
            keep = (lane < HEAD_DIM) if comp == 0 else (lane >= HEAD_DIM)
            qc = jnp.where(keep, qh, jnp.zeros_like(qh))
            s = lax.dot_general(qc, k_h, (((1,), (1,)), ((), ())), preferred_element_type=F32)
            s = jnp.where(valid, s, NEG_BIG)
            m = jnp.max(s, axis=-1, keepdims=True)
            e = jnp.exp2(s - m)
            l = jnp.sum(e, axis=-1, keepdims=True)
            outs.append(jnp.dot(e.astype(BF16), v_h, preferred_element_type=F32) / l)
        o_heads.append(outs[0] - lam * outs[1])
    base = _conv_branch_out(x_ref[0], oc_ref[0], wout_ref)
    y_ref[0] = _attn_branch_out(base, o_heads, sga_ref[0], subg_ref[...], wout_ref)


def _sample_attn(x, q, kn, vn, ck, cv, km_bf, vm, sga, oc, subg, lq1, lk1, lq2, lk2, w_out_bf):
    nb, t_new, d = x.shape
    past = ck.shape[1] // N_HEADS
    n_keys_pad = -(-(past + N_META + t_new) // V_DIM) * V_DIM
    row = lambda b: (b, 0, 0)
    const2 = lambda b: (0, 0)
    in_specs = [
        pl.BlockSpec((1, t_new, d), row),
        pl.BlockSpec((1, t_new, WIDTH), row),
        pl.BlockSpec((1, t_new * N_HEADS, V_DIM), row),
        pl.BlockSpec((1, t_new * N_HEADS, V_DIM), row),
        pl.BlockSpec((1, past * N_HEADS, V_DIM), row),
        pl.BlockSpec((1, past * N_HEADS, V_DIM), row),
        pl.BlockSpec((N_META, WIDTH), const2),
        pl.BlockSpec((N_META * N_HEADS, V_DIM), const2),
        pl.BlockSpec((1, t_new, WIDTH), row),
        pl.BlockSpec((1, t_new, WIDTH), row),
        pl.BlockSpec((1, V_DIM), const2),
        pl.BlockSpec((1, HEAD_DIM), const2),
        pl.BlockSpec((1, HEAD_DIM), const2),
        pl.BlockSpec((1, HEAD_DIM), const2),
        pl.BlockSpec((1, HEAD_DIM), const2),
        pl.BlockSpec(w_out_bf.shape, const2),
    ]
    return pl.pallas_call(
        functools.partial(_sample_attn_kernel, past=past, t_new=t_new, n_keys_pad=n_keys_pad),
        grid=(nb,),
        in_specs=in_specs,
        out_specs=pl.BlockSpec((1, t_new, d), row),
        out_shape=jax.ShapeDtypeStruct((nb, t_new, d), F32),
        scratch_shapes=[pltpu.VMEM((n_keys_pad, WIDTH), BF16),
                        pltpu.VMEM((n_keys_pad, WIDTH), BF16)],
        compiler_params=pltpu.CompilerParams(
            dimension_semantics=("arbitrary",),
            vmem_limit_bytes=V7X_VMEM_LIMIT),
        name="sample_attn",
    )(x, q, kn, vn, ck, cv, km_bf, vm, sga, oc, subg, lq1, lk1, lq2, lk2, w_out_bf)


def _rope_tables(pos):
    half = ROT_DIM // 2
    inv = ROPE_THETA ** (-jnp.arange(0, ROT_DIM, 2, dtype=F32) / ROT_DIM)
    ang = pos.astype(F32)[:, None] * inv[None, :]
    cos, sin = lax.optimization_barrier((jnp.cos(ang), jnp.sin(ang)))
    t = pos.shape[0]
    rest = HEAD_DIM - 2 * half
    c64 = jnp.concatenate([cos, cos, jnp.ones((t, rest), F32)], axis=1)
    s64 = jnp.concatenate([-sin, sin, jnp.zeros((t, rest), F32)], axis=1)
    return jnp.tile(c64, (1, 2)), jnp.tile(s64, (1, 2))


def _score_bound(qg, kg):
    bound = Q_SCALE * HEAD_DIM * jnp.max(jnp.abs(qg)) * jnp.max(jnp.abs(kg)) * (1.0 + 2.0 ** -6)
    return bound.reshape(1).astype(F32)


def _pad_state(state):
    return jnp.pad(state, ((0, 0), (STATE_ROWS - (CONV_W - 1), 0), (0, 0)))


def _tile_sizes(seq):
    kb = 256 if seq % 256 == 0 else seq
    tq = 512 if seq % 512 == 0 else kb
    return tq, tq, kb


def kernel(x_prompt, x_sample, cache_k, cache_v, state_conv, meta_tokens, ln_g, w_in, q_norm_g,
           k_norm_g, lam_q1, lam_k1, lam_q2, lam_k2, subln_g, conv_w, conv_b, w_out):
    assert ln_g.shape[0] == 1, "single-layer stack"
    nb, seq, d = x_prompt.shape
    db, t_new, _ = x_sample.shape
    past = cache_k.shape[2]

    w_in_bf = w_in[0].astype(BF16)
    w_out_bf = w_out[0].astype(BF16)
    lng = ln_g[0][None, :]
    qg = jnp.tile(q_norm_g[0], 2 * N_HEADS)[None, :]
    kg = jnp.tile(k_norm_g[0], 2 * N_HEADS)[None, :]
    subg = subln_g[0][None, :]
    cw, cb = conv_w[0], conv_b[0][None, :]
    lams = [p[0][None, :] for p in (lam_q1, lam_k1, lam_q2, lam_k2)]
    grp = jnp.arange(GROUP_SUM_WIDTH, dtype=jnp.int32) // HEAD_DIM
    gmat = (grp[:, None] == grp[None, :]).astype(BF16)

    pos_m = jnp.arange(N_META, dtype=jnp.int32)
    pos_p = N_META + jnp.arange(seq, dtype=jnp.int32)
    pos_s = N_META + past + jnp.arange(t_new, dtype=jnp.int32)
    weights = (lng, w_in_bf, qg, kg, cw, cb, gmat)
    tm, tq, kb = _tile_sizes(seq)

    assert t_new == N_META, "meta tokens and new frames are projected as equal-length streams"
    cos_m, sin_m = _rope_tables(pos_m)
    cos_s, sin_s = _rope_tables(pos_s)
    n_small = 1 + db
    q_sm, k_sm, v_sm, kb_sm, sga_sm, oc_sm, tail_sm = _proj_call(
        jnp.concatenate([meta_tokens[None], x_sample], axis=0),
        jnp.concatenate([jnp.zeros((1, STATE_ROWS, WIDTH), F32), _pad_state(state_conv[0])], axis=0),
        jnp.concatenate([cos_m] + [cos_s] * db, axis=0),
        jnp.concatenate([sin_m] + [sin_s] * db, axis=0),
        *weights, sb=n_small, tm=t_new, tkv=t_new, emit_vt=False)
    km, vm, km_bf, tail_m = k_sm[0], v_sm[0], kb_sm[0], tail_sm[0:1]
    qs, ks, vs, sga_s, oc_s, tail_s = (a[1:] for a in (q_sm, k_sm, v_sm, sga_sm, oc_sm, tail_sm))

    state_p = jnp.broadcast_to(tail_m, (nb, STATE_ROWS, WIDTH))
    qp, kp, vp, kp_bf, sga_p, oc_p, tail_p, vtp = _proj_call(
        x_prompt, state_p, *_rope_tables(pos_p), *weights, sb=1, tm=tm, tkv=kb, emit_vt=True,
        row_offset=N_META)
    y_prompt = _prompt_attn(
        _score_bound(q_norm_g[0], k_norm_g[0]), x_prompt, qp,
        kp_bf.reshape(nb, seq // kb, kb, WIDTH), vtp, km_bf,
        vm.reshape(N_META, WIDTH).T.astype(BF16),
        sga_p, oc_p, subg, *lams, w_out_bf, tq=tq, kb=kb, big=BIG_KEY_BLOCKS)
    kp = kp.at[:, :N_META * N_HEADS].set(jnp.broadcast_to(km[None], (nb,) + km.shape))
    vp = vp.at[:, :N_META * N_HEADS].set(jnp.broadcast_to(vm[None], (nb,) + vm.shape))

    y_sample = _sample_attn(
        x_sample, qs, ks, vs, cache_k[0].reshape(db, past * N_HEADS, V_DIM),
        cache_v[0].reshape(db, past * N_HEADS, V_DIM), km_bf, vm, sga_s, oc_s, subg, *lams, w_out_bf)

    def heads_form(a, rows):
        return a.reshape(1, a.shape[0], rows, N_HEADS, V_DIM)

    tail_rows = slice(STATE_ROWS - (CONV_W - 1), STATE_ROWS)
    return (y_prompt, y_sample, heads_form(kp, N_META + seq), heads_form(vp, N_META + seq),
            tail_p[None, :, tail_rows], heads_form(ks, t_new), heads_form(vs, t_new),
            tail_s[None, :, tail_rows])
```

```python
import functools
import math

import jax
import jax.numpy as jnp
from jax import lax
from jax.experimental import pallas as pl
from jax.experimental.pallas import tpu as pltpu

F32 = jnp.float32
BF16 = jnp.bfloat16

CHUNK = 64
N_META = 16
N_HEADS = 4
HEAD_DIM = 64
V_DIM = 2 * HEAD_DIM
WIDTH = N_HEADS * V_DIM
CONV_W = 3
ROT_DIM = HEAD_DIM // 4
ROPE_THETA = 500000.0
EPS = 1e-6
LAMBDA_INIT = 0.8 - 0.6 * math.exp(-0.3 * 0)
Q_SCALE = HEAD_DIM ** -0.5 * math.log2(math.e)
NEG_BIG = -1e30
SAFE_SCORE_BOUND = 40.0
BIG_KEY_BLOCKS = 8
GROUP_SUM_WIDTH = 256
STATE_ROWS = 8

V7X_VMEM_LIMIT = 56 * 1024 * 1024


def _silu(x):
    return x / (1.0 + jnp.exp(-x))


def _proj_kernel(x_ref, st_ref, cos_ref, sin_ref, lng_ref, win_ref, qg_ref, kg_ref, cw_ref, cb_ref,
                 gmat_ref, q_ref, kf_ref, vf_ref, kb_ref, sga_ref, oc_ref, tail_ref, *rest,
                 sb, tm, tkv, emit_vt):
    if emit_vt:
        vt_ref, ubuf = rest
    else:
        (ubuf,) = rest
    j = pl.program_id(1)
    rows = sb * tm

    @pl.when(j == 0)
    def _():
        ubuf[:, 0:STATE_ROWS, :] = st_ref[...]

    @pl.when(j > 0)
    def _():
        ubuf[:, 0:STATE_ROWS, :] = ubuf[:, tm:tm + STATE_ROWS, :]

    x = x_ref[...].reshape(rows, x_ref.shape[-1])
    xg = (x * lng_ref[...]).astype(BF16)
    inv_rms = lax.rsqrt(jnp.mean(x * x, axis=-1, keepdims=True) + EPS)

    def proj(i):
        return inv_rms * jnp.dot(xg, win_ref[:, i * WIDTH:(i + 1) * WIDTH],
                                 preferred_element_type=F32)

    def group_mean_sq(t):
        tt = (t * t).astype(BF16)
        halves = [jnp.dot(tt[:, i * GROUP_SUM_WIDTH:(i + 1) * GROUP_SUM_WIDTH], gmat_ref[...],
                          preferred_element_type=F32) for i in range(WIDTH // GROUP_SUM_WIDTH)]
        return jnp.concatenate(halves, axis=1) * (1.0 / HEAD_DIM)

    cos = cos_ref[...]
    sin = sin_ref[...]
    lane = lax.broadcasted_iota(jnp.int32, (1, V_DIM), 1) % HEAD_DIM
    take_lower = lane >= ROT_DIM // 2

    def norm_rope(t, mean_sq, g):
        tn = t * lax.rsqrt(mean_sq + EPS) * g
        outs = []
        for h in range(N_HEADS):
            th = tn[:, h * V_DIM:(h + 1) * V_DIM]
            lower = pltpu.roll(th, ROT_DIM // 2, axis=1)
            upper = pltpu.roll(th, V_DIM - ROT_DIM // 2, axis=1)
            outs.append(th * cos + jnp.where(take_lower, lower, upper) * sin)
        return jnp.concatenate(outs, axis=1)

    def per_stream(a):
        return a.reshape(sb, tm, a.shape[-1])

    pq = proj(0)
    pk = proj(1)
    msq = group_mean_sq(pq)
    v = proj(2)
    msk = group_mean_sq(pk)
    ga = proj(3)
    q_ref[...] = per_stream((norm_rope(pq, msq, qg_ref[...]) * Q_SCALE).astype(BF16))
    bb = proj(4)
    k = norm_rope(pk, msk, kg_ref[...])
    kb_ref[...] = per_stream(k.astype(BF16))
    cc = proj(5)
    for b in range(sb):
        for h in range(N_HEADS):
            src = (slice(b * tm, (b + 1) * tm), slice(h * V_DIM, (h + 1) * V_DIM))
            kf_ref[b, pl.ds(h, tm, stride=N_HEADS), :] = k[src]
            vf_ref[b, pl.ds(h, tm, stride=N_HEADS), :] = v[src]
    hh = proj(6)
    if emit_vt:
        for s in range(tm // tkv):
            vt_ref[0, s] = v[s * tkv:(s + 1) * tkv, :].T.astype(BF16)
    gc = proj(7)
    sga_ref[...] = per_stream(_silu(ga).astype(BF16))

    u = cc * hh
    ys = []
    for b in range(sb):
        ubuf[b, STATE_ROWS:STATE_ROWS + tm, :] = u[b * tm:(b + 1) * tm, :]
        y = cb_ref[...]
        for tap in range(CONV_W):
            off = STATE_ROWS - (CONV_W - 1) + tap
            y = y + cw_ref[tap:tap + 1, :] * ubuf[b, off:off + tm, :]
        ys.append(y)
    y = ys[0] if sb == 1 else jnp.concatenate(ys, axis=0)
    oc_ref[...] = per_stream((_silu(gc) * (bb * y)).astype(BF16))
    tail_ref[...] = ubuf[:, tm:tm + STATE_ROWS, :]


def _proj_call(x, state8, cos_t, sin_t, ln_g, w_in_bf, qg, kg, conv_w, conv_b, gmat, *, sb, tm, tkv,
               emit_vt, row_offset=0):
    nb, t, d = x.shape
    nt = t // tm
    assert t % tm == 0 and tm % tkv == 0 and nb % sb == 0 and not (emit_vt and sb > 1)
    assert cos_t.shape[0] == sb * t
    row = lambda b, j: (b, j, 0)
    const2 = lambda b, j: (0, 0)
    in_specs = [
        pl.BlockSpec((sb, tm, d), row),
        pl.BlockSpec((sb, STATE_ROWS, WIDTH), lambda b, j: (b, 0, 0)),
        pl.BlockSpec((sb * tm, V_DIM), lambda b, j: (j, 0)),
        pl.BlockSpec((sb * tm, V_DIM), lambda b, j: (j, 0)),
        pl.BlockSpec((1, d), const2),
        pl.BlockSpec(w_in_bf.shape, const2),
        pl.BlockSpec((1, WIDTH), const2),
        pl.BlockSpec((1, WIDTH), const2),
        pl.BlockSpec((CONV_W, WIDTH), const2),
        pl.BlockSpec((1, WIDTH), const2),
        pl.BlockSpec((GROUP_SUM_WIDTH, GROUP_SUM_WIDTH), const2),
    ]
    tile = pl.BlockSpec((sb, tm, WIDTH), row)
    heads_tile = pl.BlockSpec(
        (pl.Element(sb), pl.Element(tm * N_HEADS), pl.Element(V_DIM)),
        lambda b, j: (b * sb, pl.multiple_of((row_offset + j * tm) * N_HEADS, 8), 0))
    out_shape = [
        jax.ShapeDtypeStruct((nb, t, WIDTH), BF16),
        jax.ShapeDtypeStruct((nb, (row_offset + t) * N_HEADS, V_DIM), F32),
        jax.ShapeDtypeStruct((nb, (row_offset + t) * N_HEADS, V_DIM), F32),
        jax.ShapeDtypeStruct((nb, t, WIDTH), BF16),
        jax.ShapeDtypeStruct((nb, t, WIDTH), BF16),
        jax.ShapeDtypeStruct((nb, t, WIDTH), BF16),
        jax.ShapeDtypeStruct((nb, STATE_ROWS, WIDTH), F32),
    ]
    out_specs = [tile, heads_tile, heads_tile, tile, tile, tile,
                 pl.BlockSpec((sb, STATE_ROWS, WIDTH), lambda b, j: (b, 0, 0))]
    if emit_vt:
        out_shape.append(jax.ShapeDtypeStruct((nb, t // tkv, WIDTH, tkv), BF16))
        out_specs.append(pl.BlockSpec((1, tm // tkv, WIDTH, tkv), lambda b, j: (b, j, 0, 0)))
    return pl.pallas_call(
        functools.partial(_proj_kernel, sb=sb, tm=tm, tkv=tkv, emit_vt=emit_vt),
        grid=(nb // sb, nt),
        in_specs=in_specs,
        out_specs=out_specs,
        out_shape=out_shape,
        scratch_shapes=[pltpu.VMEM((sb, tm + STATE_ROWS, WIDTH), F32)],
        compiler_params=pltpu.CompilerParams(
            dimension_semantics=("arbitrary", "arbitrary"),
            vmem_limit_bytes=V7X_VMEM_LIMIT),
        name="proj",
    )(x, state8, cos_t, sin_t, ln_g, w_in_bf, qg, kg, conv_w, conv_b, gmat)


def _lambda(lq1_ref, lk1_ref, lq2_ref, lk2_ref):
    a = jnp.sum(lq1_ref[...] * lk1_ref[...], axis=-1, keepdims=True)
    b = jnp.sum(lq2_ref[...] * lk2_ref[...], axis=-1, keepdims=True)
    return jnp.exp(a) - jnp.exp(b) + LAMBDA_INIT


def _conv_branch_out(x, oc, wout_ref):
    return x + jnp.dot(oc, wout_ref[WIDTH:, :], preferred_element_type=F32)


def _gated_attn(o_heads, sga, subg):
    parts = []
    for h, o in enumerate(o_heads):
        ms = jnp.mean(o * o, axis=-1, keepdims=True)
        on = o * lax.rsqrt(ms + EPS) * subg * (1.0 - LAMBDA_INIT)
        parts.append((sga[:, h * V_DIM:(h + 1) * V_DIM].astype(F32) * on).astype(BF16))
    return jnp.concatenate(parts, axis=1)


def _attn_branch_out(base, mix, wout_ref):
    return base + jnp.dot(mix, wout_ref[:WIDTH, :], preferred_element_type=F32)


def _prompt_attn_kernel(bound_ref, x_ref, q_ref, kb_ref, vt_ref, km_ref, vmt_ref, sga_ref, oc_ref,
                        subg_ref, lq1_ref, lk1_ref, lq2_ref, lk2_ref, wout_ref, y_ref,
                        qz_ref, m_ref, l_ref, acc_ref, *, tq, kb, big):
    j = pl.program_id(1)
    lanes = 2 * tq
    n_diag = tq // kb
    n_full = j * n_diag

    lane = lax.broadcasted_iota(jnp.int32, (1, V_DIM), 1)
    for h in range(N_HEADS):
        qh = q_ref[0, :, h * V_DIM:(h + 1) * V_DIM]
        qz_ref[h, 0:tq, :] = jnp.where(lane < HEAD_DIM, qh, jnp.zeros_like(qh))
        qz_ref[h, tq:lanes, :] = jnp.where(lane >= HEAD_DIM, qh, jnp.zeros_like(qh))

    def scores(k_h, h):
        return lax.dot_general(k_h, qz_ref[h], (((1,), (1,)), ((), ())), preferred_element_type=F32)

    def k_slab(k, h):
        return k[:, h * V_DIM:(h + 1) * V_DIM]

    def sublane_partial(e):
        return jnp.sum(e.reshape(e.shape[0] // 8, 8, lanes), axis=0)

    kchunk = lax.broadcasted_iota(jnp.int32, (kb, 1), 0) // CHUNK
    qchunk = (lax.broadcasted_iota(jnp.int32, (1, lanes), 1) % tq) // CHUNK
    diag_masks = [kchunk + d * (kb // CHUNK) <= qchunk for d in range(n_diag)]

    def late_queries(a, d, axis):
        lo = d * kb
        if axis == 0:
            return jnp.concatenate([a[lo:tq], a[tq + lo:lanes]], axis=0)
        return jnp.concatenate([a[:, lo:tq], a[:, tq + lo:lanes]], axis=1)

    def add_late(full, part, d):
        lo, w = d * kb, tq - d * kb
        return jnp.concatenate([full[:, 0:lo], full[:, lo:tq] + part[:, 0:w],
                                full[:, tq:tq + lo], full[:, tq + lo:lanes] + part[:, w:2 * w]],
                               axis=1)

    @pl.when(bound_ref[0] <= SAFE_SCORE_BOUND)
    def _():
        def key_pass(k, pieces, assign, late_blocks=()):
            s_next = scores(k_slab(k, 0), 0)
            for h in range(N_HEADS):
                s = s_next
                if h + 1 < N_HEADS:
                    s_next = scores(k_slab(k, h + 1), h + 1)
                s_late = [lax.dot_general(k_slab(kl, h), late_queries(qz_ref[h], d, 0),
                                          (((1,), (1,)), ((), ())), preferred_element_type=F32)
                          for d, kl, _ in late_blocks]
                e = jnp.exp2(s)
                l_new = acc_new = None
                for row0, rows, vt, mask in pieces:
                    ep = e[row0:row0 + rows, :]
                    if mask is not None:
                        ep = jnp.where(mask, ep, 0.0)
                    lp = sublane_partial(ep)
                    ap = jnp.dot(vt[h * V_DIM:(h + 1) * V_DIM, :], ep.astype(BF16),
                                 preferred_element_type=F32)
                    l_new = lp if l_new is None else l_new + lp
                    acc_new = ap if acc_new is None else acc_new + ap
                for (d, _, vt), sl in zip(late_blocks, s_late):
                    ep = jnp.where(late_queries(diag_masks[d], d, 1), jnp.exp2(sl), 0.0)
                    lp = jnp.sum(ep.reshape(kb // 8, 8, ep.shape[1]), axis=0)
                    ap = jnp.dot(vt[h * V_DIM:(h + 1) * V_DIM, :], ep.astype(BF16),
                                 preferred_element_type=F32)
                    l_new = add_late(l_new, lp, d)
                    acc_new = add_late(acc_new, ap, d)
                if assign:
                    l_ref[h] = l_new
                    acc_ref[h] = acc_new
                else:
                    l_ref[h] += l_new
                    acc_ref[h] += acc_new

        n_big = n_full // big
        rem = n_full - n_big * big
        for r in range(0, big, n_diag):
            @pl.when(rem == r)
            def _():
                first = n_full - r
                n_blocks = r + 1
                kblocks = kb_ref[0, pl.ds(first, n_blocks)].reshape(n_blocks * kb, WIDTH)
                k = jnp.concatenate([km_ref[...], kblocks], axis=0)
                pieces = [(0, N_META, vmt_ref[...], None)]
                pieces += [(N_META + t * kb, kb, vt_ref[0, first + t],
                            None if t < r else diag_masks[0]) for t in range(n_blocks)]
                late = [(d, kb_ref[0, n_full + d], vt_ref[0, n_full + d])
                        for d in range(1, n_diag)]
                key_pass(k, pieces, assign=True, late_blocks=late)

        def big_block(i, carry):
            first = i * big
            k = kb_ref[0, pl.ds(first, big)].reshape(big * kb, WIDTH)
            key_pass(k, [(t * kb, kb, vt_ref[0, first + t], None) for t in range(big)], assign=False)
            return carry

        lax.fori_loop(0, n_big, big_block, 0)

    @pl.when(bound_ref[0] > SAFE_SCORE_BOUND)
    def _():
        for h in range(N_HEADS):
            s = scores(k_slab(km_ref, h), h)
            m = jnp.max(s, axis=0, keepdims=True)
            e = jnp.exp2(s - m)
            m_ref[h:h + 1, :] = m
            l_ref[h] = jnp.zeros((8, lanes), F32)
            l_ref[h, 0:1, :] = jnp.sum(e, axis=0, keepdims=True)
            acc_ref[h] = jnp.dot(vmt_ref[h * V_DIM:(h + 1) * V_DIM, :], e.astype(BF16),
                                 preferred_element_type=F32)

        def block(i, mask):
            k = kb_ref[0, i]
            vt = vt_ref[0, i]
            for h in range(N_HEADS):
                s = scores(k_slab(k, h), h)
                if mask is not None:
                    s = jnp.where(mask, s, NEG_BIG)
                m_old = m_ref[h:h + 1, :]
                m_new = jnp.maximum(m_old, jnp.max(s, axis=0, keepdims=True))
                alpha = jnp.exp2(m_old - m_new)
                e = jnp.exp2(s - m_new)
                l_ref[h, 0:1, :] = alpha * l_ref[h, 0:1, :] + jnp.sum(e, axis=0, keepdims=True)
                acc_ref[h] = alpha * acc_ref[h] + jnp.dot(
                    vt[h * V_DIM:(h + 1) * V_DIM, :], e.astype(BF16), preferred_element_type=F32)
                m_ref[h:h + 1, :] = m_new

        def full_block(i, carry):
            block(i, None)
            return carry

        lax.fori_loop(0, n_full, full_block, 0)
        for d in range(n_diag):
            block(n_full + d, diag_masks[d])

    base = _conv_branch_out(x_ref[0], oc_ref[0], wout_ref)
    lam = _lambda(lq1_ref, lk1_ref, lq2_ref, lk2_ref)
    o_heads = []
    for h in range(N_HEADS):
        o2 = acc_ref[h] / jnp.sum(l_ref[h], axis=0, keepdims=True)
        o_heads.append((o2[:, 0:tq] - lam * o2[:, tq:lanes]).T)
    y_ref[0] = _attn_branch_out(base, _gated_attn(o_heads, sga_ref[0], subg_ref[...]), wout_ref)


def _prompt_attn(bound, x, q, kb4, vt4, km_bf, vmt_bf, sga, oc, subg, lq1, lk1, lq2, lk2, w_out_bf,
                 *, tq, kb, big):
    nb, s, d = x.shape
    nq = s // tq
    assert tq % kb == 0 and big % (tq // kb) == 0
    resident = dict(pipeline_mode=pl.Buffered(1))
    row = lambda b, j: (b, j, 0)
    const2 = lambda b, j: (0, 0)
    whole = lambda b, j: (b, 0, 0, 0)
    in_specs = [
        pl.BlockSpec(memory_space=pltpu.SMEM),
        pl.BlockSpec((1, tq, d), row),
        pl.BlockSpec((1, tq, WIDTH), row),
        pl.BlockSpec((1, s // kb, kb, WIDTH), whole, **resident),
        pl.BlockSpec((1, s // kb, WIDTH, kb), whole, **resident),
        pl.BlockSpec((N_META, WIDTH), const2),
        pl.BlockSpec((WIDTH, N_META), const2),
        pl.BlockSpec((1, tq, WIDTH), row),
        pl.BlockSpec((1, tq, WIDTH), row),
        pl.BlockSpec((1, V_DIM), const2),
        pl.BlockSpec((1, HEAD_DIM), const2),
        pl.BlockSpec((1, HEAD_DIM), const2),
        pl.BlockSpec((1, HEAD_DIM), const2),
        pl.BlockSpec((1, HEAD_DIM), const2),
        pl.BlockSpec(w_out_bf.shape, const2),
    ]
    return pl.pallas_call(
        functools.partial(_prompt_attn_kernel, tq=tq, kb=kb, big=big),
        grid=(nb, nq),
        in_specs=in_specs,
        out_specs=pl.BlockSpec((1, tq, d), row),
        out_shape=jax.ShapeDtypeStruct((nb, s, d), F32),
        scratch_shapes=[
            pltpu.VMEM((N_HEADS, 2 * tq, V_DIM), BF16),
            pltpu.VMEM((N_HEADS, 2 * tq), F32),
            pltpu.VMEM((N_HEADS, 8, 2 * tq), F32),
            pltpu.VMEM((N_HEADS, V_DIM, 2 * tq), F32),
        ],
        compiler_params=pltpu.CompilerParams(
            dimension_semantics=("arbitrary", "arbitrary"),
            vmem_limit_bytes=V7X_VMEM_LIMIT),
        name="prompt_attn",
    )(bound, x, q, kb4, vt4, km_bf, vmt_bf, sga, oc, subg, lq1, lk1, lq2, lk2, w_out_bf)


def _sample_attn_kernel(x_ref, q_ref, kn_ref, vn_ref, ck_ref, cv_ref, km_ref, vm_ref, sga_ref,
                        oc_ref, subg_ref, lq1_ref, lk1_ref, lq2_ref, lk2_ref, wout_ref, y_ref,
                        kcat, vcat, mix_ref, *, past, t_new, n_keys_pad):
    b = pl.program_id(0)
    n_keys = past + N_META + t_new
    for h in range(N_HEADS):
        slab = slice(h * V_DIM, (h + 1) * V_DIM)
        kcat[0:past, slab] = ck_ref[0, pl.ds(h, past, stride=N_HEADS), :].astype(BF16)
        kcat[past + N_META:n_keys, slab] = kn_ref[0, pl.ds(h, t_new, stride=N_HEADS), :].astype(BF16)
        vcat[0:past, slab] = cv_ref[0, pl.ds(h, past, stride=N_HEADS), :].astype(BF16)
        vcat[past:past + N_META, slab] = vm_ref[pl.ds(h, N_META, stride=N_HEADS), :].astype(BF16)
        vcat[past + N_META:n_keys, slab] = vn_ref[0, pl.ds(h, t_new, stride=N_HEADS), :].astype(BF16)
    kcat[past:past + N_META, :] = km_ref[...]
    kcat[n_keys:n_keys_pad, :] = jnp.zeros((n_keys_pad - n_keys, WIDTH), BF16)
    vcat[n_keys:n_keys_pad, :] = jnp.zeros((n_keys_pad - n_keys, WIDTH), BF16)

    lane = lax.broadcasted_iota(jnp.int32, (1, V_DIM), 1)
    valid = lax.broadcasted_iota(jnp.int32, (1, n_keys_pad), 1) < n_keys
    lam = _lambda(lq1_ref, lk1_ref, lq2_ref, lk2_ref)
    o_heads = []
    for h in range(N_HEADS):
        qh = q_ref[0, :, h * V_DIM:(h + 1) * V_DIM]
        q2 = jnp.concatenate([jnp.where(lane < HEAD_DIM, qh, jnp.zeros_like(qh)),
                              jnp.where(lane >= HEAD_DIM, qh, jnp.zeros_like(qh))], axis=0)
        s = lax.dot_general(q2, kcat[:, h * V_DIM:(h + 1) * V_DIM], (((1,), (1,)), ((), ())),
                            preferred_element_type=F32)
        s = jnp.where(valid, s, NEG_BIG)
        e = jnp.exp2(s - jnp.max(s, axis=-1, keepdims=True))
        o2 = jnp.dot(e.astype(BF16), vcat[:, h * V_DIM:(h + 1) * V_DIM],
                     preferred_element_type=F32) / jnp.sum(e, axis=-1, keepdims=True)
        o_heads.append(o2[0:t_new] - lam * o2[t_new:2 * t_new])
    rows = pl.ds(pl.multiple_of(b * t_new, t_new), t_new)
    mix_ref[rows, :] = _gated_attn(o_heads, sga_ref[b], subg_ref[...])

    @pl.when(b == pl.num_programs(0) - 1)
    def _():
        n_rows = mix_ref.shape[0]
        base = _conv_branch_out(x_ref[...].reshape(n_rows, x_ref.shape[-1]),
                                oc_ref[...].reshape(n_rows, WIDTH), wout_ref)
        y_ref[...] = _attn_branch_out(base, mix_ref[...], wout_ref).reshape(y_ref.shape)


def _sample_attn(x, q, kn, vn, ck, cv, km_bf, vm, sga, oc, subg, lq1, lk1, lq2, lk2, w_out_bf):
    nb, t_new, d = x.shape
    past = ck.shape[1] // N_HEADS
    n_keys_pad = -(-(past + N_META + t_new) // V_DIM) * V_DIM
    row = lambda b: (b, 0, 0)
    const2 = lambda b: (0, 0)
    const3 = lambda b: (0, 0, 0)
    in_specs = [
        pl.BlockSpec((nb, t_new, d), const3),
        pl.BlockSpec((1, t_new, WIDTH), row),
        pl.BlockSpec((1, t_new * N_HEADS, V_DIM), row),
        pl.BlockSpec((1, t_new * N_HEADS, V_DIM), row),
        pl.BlockSpec((1, past * N_HEADS, V_DIM), row),
        pl.BlockSpec((1, past * N_HEADS, V_DIM), row),
        pl.BlockSpec((N_META, WIDTH), const2),
        pl.BlockSpec((N_META * N_HEADS, V_DIM), const2),
        pl.BlockSpec((nb, t_new, WIDTH), const3),
        pl.BlockSpec((nb, t_new, WIDTH), const3),
        pl.BlockSpec((1, V_DIM), const2),
        pl.BlockSpec((1, HEAD_DIM), const2),
        pl.BlockSpec((1, HEAD_DIM), const2),
        pl.BlockSpec((1, HEAD_DIM), const2),
        pl.BlockSpec((1, HEAD_DIM), const2),
        pl.BlockSpec(w_out_bf.shape, const2),
    ]
    return pl.pallas_call(
        functools.partial(_sample_attn_kernel, past=past, t_new=t_new, n_keys_pad=n_keys_pad),
        grid=(nb,),
        in_specs=in_specs,
        out_specs=pl.BlockSpec((nb, t_new, d), const3),
        out_shape=jax.ShapeDtypeStruct((nb, t_new, d), F32),
        scratch_shapes=[pltpu.VMEM((n_keys_pad, WIDTH), BF16),
                        pltpu.VMEM((n_keys_pad, WIDTH), BF16),
                        pltpu.VMEM((nb * t_new, WIDTH), BF16)],
        compiler_params=pltpu.CompilerParams(
            dimension_semantics=("arbitrary",),
            vmem_limit_bytes=V7X_VMEM_LIMIT),
        name="sample_attn",
    )(x, q, kn, vn, ck, cv, km_bf, vm, sga, oc, subg, lq1, lk1, lq2, lk2, w_out_bf)


def _rope_tables(pos):
    half = ROT_DIM // 2
    inv = ROPE_THETA ** (-jnp.arange(0, ROT_DIM, 2, dtype=F32) / ROT_DIM)
    ang = pos.astype(F32)[:, None] * inv[None, :]
    cos, sin = lax.optimization_barrier((jnp.cos(ang), jnp.sin(ang)))
    t = pos.shape[0]
    rest = HEAD_DIM - 2 * half
    c64 = jnp.concatenate([cos, cos, jnp.ones((t, rest), F32)], axis=1)
    s64 = jnp.concatenate([-sin, sin, jnp.zeros((t, rest), F32)], axis=1)
    return jnp.tile(c64, (1, 2)), jnp.tile(s64, (1, 2))


def _score_bound(qg, kg):
    bound = Q_SCALE * HEAD_DIM * jnp.max(jnp.abs(qg)) * jnp.max(jnp.abs(kg)) * (1.0 + 2.0 ** -6)
    return bound.reshape(1).astype(F32)


def _pad_state(state):
    return jnp.pad(state, ((0, 0), (STATE_ROWS - (CONV_W - 1), 0), (0, 0)))


def _tile_sizes(seq):
    kb = 256 if seq % 256 == 0 else seq
    tq = 512 if seq % 512 == 0 else kb
    return tq, tq, kb


def kernel(x_prompt, x_sample, cache_k, cache_v, state_conv, meta_tokens, ln_g, w_in, q_norm_g,
           k_norm_g, lam_q1, lam_k1, lam_q2, lam_k2, subln_g, conv_w, conv_b, w_out):
    assert ln_g.shape[0] == 1, "single-layer stack"
    nb, seq, d = x_prompt.shape
    db, t_new, _ = x_sample.shape
    past = cache_k.shape[2]

    w_in_bf = w_in[0].astype(BF16)
    w_out_bf = w_out[0].astype(BF16)
    lng = ln_g[0][None, :]
    qg = jnp.tile(q_norm_g[0], 2 * N_HEADS)[None, :]
    kg = jnp.tile(k_norm_g[0], 2 * N_HEADS)[None, :]
    subg = subln_g[0][None, :]
    cw, cb = conv_w[0], conv_b[0][None, :]
    lams = [p[0][None, :] for p in (lam_q1, lam_k1, lam_q2, lam_k2)]
    grp = jnp.arange(GROUP_SUM_WIDTH, dtype=jnp.int32) // HEAD_DIM
    gmat = (grp[:, None] == grp[None, :]).astype(BF16)

    pos_m = jnp.arange(N_META, dtype=jnp.int32)
    pos_p = N_META + jnp.arange(seq, dtype=jnp.int32)
    pos_s = N_META + past + jnp.arange(t_new, dtype=jnp.int32)
    weights = (lng, w_in_bf, qg, kg, cw, cb, gmat)
    tm, tq, kb = _tile_sizes(seq)

    assert t_new == N_META, "meta tokens and new frames are projected as equal-length streams"
    cos_m, sin_m = _rope_tables(pos_m)
    cos_s, sin_s = _rope_tables(pos_s)
    n_small = 1 + db
    q_sm, k_sm, v_sm, kb_sm, sga_sm, oc_sm, tail_sm = _proj_call(
        jnp.concatenate([meta_tokens[None], x_sample], axis=0),
        jnp.concatenate([jnp.zeros((1, STATE_ROWS, WIDTH), F32), _pad_state(state_conv[0])], axis=0),
        jnp.concatenate([cos_m] + [cos_s] * db, axis=0),
        jnp.concatenate([sin_m] + [sin_s] * db, axis=0),
        *weights, sb=n_small, tm=t_new, tkv=t_new, emit_vt=False)
    km, vm, km_bf, tail_m = k_sm[0], v_sm[0], kb_sm[0], tail_sm[0:1]
    qs, ks, vs, sga_s, oc_s, tail_s = (a[1:] for a in (q_sm, k_sm, v_sm, sga_sm, oc_sm, tail_sm))

    state_p = jnp.broadcast_to(tail_m, (nb, STATE_ROWS, WIDTH))
    qp, kp, vp, kp_bf, sga_p, oc_p, tail_p, vtp = _proj_call(
        x_prompt, state_p, *_rope_tables(pos_p), *weights, sb=1, tm=tm, tkv=kb, emit_vt=True,
        row_offset=N_META)
    y_prompt = _prompt_attn(
        _score_bound(q_norm_g[0], k_norm_g[0]), x_prompt, qp,
        kp_bf.reshape(nb, seq // kb, kb, WIDTH), vtp, km_bf,
        vm.reshape(N_META, WIDTH).T.astype(BF16),
        sga_p, oc_p, subg, *lams, w_out_bf, tq=tq, kb=kb, big=BIG_KEY_BLOCKS)
    kp = kp.at[:, :N_META * N_HEADS].set(jnp.broadcast_to(km[None], (nb,) + km.shape))
    vp = vp.at[:, :N_META * N_HEADS].set(jnp.broadcast_to(vm[None], (nb,) + vm.shape))

    y_sample = _sample_attn(
        x_sample, qs, ks, vs, cache_k[0].reshape(db, past * N_HEADS, V_DIM),
        cache_v[0].reshape(db, past * N_HEADS, V_DIM), km_bf, vm, sga_s, oc_s, subg, *lams, w_out_bf)

    def heads_form(a, rows):
        return a.reshape(1, a.shape[0], rows, N_HEADS, V_DIM)

    tail_rows = slice(STATE_ROWS - (CONV_W - 1), STATE_ROWS)
    return (y_prompt, y_sample, heads_form(kp, N_META + seq), heads_form(vp, N_META + seq),
            tail_p[None, :, tail_rows], heads_form(ks, t_new), heads_form(vs, t_new),
            tail_s[None, :, tail_rows])
```

```python
import functools
import math

import jax
import jax.numpy as jnp
from jax import lax
from jax.experimental import pallas as pl
from jax.experimental.pallas import tpu as pltpu

F32 = jnp.float32
BF16 = jnp.bfloat16

CHUNK = 64
N_META = 16
N_HEADS = 4
HEAD_DIM = 64
V_DIM = 2 * HEAD_DIM
WIDTH = N_HEADS * V_DIM
CONV_W = 3
ROT_DIM = HEAD_DIM // 4
ROPE_THETA = 500000.0
EPS = 1e-6
LAMBDA_INIT = 0.8 - 0.6 * math.exp(-0.3 * 0)
Q_SCALE = HEAD_DIM ** -0.5 * math.log2(math.e)
NEG_BIG = -1e30
SAFE_SCORE_BOUND = 40.0
BIG_KEY_BLOCKS = 8
GROUP_SUM_WIDTH = 256
STATE_ROWS = 8

V7X_VMEM_LIMIT = 60 * 1024 * 1024


def _silu(x):
    return x / (1.0 + jnp.exp(-x))


def _proj_kernel(x_ref, st_ref, cos_ref, sin_ref, lng_ref, win_ref, qg_ref, kg_ref, cw_ref, cb_ref,
                 gmat_ref, q_ref, kf_ref, vf_ref, kb_ref, sga_ref, oc_ref, tail_ref, *rest,
                 sb, tm, tkv, emit_vt):
    if emit_vt:
        vt_ref, ubuf = rest
    else:
        (ubuf,) = rest
    j = pl.program_id(1)
    rows = sb * tm

    @pl.when(j == 0)
    def _():
        ubuf[:, 0:STATE_ROWS, :] = st_ref[...]

    @pl.when(j > 0)
    def _():
        ubuf[:, 0:STATE_ROWS, :] = ubuf[:, tm:tm + STATE_ROWS, :]

    x = x_ref[...].reshape(rows, x_ref.shape[-1])
    xg = (x * lng_ref[...]).astype(BF16)
    inv_rms = lax.rsqrt(jnp.mean(x * x, axis=-1, keepdims=True) + EPS)

    def proj(i):
        return inv_rms * jnp.dot(xg, win_ref[:, i * WIDTH:(i + 1) * WIDTH],
                                 preferred_element_type=F32)

    def group_mean_sq(t):
        tt = (t * t).astype(BF16)
        halves = [jnp.dot(tt[:, i * GROUP_SUM_WIDTH:(i + 1) * GROUP_SUM_WIDTH], gmat_ref[...],
                          preferred_element_type=F32) for i in range(WIDTH // GROUP_SUM_WIDTH)]
        return jnp.concatenate(halves, axis=1) * (1.0 / HEAD_DIM)

    cos = cos_ref[...]
    sin = sin_ref[...]
    lane = lax.broadcasted_iota(jnp.int32, (1, V_DIM), 1) % HEAD_DIM
    take_lower = lane >= ROT_DIM // 2

    def norm_rope(t, mean_sq, g):
        tn = t * lax.rsqrt(mean_sq + EPS) * g
        outs = []
        for h in range(N_HEADS):
            th = tn[:, h * V_DIM:(h + 1) * V_DIM]
            lower = pltpu.roll(th, ROT_DIM // 2, axis=1)
            upper = pltpu.roll(th, V_DIM - ROT_DIM // 2, axis=1)
            outs.append(th * cos + jnp.where(take_lower, lower, upper) * sin)
        return jnp.concatenate(outs, axis=1)

    def per_stream(a):
        return a.reshape(sb, tm, a.shape[-1])

    pq = proj(0)
    pk = proj(1)
    msq = group_mean_sq(pq)
    v = proj(2)
    msk = group_mean_sq(pk)
    ga = proj(3)
    q_ref[...] = per_stream((norm_rope(pq, msq, qg_ref[...]) * Q_SCALE).astype(BF16))
    bb = proj(4)
    k = norm_rope(pk, msk, kg_ref[...])
    kb_ref[...] = per_stream(k.astype(BF16))
    cc = proj(5)
    for b in range(sb):
        for h in range(N_HEADS):
            src = (slice(b * tm, (b + 1) * tm), slice(h * V_DIM, (h + 1) * V_DIM))
            kf_ref[b, pl.ds(h, tm, stride=N_HEADS), :] = k[src]
            vf_ref[b, pl.ds(h, tm, stride=N_HEADS), :] = v[src]
    hh = proj(6)
    if emit_vt:
        for s in range(tm // tkv):
            vt_ref[0, s] = v[s * tkv:(s + 1) * tkv, :].T.astype(BF16)
    gc = proj(7)
    sga_ref[...] = per_stream(_silu(ga).astype(BF16))

    u = cc * hh
    ys = []
    for b in range(sb):
        ubuf[b, STATE_ROWS:STATE_ROWS + tm, :] = u[b * tm:(b + 1) * tm, :]
        y = cb_ref[...]
        for tap in range(CONV_W):
            off = STATE_ROWS - (CONV_W - 1) + tap
            y = y + cw_ref[tap:tap + 1, :] * ubuf[b, off:off + tm, :]
        ys.append(y)
    y = ys[0] if sb == 1 else jnp.concatenate(ys, axis=0)
    oc_ref[...] = per_stream((_silu(gc) * (bb * y)).astype(BF16))
    tail_ref[...] = ubuf[:, tm:tm + STATE_ROWS, :]


def _proj_call(x, state8, cos_t, sin_t, ln_g, w_in_bf, qg, kg, conv_w, conv_b, gmat, *, sb, tm, tkv,
               emit_vt, row_offset=0):
    nb, t, d = x.shape
    nt = t // tm
    assert t % tm == 0 and tm % tkv == 0 and nb % sb == 0 and not (emit_vt and sb > 1)
    assert cos_t.shape[0] == sb * t
    row = lambda b, j: (b, j, 0)
    const2 = lambda b, j: (0, 0)
    in_specs = [
        pl.BlockSpec((sb, tm, d), row),
        pl.BlockSpec((sb, STATE_ROWS, WIDTH), lambda b, j: (b, 0, 0)),
        pl.BlockSpec((sb * tm, V_DIM), lambda b, j: (j, 0)),
        pl.BlockSpec((sb * tm, V_DIM), lambda b, j: (j, 0)),
        pl.BlockSpec((1, d), const2),
        pl.BlockSpec(w_in_bf.shape, const2),
        pl.BlockSpec((1, WIDTH), const2),
        pl.BlockSpec((1, WIDTH), const2),
        pl.BlockSpec((CONV_W, WIDTH), const2),
        pl.BlockSpec((1, WIDTH), const2),
        pl.BlockSpec((GROUP_SUM_WIDTH, GROUP_SUM_WIDTH), const2),
    ]
    tile = pl.BlockSpec((sb, tm, WIDTH), row)
    heads_tile = pl.BlockSpec(
        (pl.Element(sb), pl.Element(tm * N_HEADS), pl.Element(V_DIM)),
        lambda b, j: (b * sb, pl.multiple_of((row_offset + j * tm) * N_HEADS, 8), 0))
    out_shape = [
        jax.ShapeDtypeStruct((nb, t, WIDTH), BF16),
        jax.ShapeDtypeStruct((nb, (row_offset + t) * N_HEADS, V_DIM), F32),
        jax.ShapeDtypeStruct((nb, (row_offset + t) * N_HEADS, V_DIM), F32),
        jax.ShapeDtypeStruct((nb, t, WIDTH), BF16),
        jax.ShapeDtypeStruct((nb, t, WIDTH), BF16),
        jax.ShapeDtypeStruct((nb, t, WIDTH), BF16),
        jax.ShapeDtypeStruct((nb, STATE_ROWS, WIDTH), F32),
    ]
    out_specs = [tile, heads_tile, heads_tile, tile, tile, tile,
                 pl.BlockSpec((sb, STATE_ROWS, WIDTH), lambda b, j: (b, 0, 0))]
    if emit_vt:
        out_shape.append(jax.ShapeDtypeStruct((nb, t // tkv, WIDTH, tkv), BF16))
        out_specs.append(pl.BlockSpec((1, tm // tkv, WIDTH, tkv), lambda b, j: (b, j, 0, 0)))
    return pl.pallas_call(
        functools.partial(_proj_kernel, sb=sb, tm=tm, tkv=tkv, emit_vt=emit_vt),
        grid=(nb // sb, nt),
        in_specs=in_specs,
        out_specs=out_specs,
        out_shape=out_shape,
        scratch_shapes=[pltpu.VMEM((sb, tm + STATE_ROWS, WIDTH), F32)],
        compiler_params=pltpu.CompilerParams(
            dimension_semantics=("arbitrary", "arbitrary"),
            vmem_limit_bytes=V7X_VMEM_LIMIT),
        name="proj",
    )(x, state8, cos_t, sin_t, ln_g, w_in_bf, qg, kg, conv_w, conv_b, gmat)


def _lambda(lq1_ref, lk1_ref, lq2_ref, lk2_ref):
    a = jnp.sum(lq1_ref[...] * lk1_ref[...], axis=-1, keepdims=True)
    b = jnp.sum(lq2_ref[...] * lk2_ref[...], axis=-1, keepdims=True)
    return jnp.exp(a) - jnp.exp(b) + LAMBDA_INIT


def _conv_branch_out(x, oc, wout_ref):
    return x + jnp.dot(oc, wout_ref[WIDTH:, :], preferred_element_type=F32)


def _gated_attn(o_heads, sga, subg):
    parts = []
    for h, o in enumerate(o_heads):
        ms = jnp.mean(o * o, axis=-1, keepdims=True)
        on = o * lax.rsqrt(ms + EPS) * subg * (1.0 - LAMBDA_INIT)
        parts.append((sga[:, h * V_DIM:(h + 1) * V_DIM].astype(F32) * on).astype(BF16))
    return jnp.concatenate(parts, axis=1)


def _attn_branch_out(base, mix, wout_ref):
    return base + jnp.dot(mix, wout_ref[:WIDTH, :], preferred_element_type=F32)


def _prompt_attn_kernel(bound_ref, x_ref, q_ref, kb_ref, vt_ref, km_ref, vmt_ref, sga_ref, oc_ref,
                        subg_ref, lq1_ref, lk1_ref, lq2_ref, lk2_ref, wout_ref, y_ref,
                        qz_ref, m_ref, l_ref, acc_ref, *, tq, kb, big):
    j = pl.program_id(1)
    lanes = 2 * tq
    n_diag = tq // kb
    n_full = j * n_diag

    lane = lax.broadcasted_iota(jnp.int32, (1, V_DIM), 1)
    for h in range(N_HEADS):
        qh = q_ref[0, :, h * V_DIM:(h + 1) * V_DIM]
        qz_ref[h, 0:tq, :] = jnp.where(lane < HEAD_DIM, qh, jnp.zeros_like(qh))
        qz_ref[h, tq:lanes, :] = jnp.where(lane >= HEAD_DIM, qh, jnp.zeros_like(qh))

    def scores(k_h, h):
        return lax.dot_general(k_h, qz_ref[h], (((1,), (1,)), ((), ())), preferred_element_type=F32)

    def k_slab(k, h):
        return k[:, h * V_DIM:(h + 1) * V_DIM]

    def sublane_partial(e):
        return jnp.sum(e.reshape(e.shape[0] // 8, 8, lanes), axis=0)

    kchunk = lax.broadcasted_iota(jnp.int32, (kb, 1), 0) // CHUNK
    qchunk = (lax.broadcasted_iota(jnp.int32, (1, lanes), 1) % tq) // CHUNK
    diag_masks = [kchunk + d * (kb // CHUNK) <= qchunk for d in range(n_diag)]

    def late_queries(a, d, axis):
        lo = d * kb
        if axis == 0:
            return jnp.concatenate([a[lo:tq], a[tq + lo:lanes]], axis=0)
        return jnp.concatenate([a[:, lo:tq], a[:, tq + lo:lanes]], axis=1)

    def add_late(full, part, d):
        lo, w = d * kb, tq - d * kb
        return jnp.concatenate([full[:, 0:lo], full[:, lo:tq] + part[:, 0:w],
                                full[:, tq:tq + lo], full[:, tq + lo:lanes] + part[:, w:2 * w]],
                               axis=1)

    @pl.when(bound_ref[0] <= SAFE_SCORE_BOUND)
    def _():
        def key_pass(k, pieces, assign, late_blocks=()):
            s_next = scores(k_slab(k, 0), 0)
            for h in range(N_HEADS):
                s = s_next
                if h + 1 < N_HEADS:
                    s_next = scores(k_slab(k, h + 1), h + 1)
                s_late = [lax.dot_general(k_slab(kl, h), late_queries(qz_ref[h], d, 0),
                                          (((1,), (1,)), ((), ())), preferred_element_type=F32)
                          for d, kl, _ in late_blocks]
                e = jnp.exp2(s)
                l_new = acc_new = None
                for row0, rows, vt, mask in pieces:
                    ep = e[row0:row0 + rows, :]
                    if mask is not None:
                        ep = jnp.where(mask, ep, 0.0)
                    lp = sublane_partial(ep)
                    ap = jnp.dot(vt[h * V_DIM:(h + 1) * V_DIM, :], ep.astype(BF16),
                                 preferred_element_type=F32)
                    l_new = lp if l_new is None else l_new + lp
                    acc_new = ap if acc_new is None else acc_new + ap
                for (d, _, vt), sl in zip(late_blocks, s_late):
                    ep = jnp.where(late_queries(diag_masks[d], d, 1), jnp.exp2(sl), 0.0)
                    lp = jnp.sum(ep.reshape(kb // 8, 8, ep.shape[1]), axis=0)
                    ap = jnp.dot(vt[h * V_DIM:(h + 1) * V_DIM, :], ep.astype(BF16),
                                 preferred_element_type=F32)
                    l_new = add_late(l_new, lp, d)
                    acc_new = add_late(acc_new, ap, d)
                if assign:
                    l_ref[h] = l_new
                    acc_ref[h] = acc_new
                else:
                    l_ref[h] += l_new
                    acc_ref[h] += acc_new

        n_big = n_full // big
        rem = n_full - n_big * big
        for r in range(0, big, n_diag):
            @pl.when(rem == r)
            def _():
                first = n_full - r
                n_blocks = r + 1
                kblocks = kb_ref[0, pl.ds(first, n_blocks)].reshape(n_blocks * kb, WIDTH)
                k = jnp.concatenate([km_ref[...], kblocks], axis=0)
                pieces = [(0, N_META, vmt_ref[...], None)]
                pieces += [(N_META + t * kb, kb, vt_ref[0, first + t],
                            None if t < r else diag_masks[0]) for t in range(n_blocks)]
                late = [(d, kb_ref[0, n_full + d], vt_ref[0, n_full + d])
                        for d in range(1, n_diag)]
                key_pass(k, pieces, assign=True, late_blocks=late)

        def big_block(i, carry):
            first = i * big
            k = kb_ref[0, pl.ds(first, big)].reshape(big * kb, WIDTH)
            key_pass(k, [(t * kb, kb, vt_ref[0, first + t], None) for t in range(big)], assign=False)
            return carry

        lax.fori_loop(0, n_big, big_block, 0)

    @pl.when(bound_ref[0] > SAFE_SCORE_BOUND)
    def _():
        for h in range(N_HEADS):
            s = scores(k_slab(km_ref, h), h)
            m = jnp.max(s, axis=0, keepdims=True)
            e = jnp.exp2(s - m)
            m_ref[h:h + 1, :] = m
            l_ref[h] = jnp.zeros((8, lanes), F32)
            l_ref[h, 0:1, :] = jnp.sum(e, axis=0, keepdims=True)
            acc_ref[h] = jnp.dot(vmt_ref[h * V_DIM:(h + 1) * V_DIM, :], e.astype(BF16),
                                 preferred_element_type=F32)

        def block(i, mask):
            k = kb_ref[0, i]
            vt = vt_ref[0, i]
            for h in range(N_HEADS):
                s = scores(k_slab(k, h), h)
                if mask is not None:
                    s = jnp.where(mask, s, NEG_BIG)
                m_old = m_ref[h:h + 1, :]
                m_new = jnp.maximum(m_old, jnp.max(s, axis=0, keepdims=True))
                alpha = jnp.exp2(m_old - m_new)
                e = jnp.exp2(s - m_new)
                l_ref[h, 0:1, :] = alpha * l_ref[h, 0:1, :] + jnp.sum(e, axis=0, keepdims=True)
                acc_ref[h] = alpha * acc_ref[h] + jnp.dot(
                    vt[h * V_DIM:(h + 1) * V_DIM, :], e.astype(BF16), preferred_element_type=F32)
                m_ref[h:h + 1, :] = m_new

        def full_block(i, carry):
            block(i, None)
            return carry

        lax.fori_loop(0, n_full, full_block, 0)
        for d in range(n_diag):
            block(n_full + d, diag_masks[d])

    base = _conv_branch_out(x_ref[0], oc_ref[0], wout_ref)
    lam = _lambda(lq1_ref, lk1_ref, lq2_ref, lk2_ref)
    o_heads = []
    for h in range(N_HEADS):
        o2 = acc_ref[h] / jnp.sum(l_ref[h], axis=0, keepdims=True)
        o_heads.append((o2[:, 0:tq] - lam * o2[:, tq:lanes]).T)
    y_ref[0] = _attn_branch_out(base, _gated_attn(o_heads, sga_ref[0], subg_ref[...]), wout_ref)


def _prompt_attn(bound, x, q, kb4, vt4, km_bf, vmt_bf, sga, oc, subg, lq1, lk1, lq2, lk2, w_out_bf,
                 *, tq, kb, big):
    nb, s, d = x.shape
    nq = s // tq
    assert tq % kb == 0 and big % (tq // kb) == 0
    resident = dict(pipeline_mode=pl.Buffered(1))
    row = lambda b, j: (b, j, 0)
    const2 = lambda b, j: (0, 0)
    whole = lambda b, j: (b, 0, 0, 0)
    in_specs = [
        pl.BlockSpec(memory_space=pltpu.SMEM),
        pl.BlockSpec((1, tq, d), row),
        pl.BlockSpec((1, tq, WIDTH), row),
        pl.BlockSpec((1, s // kb, kb, WIDTH), whole),
        pl.BlockSpec((1, s // kb, WIDTH, kb), whole, **resident),
        pl.BlockSpec((N_META, WIDTH), const2),
        pl.BlockSpec((WIDTH, N_META), const2),
        pl.BlockSpec((1, tq, WIDTH), row),
        pl.BlockSpec((1, tq, WIDTH), row),
        pl.BlockSpec((1, V_DIM), const2),
        pl.BlockSpec((1, HEAD_DIM), const2),
        pl.BlockSpec((1, HEAD_DIM), const2),
        pl.BlockSpec((1, HEAD_DIM), const2),
        pl.BlockSpec((1, HEAD_DIM), const2),
        pl.BlockSpec(w_out_bf.shape, const2),
    ]
    return pl.pallas_call(
        functools.partial(_prompt_attn_kernel, tq=tq, kb=kb, big=big),
        grid=(nb, nq),
        in_specs=in_specs,
        out_specs=pl.BlockSpec((1, tq, d), row),
        out_shape=jax.ShapeDtypeStruct((nb, s, d), F32),
        scratch_shapes=[
            pltpu.VMEM((N_HEADS, 2 * tq, V_DIM), BF16),
            pltpu.VMEM((N_HEADS, 2 * tq), F32),
            pltpu.VMEM((N_HEADS, 8, 2 * tq), F32),
            pltpu.VMEM((N_HEADS, V_DIM, 2 * tq), F32),
        ],
        compiler_params=pltpu.CompilerParams(
            dimension_semantics=("arbitrary", "arbitrary"),
            vmem_limit_bytes=V7X_VMEM_LIMIT),
        name="prompt_attn",
    )(bound, x, q, kb4, vt4, km_bf, vmt_bf, sga, oc, subg, lq1, lk1, lq2, lk2, w_out_bf)


def _sample_attn_kernel(x_ref, q_ref, kn_ref, vn_ref, ck_ref, cv_ref, km_ref, vm_ref, sga_ref,
                        oc_ref, subg_ref, lq1_ref, lk1_ref, lq2_ref, lk2_ref, wout_ref, y_ref,
                        kcat, vcat, mix_ref, *, past, t_new, n_keys_pad):
    b = pl.program_id(0)
    n_keys = past + N_META + t_new
    for h in range(N_HEADS):
        slab = slice(h * V_DIM, (h + 1) * V_DIM)
        kcat[0:past, slab] = ck_ref[0, pl.ds(h, past, stride=N_HEADS), :].astype(BF16)
        kcat[past + N_META:n_keys, slab] = kn_ref[0, pl.ds(h, t_new, stride=N_HEADS), :].astype(BF16)
        vcat[0:past, slab] = cv_ref[0, pl.ds(h, past, stride=N_HEADS), :].astype(BF16)
        vcat[past:past + N_META, slab] = vm_ref[pl.ds(h, N_META, stride=N_HEADS), :].astype(BF16)
        vcat[past + N_META:n_keys, slab] = vn_ref[0, pl.ds(h, t_new, stride=N_HEADS), :].astype(BF16)
    kcat[past:past + N_META, :] = km_ref[...]
    kcat[n_keys:n_keys_pad, :] = jnp.zeros((n_keys_pad - n_keys, WIDTH), BF16)
    vcat[n_keys:n_keys_pad, :] = jnp.zeros((n_keys_pad - n_keys, WIDTH), BF16)

    lane = lax.broadcasted_iota(jnp.int32, (1, V_DIM), 1)
    valid = lax.broadcasted_iota(jnp.int32, (1, n_keys_pad), 1) < n_keys
    lam = _lambda(lq1_ref, lk1_ref, lq2_ref, lk2_ref)
    o_heads = []
    for h in range(N_HEADS):
        qh = q_ref[0, :, h * V_DIM:(h + 1) * V_DIM]
        q2 = jnp.concatenate([jnp.where(lane < HEAD_DIM, qh, jnp.zeros_like(qh)),
                              jnp.where(lane >= HEAD_DIM, qh, jnp.zeros_like(qh))], axis=0)
        s = lax.dot_general(q2, kcat[:, h * V_DIM:(h + 1) * V_DIM], (((1,), (1,)), ((), ())),
                            preferred_element_type=F32)
        s = jnp.where(valid, s, NEG_BIG)
        e = jnp.exp2(s - jnp.max(s, axis=-1, keepdims=True))
        o2 = jnp.dot(e.astype(BF16), vcat[:, h * V_DIM:(h + 1) * V_DIM],
                     preferred_element_type=F32) / jnp.sum(e, axis=-1, keepdims=True)
        o_heads.append(o2[0:t_new] - lam * o2[t_new:2 * t_new])
    rows = pl.ds(pl.multiple_of(b * t_new, t_new), t_new)
    mix_ref[rows, :] = _gated_attn(o_heads, sga_ref[b], subg_ref[...])

    @pl.when(b == pl.num_programs(0) - 1)
    def _():
        n_rows = mix_ref.shape[0]
        base = _conv_branch_out(x_ref[...].reshape(n_rows, x_ref.shape[-1]),
                                oc_ref[...].reshape(n_rows, WIDTH), wout_ref)
        y_ref[...] = _attn_branch_out(base, mix_ref[...], wout_ref).reshape(y_ref.shape)


def _sample_attn(x, q, kn, vn, ck, cv, km_bf, vm, sga, oc, subg, lq1, lk1, lq2, lk2, w_out_bf):
    nb, t_new, d = x.shape
    past = ck.shape[1] // N_HEADS
    n_keys_pad = -(-(past + N_META + t_new) // V_DIM) * V_DIM
    row = lambda b: (b, 0, 0)
    const2 = lambda b: (0, 0)
    const3 = lambda b: (0, 0, 0)
    in_specs = [
        pl.BlockSpec((nb, t_new, d), const3),
        pl.BlockSpec((1, t_new, WIDTH), row),
        pl.BlockSpec((1, t_new * N_HEADS, V_DIM), row),
        pl.BlockSpec((1, t_new * N_HEADS, V_DIM), row),
        pl.BlockSpec((1, past * N_HEADS, V_DIM), row),
        pl.BlockSpec((1, past * N_HEADS, V_DIM), row),
        pl.BlockSpec((N_META, WIDTH), const2),
        pl.BlockSpec((N_META * N_HEADS, V_DIM), const2),
        pl.BlockSpec((nb, t_new, WIDTH), const3),
        pl.BlockSpec((nb, t_new, WIDTH), const3),
        pl.BlockSpec((1, V_DIM), const2),
        pl.BlockSpec((1, HEAD_DIM), const2),
        pl.BlockSpec((1, HEAD_DIM), const2),
        pl.BlockSpec((1, HEAD_DIM), const2),
        pl.BlockSpec((1, HEAD_DIM), const2),
        pl.BlockSpec(w_out_bf.shape, const2),
    ]
    return pl.pallas_call(
        functools.partial(_sample_attn_kernel, past=past, t_new=t_new, n_keys_pad=n_keys_pad),
        grid=(nb,),
        in_specs=in_specs,
        out_specs=pl.BlockSpec((nb, t_new, d), const3),
        out_shape=jax.ShapeDtypeStruct((nb, t_new, d), F32),
        scratch_shapes=[pltpu.VMEM((n_keys_pad, WIDTH), BF16),
                        pltpu.VMEM((n_keys_pad, WIDTH), BF16),
                        pltpu.VMEM((nb * t_new, WIDTH), BF16)],
        compiler_params=pltpu.CompilerParams(
            dimension_semantics=("arbitrary",),
            vmem_limit_bytes=V7X_VMEM_LIMIT),
        name="sample_attn",
    )(x, q, kn, vn, ck, cv, km_bf, vm, sga, oc, subg, lq1, lk1, lq2, lk2, w_out_bf)


def _rope_tables(pos):
    half = ROT_DIM // 2
    inv = ROPE_THETA ** (-jnp.arange(0, ROT_DIM, 2, dtype=F32) / ROT_DIM)
    ang = pos.astype(F32)[:, None] * inv[None, :]
    cos, sin = lax.optimization_barrier((jnp.cos(ang), jnp.sin(ang)))
    t = pos.shape[0]
    rest = HEAD_DIM - 2 * half
    c64 = jnp.concatenate([cos, cos, jnp.ones((t, rest), F32)], axis=1)
    s64 = jnp.concatenate([-sin, sin, jnp.zeros((t, rest), F32)], axis=1)
    return jnp.tile(c64, (1, 2)), jnp.tile(s64, (1, 2))


def _score_bound(qg, kg):
    bound = Q_SCALE * HEAD_DIM * jnp.max(jnp.abs(qg)) * jnp.max(jnp.abs(kg)) * (1.0 + 2.0 ** -6)
    return bound.reshape(1).astype(F32)


def _pad_state(state):
    return jnp.pad(state, ((0, 0), (STATE_ROWS - (CONV_W - 1), 0), (0, 0)))


def _tile_sizes(seq):
    kb = 256 if seq % 256 == 0 else seq
    tq = 512 if seq % 512 == 0 else kb
    return tq, tq, kb


def kernel(x_prompt, x_sample, cache_k, cache_v, state_conv, meta_tokens, ln_g, w_in, q_norm_g,
           k_norm_g, lam_q1, lam_k1, lam_q2, lam_k2, subln_g, conv_w, conv_b, w_out):
    assert ln_g.shape[0] == 1, "single-layer stack"
    nb, seq, d = x_prompt.shape
    db, t_new, _ = x_sample.shape
    past = cache_k.shape[2]

    w_in_bf = w_in[0].astype(BF16)
    w_out_bf = w_out[0].astype(BF16)
    lng = ln_g[0][None, :]
    qg = jnp.tile(q_norm_g[0], 2 * N_HEADS)[None, :]
    kg = jnp.tile(k_norm_g[0], 2 * N_HEADS)[None, :]
    subg = subln_g[0][None, :]
    cw, cb = conv_w[0], conv_b[0][None, :]
    lams = [p[0][None, :] for p in (lam_q1, lam_k1, lam_q2, lam_k2)]
    grp = jnp.arange(GROUP_SUM_WIDTH, dtype=jnp.int32) // HEAD_DIM
    gmat = (grp[:, None] == grp[None, :]).astype(BF16)

    pos_m = jnp.arange(N_META, dtype=jnp.int32)
    pos_p = N_META + jnp.arange(seq, dtype=jnp.int32)
    pos_s = N_META + past + jnp.arange(t_new, dtype=jnp.int32)
    weights = (lng, w_in_bf, qg, kg, cw, cb, gmat)
    tm, tq, kb = _tile_sizes(seq)

    assert t_new == N_META, "meta tokens and new frames are projected as equal-length streams"
    cos_m, sin_m = _rope_tables(pos_m)
    cos_s, sin_s = _rope_tables(pos_s)
    n_small = 1 + db
    q_sm, k_sm, v_sm, kb_sm, sga_sm, oc_sm, tail_sm = _proj_call(
        jnp.concatenate([meta_tokens[None], x_sample], axis=0),
        jnp.concatenate([jnp.zeros((1, STATE_ROWS, WIDTH), F32), _pad_state(state_conv[0])], axis=0),
        jnp.concatenate([cos_m] + [cos_s] * db, axis=0),
        jnp.concatenate([sin_m] + [sin_s] * db, axis=0),
        *weights, sb=n_small, tm=t_new, tkv=t_new, emit_vt=False)
    km, vm, km_bf, tail_m = k_sm[0], v_sm[0], kb_sm[0], tail_sm[0:1]
    qs, ks, vs, sga_s, oc_s, tail_s = (a[1:] for a in (q_sm, k_sm, v_sm, sga_sm, oc_sm, tail_sm))

    state_p = jnp.broadcast_to(tail_m, (nb, STATE_ROWS, WIDTH))
    qp, kp, vp, kp_bf, sga_p, oc_p, tail_p, vtp = _proj_call(
        x_prompt, state_p, *_rope_tables(pos_p), *weights, sb=1, tm=tm, tkv=kb, emit_vt=True,
        row_offset=N_META)
    y_prompt = _prompt_attn(
        _score_bound(q_norm_g[0], k_norm_g[0]), x_prompt, qp,
        kp_bf.reshape(nb, seq // kb, kb, WIDTH), vtp, km_bf,
        vm.reshape(N_META, WIDTH).T.astype(BF16),
        sga_p, oc_p, subg, *lams, w_out_bf, tq=tq, kb=kb, big=BIG_KEY_BLOCKS)
    kp = kp.at[:, :N_META * N_HEADS].set(jnp.broadcast_to(km[None], (nb,) + km.shape))
    vp = vp.at[:, :N_META * N_HEADS].set(jnp.broadcast_to(vm[None], (nb,) + vm.shape))

    y_sample = _sample_attn(
        x_sample, qs, ks, vs, cache_k[0].reshape(db, past * N_HEADS, V_DIM),
        cache_v[0].reshape(db, past * N_HEADS, V_DIM), km_bf, vm, sga_s, oc_s, subg, *lams, w_out_bf)

    def heads_form(a, rows):
        return a.reshape(1, a.shape[0], rows, N_HEADS, V_DIM)

    tail_rows = slice(STATE_ROWS - (CONV_W - 1), STATE_ROWS)
    return (y_prompt, y_sample, heads_form(kp, N_META + seq), heads_form(vp, N_META + seq),
            tail_p[None, :, tail_rows], heads_form(ks, t_new), heads_form(vs, t_new),
            tail_s[None, :, tail_rows])
```

```python
import functools
import math

import jax
import jax.numpy as jnp
from jax import lax
from jax.experimental import pallas as pl
from jax.experimental.pallas import tpu as pltpu

F32 = jnp.float32
BF16 = jnp.bfloat16

CHUNK = 64
N_META = 16
N_HEADS = 4
HEAD_DIM = 64
V_DIM = 2 * HEAD_DIM
WIDTH = N_HEADS * V_DIM
CONV_W = 3
ROT_DIM = HEAD_DIM // 4
ROPE_THETA = 500000.0
EPS = 1e-6
LAMBDA_INIT = 0.8 - 0.6 * math.exp(-0.3 * 0)
Q_SCALE = HEAD_DIM ** -0.5 * math.log2(math.e)
NEG_BIG = -1e30
SAFE_SCORE_BOUND = 40.0
V7X_SUBLANES = 8
V7X_MXU_WIDTH = 256
V7X_VMEM_LIMIT = 60 * 1024 * 1024

KEY_BLOCK = V7X_MXU_WIDTH
QUERY_TILE = 2 * KEY_BLOCK
BIG_KEY_BLOCKS = 8
GROUP_SUM_WIDTH = V7X_MXU_WIDTH
STATE_ROWS = V7X_SUBLANES


def _silu(x):
    return x / (1.0 + jnp.exp(-x))


def _proj_kernel(x_ref, st_ref, cos_ref, sin_ref, lng_ref, win_ref, qg_ref, kg_ref, cw_ref, cb_ref,
                 gmat_ref, q_ref, kf_ref, vf_ref, kb_ref, sga_ref, oc_ref, tail_ref, *rest,
                 sb, tm, tkv, emit_vt):
    if emit_vt:
        vt_ref, ubuf = rest
    else:
        (ubuf,) = rest
    j = pl.program_id(1)
    rows = sb * tm

    @pl.when(j == 0)
    def _():
        ubuf[:, 0:STATE_ROWS, :] = st_ref[...]

    @pl.when(j > 0)
    def _():
        ubuf[:, 0:STATE_ROWS, :] = ubuf[:, tm:tm + STATE_ROWS, :]

    x = x_ref[...].reshape(rows, x_ref.shape[-1])
    xg = (x * lng_ref[...]).astype(BF16)
    inv_rms = lax.rsqrt(jnp.mean(x * x, axis=-1, keepdims=True) + EPS)

    def proj(i):
        return inv_rms * jnp.dot(xg, win_ref[:, i * WIDTH:(i + 1) * WIDTH],
                                 preferred_element_type=F32)

    def group_mean_sq(t):
        tt = (t * t).astype(BF16)
        halves = [jnp.dot(tt[:, i * GROUP_SUM_WIDTH:(i + 1) * GROUP_SUM_WIDTH], gmat_ref[...],
                          preferred_element_type=F32) for i in range(WIDTH // GROUP_SUM_WIDTH)]
        return jnp.concatenate(halves, axis=1) * (1.0 / HEAD_DIM)

    cos = cos_ref[...]
    sin = sin_ref[...]
    lane = lax.broadcasted_iota(jnp.int32, (1, V_DIM), 1) % HEAD_DIM
    take_lower = lane >= ROT_DIM // 2

    def norm_rope(t, mean_sq, g):
        tn = t * lax.rsqrt(mean_sq + EPS) * g
        outs = []
        for h in range(N_HEADS):
            th = tn[:, h * V_DIM:(h + 1) * V_DIM]
            lower = pltpu.roll(th, ROT_DIM // 2, axis=1)
            upper = pltpu.roll(th, V_DIM - ROT_DIM // 2, axis=1)
            outs.append(th * cos + jnp.where(take_lower, lower, upper) * sin)
        return jnp.concatenate(outs, axis=1)

    def per_stream(a):
        return a.reshape(sb, tm, a.shape[-1])

    pq = proj(0)
    pk = proj(1)
    msq = group_mean_sq(pq)
    v = proj(2)
    msk = group_mean_sq(pk)
    ga = proj(3)
    q_ref[...] = per_stream((norm_rope(pq, msq, qg_ref[...]) * Q_SCALE).astype(BF16))
    bb = proj(4)
    k = norm_rope(pk, msk, kg_ref[...])
    kb_ref[...] = per_stream(k.astype(BF16))
    cc = proj(5)
    for b in range(sb):
        for h in range(N_HEADS):
            src = (slice(b * tm, (b + 1) * tm), slice(h * V_DIM, (h + 1) * V_DIM))
            kf_ref[b, pl.ds(h, tm, stride=N_HEADS), :] = k[src]
            vf_ref[b, pl.ds(h, tm, stride=N_HEADS), :] = v[src]
    hh = proj(6)
    if emit_vt:
        for s in range(tm // tkv):
            vt_ref[0, s] = v[s * tkv:(s + 1) * tkv, :].T.astype(BF16)
    gc = proj(7)
    sga_ref[...] = per_stream(_silu(ga).astype(BF16))

    u = cc * hh
    ys = []
    for b in range(sb):
        ubuf[b, STATE_ROWS:STATE_ROWS + tm, :] = u[b * tm:(b + 1) * tm, :]
        y = cb_ref[...]
        for tap in range(CONV_W):
            off = STATE_ROWS - (CONV_W - 1) + tap
            y = y + cw_ref[tap:tap + 1, :] * ubuf[b, off:off + tm, :]
        ys.append(y)
    y = ys[0] if sb == 1 else jnp.concatenate(ys, axis=0)
    oc_ref[...] = per_stream((_silu(gc) * (bb * y)).astype(BF16))
    tail_ref[...] = ubuf[:, tm:tm + STATE_ROWS, :]


def _proj_call(x, state8, cos_t, sin_t, ln_g, w_in_bf, qg, kg, conv_w, conv_b, gmat, *, sb, tm, tkv,
               emit_vt, row_offset=0):
    nb, t, d = x.shape
    nt = t // tm
    assert t % tm == 0 and tm % tkv == 0 and nb % sb == 0 and not (emit_vt and sb > 1)
    assert cos_t.shape[0] == sb * t
    row = lambda b, j: (b, j, 0)
    const2 = lambda b, j: (0, 0)
    in_specs = [
        pl.BlockSpec((sb, tm, d), row),
        pl.BlockSpec((sb, STATE_ROWS, WIDTH), lambda b, j: (b, 0, 0)),
        pl.BlockSpec((sb * tm, V_DIM), lambda b, j: (j, 0)),
        pl.BlockSpec((sb * tm, V_DIM), lambda b, j: (j, 0)),
        pl.BlockSpec((1, d), const2),
        pl.BlockSpec(w_in_bf.shape, const2),
        pl.BlockSpec((1, WIDTH), const2),
        pl.BlockSpec((1, WIDTH), const2),
        pl.BlockSpec((CONV_W, WIDTH), const2),
        pl.BlockSpec((1, WIDTH), const2),
        pl.BlockSpec((GROUP_SUM_WIDTH, GROUP_SUM_WIDTH), const2),
    ]
    tile = pl.BlockSpec((sb, tm, WIDTH), row)
    heads_tile = pl.BlockSpec(
        (pl.Element(sb), pl.Element(tm * N_HEADS), pl.Element(V_DIM)),
        lambda b, j: (b * sb, pl.multiple_of((row_offset + j * tm) * N_HEADS, V7X_SUBLANES), 0))
    out_shape = [
        jax.ShapeDtypeStruct((nb, t, WIDTH), BF16),
        jax.ShapeDtypeStruct((nb, (row_offset + t) * N_HEADS, V_DIM), F32),
        jax.ShapeDtypeStruct((nb, (row_offset + t) * N_HEADS, V_DIM), F32),
        jax.ShapeDtypeStruct((nb, t, WIDTH), BF16),
        jax.ShapeDtypeStruct((nb, t, WIDTH), BF16),
        jax.ShapeDtypeStruct((nb, t, WIDTH), BF16),
        jax.ShapeDtypeStruct((nb, STATE_ROWS, WIDTH), F32),
    ]
    out_specs = [tile, heads_tile, heads_tile, tile, tile, tile,
                 pl.BlockSpec((sb, STATE_ROWS, WIDTH), lambda b, j: (b, 0, 0))]
    if emit_vt:
        out_shape.append(jax.ShapeDtypeStruct((nb, t // tkv, WIDTH, tkv), BF16))
        out_specs.append(pl.BlockSpec((1, tm // tkv, WIDTH, tkv), lambda b, j: (b, j, 0, 0)))
    return pl.pallas_call(
        functools.partial(_proj_kernel, sb=sb, tm=tm, tkv=tkv, emit_vt=emit_vt),
        grid=(nb // sb, nt),
        in_specs=in_specs,
        out_specs=out_specs,
        out_shape=out_shape,
        scratch_shapes=[pltpu.VMEM((sb, tm + STATE_ROWS, WIDTH), F32)],
        compiler_params=pltpu.CompilerParams(
            dimension_semantics=("arbitrary", "arbitrary"),
            vmem_limit_bytes=V7X_VMEM_LIMIT),
        name="proj",
    )(x, state8, cos_t, sin_t, ln_g, w_in_bf, qg, kg, conv_w, conv_b, gmat)


def _lambda(lq1_ref, lk1_ref, lq2_ref, lk2_ref):
    a = jnp.sum(lq1_ref[...] * lk1_ref[...], axis=-1, keepdims=True)
    b = jnp.sum(lq2_ref[...] * lk2_ref[...], axis=-1, keepdims=True)
    return jnp.exp(a) - jnp.exp(b) + LAMBDA_INIT


def _conv_branch_out(x, oc, wout_ref):
    return x + jnp.dot(oc, wout_ref[WIDTH:, :], preferred_element_type=F32)


def _gated_attn(o_heads, sga, subg):
    parts = []
    for h, o in enumerate(o_heads):
        ms = jnp.mean(o * o, axis=-1, keepdims=True)
        on = o * lax.rsqrt(ms + EPS) * subg * (1.0 - LAMBDA_INIT)
        parts.append((sga[:, h * V_DIM:(h + 1) * V_DIM].astype(F32) * on).astype(BF16))
    return jnp.concatenate(parts, axis=1)


def _attn_branch_out(base, mix, wout_ref):
    return base + jnp.dot(mix, wout_ref[:WIDTH, :], preferred_element_type=F32)


def _prompt_attn_kernel(bound_ref, x_ref, q_ref, kb_ref, vt_ref, km_ref, vmt_ref, sga_ref, oc_ref,
                        subg_ref, lq1_ref, lk1_ref, lq2_ref, lk2_ref, wout_ref, y_ref,
                        qz_ref, m_ref, l_ref, acc_ref, *, tq, kb, big):
    j = pl.program_id(1)
    lanes = 2 * tq
    n_diag = tq // kb
    n_full = j * n_diag

    lane = lax.broadcasted_iota(jnp.int32, (1, V_DIM), 1)
    for h in range(N_HEADS):
        qh = q_ref[0, :, h * V_DIM:(h + 1) * V_DIM]
        qz_ref[h, 0:tq, :] = jnp.where(lane < HEAD_DIM, qh, jnp.zeros_like(qh))
        qz_ref[h, tq:lanes, :] = jnp.where(lane >= HEAD_DIM, qh, jnp.zeros_like(qh))

    def scores(k_h, h):
        return lax.dot_general(k_h, qz_ref[h], (((1,), (1,)), ((), ())), preferred_element_type=F32)

    def k_slab(k, h):
        return k[:, h * V_DIM:(h + 1) * V_DIM]

    def sublane_partial(e):
        return jnp.sum(e.reshape(e.shape[0] // V7X_SUBLANES, V7X_SUBLANES, lanes), axis=0)

    kchunk = lax.broadcasted_iota(jnp.int32, (kb, 1), 0) // CHUNK
    qchunk = (lax.broadcasted_iota(jnp.int32, (1, lanes), 1) % tq) // CHUNK
    diag_masks = [kchunk + d * (kb // CHUNK) <= qchunk for d in range(n_diag)]

    def late_queries(a, d, axis):
        lo = d * kb
        if axis == 0:
            return jnp.concatenate([a[lo:tq], a[tq + lo:lanes]], axis=0)
        return jnp.concatenate([a[:, lo:tq], a[:, tq + lo:lanes]], axis=1)

    def add_late(full, part, d):
        lo, w = d * kb, tq - d * kb
        return jnp.concatenate([full[:, 0:lo], full[:, lo:tq] + part[:, 0:w],
                                full[:, tq:tq + lo], full[:, tq + lo:lanes] + part[:, w:2 * w]],
                               axis=1)

    @pl.when(bound_ref[0] <= SAFE_SCORE_BOUND)
    def _():
        def key_pass(k, pieces, assign, late_blocks=()):
            s_next = scores(k_slab(k, 0), 0)
            for h in range(N_HEADS):
                s = s_next
                if h + 1 < N_HEADS:
                    s_next = scores(k_slab(k, h + 1), h + 1)
                s_late = [lax.dot_general(k_slab(kl, h), late_queries(qz_ref[h], d, 0),
                                          (((1,), (1,)), ((), ())), preferred_element_type=F32)
                          for d, kl, _ in late_blocks]
                e = jnp.exp2(s)
                l_new = acc_new = None
                for row0, rows, vt, mask in pieces:
                    ep = e[row0:row0 + rows, :]
                    if mask is not None:
                        ep = jnp.where(mask, ep, 0.0)
                    lp = sublane_partial(ep)
                    ap = jnp.dot(vt[h * V_DIM:(h + 1) * V_DIM, :], ep.astype(BF16),
                                 preferred_element_type=F32)
                    l_new = lp if l_new is None else l_new + lp
                    acc_new = ap if acc_new is None else acc_new + ap
                for (d, _, vt), sl in zip(late_blocks, s_late):
                    ep = jnp.where(late_queries(diag_masks[d], d, 1), jnp.exp2(sl), 0.0)
                    lp = jnp.sum(ep.reshape(kb // V7X_SUBLANES, V7X_SUBLANES, ep.shape[1]), axis=0)
                    ap = jnp.dot(vt[h * V_DIM:(h + 1) * V_DIM, :], ep.astype(BF16),
                                 preferred_element_type=F32)
                    l_new = add_late(l_new, lp, d)
                    acc_new = add_late(acc_new, ap, d)
                if assign:
                    l_ref[h] = l_new
                    acc_ref[h] = acc_new
                else:
                    l_ref[h] += l_new
                    acc_ref[h] += acc_new

        n_big = n_full // big
        rem = n_full - n_big * big
        for r in range(0, big, n_diag):
            @pl.when(rem == r)
            def _():
                first = n_full - r
                n_blocks = r + 1
                kblocks = kb_ref[0, pl.ds(first, n_blocks)].reshape(n_blocks * kb, WIDTH)
                k = jnp.concatenate([km_ref[...], kblocks], axis=0)
                pieces = [(0, N_META, vmt_ref[...], None)]
                pieces += [(N_META + t * kb, kb, vt_ref[0, first + t],
                            None if t < r else diag_masks[0]) for t in range(n_blocks)]
                late = [(d, kb_ref[0, n_full + d], vt_ref[0, n_full + d])
                        for d in range(1, n_diag)]
                key_pass(k, pieces, assign=True, late_blocks=late)

        def big_block(i, carry):
            first = i * big
            k = kb_ref[0, pl.ds(first, big)].reshape(big * kb, WIDTH)
            key_pass(k, [(t * kb, kb, vt_ref[0, first + t], None) for t in range(big)], assign=False)
            return carry

        lax.fori_loop(0, n_big, big_block, 0)

    @pl.when(bound_ref[0] > SAFE_SCORE_BOUND)
    def _():
        for h in range(N_HEADS):
            s = scores(k_slab(km_ref, h), h)
            m = jnp.max(s, axis=0, keepdims=True)
            e = jnp.exp2(s - m)
            m_ref[h:h + 1, :] = m
            l_ref[h] = jnp.zeros((V7X_SUBLANES, lanes), F32)
            l_ref[h, 0:1, :] = jnp.sum(e, axis=0, keepdims=True)
            acc_ref[h] = jnp.dot(vmt_ref[h * V_DIM:(h + 1) * V_DIM, :], e.astype(BF16),
                                 preferred_element_type=F32)

        def block(i, mask):
            k = kb_ref[0, i]
            vt = vt_ref[0, i]
            for h in range(N_HEADS):
                s = scores(k_slab(k, h), h)
                if mask is not None:
                    s = jnp.where(mask, s, NEG_BIG)
                m_old = m_ref[h:h + 1, :]
                m_new = jnp.maximum(m_old, jnp.max(s, axis=0, keepdims=True))
                alpha = jnp.exp2(m_old - m_new)
                e = jnp.exp2(s - m_new)
                l_ref[h, 0:1, :] = alpha * l_ref[h, 0:1, :] + jnp.sum(e, axis=0, keepdims=True)
                acc_ref[h] = alpha * acc_ref[h] + jnp.dot(
                    vt[h * V_DIM:(h + 1) * V_DIM, :], e.astype(BF16), preferred_element_type=F32)
                m_ref[h:h + 1, :] = m_new

        def full_block(i, carry):
            block(i, None)
            return carry

        lax.fori_loop(0, n_full, full_block, 0)
        for d in range(n_diag):
            block(n_full + d, diag_masks[d])

    base = _conv_branch_out(x_ref[0], oc_ref[0], wout_ref)
    lam = _lambda(lq1_ref, lk1_ref, lq2_ref, lk2_ref)
    o_heads = []
    for h in range(N_HEADS):
        o2 = acc_ref[h] / jnp.sum(l_ref[h], axis=0, keepdims=True)
        o_heads.append((o2[:, 0:tq] - lam * o2[:, tq:lanes]).T)
    y_ref[0] = _attn_branch_out(base, _gated_attn(o_heads, sga_ref[0], subg_ref[...]), wout_ref)


def _prompt_attn(bound, x, q, kb4, vt4, km_bf, vmt_bf, sga, oc, subg, lq1, lk1, lq2, lk2, w_out_bf,
                 *, tq, kb, big):
    nb, s, d = x.shape
    nq = s // tq
    assert tq % kb == 0 and big % (tq // kb) == 0
    resident = dict(pipeline_mode=pl.Buffered(1))
    row = lambda b, j: (b, j, 0)
    const2 = lambda b, j: (0, 0)
    whole = lambda b, j: (b, 0, 0, 0)
    in_specs = [
        pl.BlockSpec(memory_space=pltpu.SMEM),
        pl.BlockSpec((1, tq, d), row),
        pl.BlockSpec((1, tq, WIDTH), row),
        pl.BlockSpec((1, s // kb, kb, WIDTH), whole),
        pl.BlockSpec((1, s // kb, WIDTH, kb), whole, **resident),
        pl.BlockSpec((N_META, WIDTH), const2),
        pl.BlockSpec((WIDTH, N_META), const2),
        pl.BlockSpec((1, tq, WIDTH), row),
        pl.BlockSpec((1, tq, WIDTH), row),
        pl.BlockSpec((1, V_DIM), const2),
        pl.BlockSpec((1, HEAD_DIM), const2),
        pl.BlockSpec((1, HEAD_DIM), const2),
        pl.BlockSpec((1, HEAD_DIM), const2),
        pl.BlockSpec((1, HEAD_DIM), const2),
        pl.BlockSpec(w_out_bf.shape, const2),
    ]
    return pl.pallas_call(
        functools.partial(_prompt_attn_kernel, tq=tq, kb=kb, big=big),
        grid=(nb, nq),
        in_specs=in_specs,
        out_specs=pl.BlockSpec((1, tq, d), row),
        out_shape=jax.ShapeDtypeStruct((nb, s, d), F32),
        scratch_shapes=[
            pltpu.VMEM((N_HEADS, 2 * tq, V_DIM), BF16),
            pltpu.VMEM((N_HEADS, 2 * tq), F32),
            pltpu.VMEM((N_HEADS, V7X_SUBLANES, 2 * tq), F32),
            pltpu.VMEM((N_HEADS, V_DIM, 2 * tq), F32),
        ],
        compiler_params=pltpu.CompilerParams(
            dimension_semantics=("arbitrary", "arbitrary"),
            vmem_limit_bytes=V7X_VMEM_LIMIT),
        name="prompt_attn",
    )(bound, x, q, kb4, vt4, km_bf, vmt_bf, sga, oc, subg, lq1, lk1, lq2, lk2, w_out_bf)


def _sample_attn_kernel(x_ref, q_ref, kn_ref, vn_ref, ck_ref, cv_ref, km_ref, vm_ref, sga_ref,
                        oc_ref, subg_ref, lq1_ref, lk1_ref, lq2_ref, lk2_ref, wout_ref, y_ref,
                        kcat, vcat, mix_ref, *, past, t_new, n_keys_pad, first):
    b = pl.program_id(0)
    nb = pl.num_programs(0)
    n_keys = past + N_META + t_new
    for h in range(N_HEADS):
        slab = slice(h * V_DIM, (h + 1) * V_DIM)
        kcat[0:past, slab] = ck_ref[0, pl.ds(h, past, stride=N_HEADS), :].astype(BF16)
        kcat[past + N_META:n_keys, slab] = kn_ref[0, pl.ds(h, t_new, stride=N_HEADS), :].astype(BF16)
        vcat[0:past, slab] = cv_ref[0, pl.ds(h, past, stride=N_HEADS), :].astype(BF16)
        vcat[past:past + N_META, slab] = vm_ref[pl.ds(h, N_META, stride=N_HEADS), :].astype(BF16)
        vcat[past + N_META:n_keys, slab] = vn_ref[0, pl.ds(h, t_new, stride=N_HEADS), :].astype(BF16)
    kcat[past:past + N_META, :] = km_ref[...]
    kcat[n_keys:n_keys_pad, :] = jnp.zeros((n_keys_pad - n_keys, WIDTH), BF16)
    vcat[n_keys:n_keys_pad, :] = jnp.zeros((n_keys_pad - n_keys, WIDTH), BF16)

    lane = lax.broadcasted_iota(jnp.int32, (1, V_DIM), 1)
    valid = lax.broadcasted_iota(jnp.int32, (1, n_keys_pad), 1) < n_keys
    lam = _lambda(lq1_ref, lk1_ref, lq2_ref, lk2_ref)
    o_heads = []
    for h in range(N_HEADS):
        qh = q_ref[0, :, h * V_DIM:(h + 1) * V_DIM]
        q2 = jnp.concatenate([jnp.where(lane < HEAD_DIM, qh, jnp.zeros_like(qh)),
                              jnp.where(lane >= HEAD_DIM, qh, jnp.zeros_like(qh))], axis=0)
        s = lax.dot_general(q2, kcat[:, h * V_DIM:(h + 1) * V_DIM], (((1,), (1,)), ((), ())),
                            preferred_element_type=F32)
        s = jnp.where(valid, s, NEG_BIG)
        e = jnp.exp2(s - jnp.max(s, axis=-1, keepdims=True))
        o2 = jnp.dot(e.astype(BF16), vcat[:, h * V_DIM:(h + 1) * V_DIM],
                     preferred_element_type=F32) / jnp.sum(e, axis=-1, keepdims=True)
        o_heads.append(o2[0:t_new] - lam * o2[t_new:2 * t_new])
    rows = pl.ds(pl.multiple_of(b * t_new, t_new), t_new)
    mix_ref[rows, :] = _gated_attn(o_heads, sga_ref[b + first], subg_ref[...])

    @pl.when(b == nb - 1)
    def _():
        n_rows = mix_ref.shape[0]
        oc = oc_ref[first:first + n_rows // t_new]
        base = _conv_branch_out(x_ref[...].reshape(n_rows, x_ref.shape[-1]),
                                oc.reshape(n_rows, WIDTH), wout_ref)
        y_ref[...] = _attn_branch_out(base, mix_ref[...], wout_ref).reshape(y_ref.shape)


def _sample_attn(x, q, kn, vn, ck, cv, km_bf, vm, sga, oc, subg, lq1, lk1, lq2, lk2, w_out_bf, *,
                 first):
    nb, t_new, d = x.shape
    n_proj = q.shape[0]
    past = ck.shape[1] // N_HEADS
    n_keys_pad = -(-(past + N_META + t_new) // V_DIM) * V_DIM
    row = lambda b: (b, 0, 0)
    proj_row = lambda b: (b + first, 0, 0)
    const2 = lambda b: (0, 0)
    const3 = lambda b: (0, 0, 0)
    in_specs = [
        pl.BlockSpec((nb, t_new, d), const3),
        pl.BlockSpec((1, t_new, WIDTH), proj_row),
        pl.BlockSpec((1, t_new * N_HEADS, V_DIM), proj_row),
        pl.BlockSpec((1, t_new * N_HEADS, V_DIM), proj_row),
        pl.BlockSpec((1, past * N_HEADS, V_DIM), row),
        pl.BlockSpec((1, past * N_HEADS, V_DIM), row),
        pl.BlockSpec((N_META, WIDTH), const2),
        pl.BlockSpec((N_META * N_HEADS, V_DIM), const2),
        pl.BlockSpec((n_proj, t_new, WIDTH), const3),
        pl.BlockSpec((n_proj, t_new, WIDTH), const3),
        pl.BlockSpec((1, V_DIM), const2),
        pl.BlockSpec((1, HEAD_DIM), const2),
        pl.BlockSpec((1, HEAD_DIM), const2),
        pl.BlockSpec((1, HEAD_DIM), const2),
        pl.BlockSpec((1, HEAD_DIM), const2),
        pl.BlockSpec(w_out_bf.shape, const2),
    ]
    return pl.pallas_call(
        functools.partial(_sample_attn_kernel, past=past, t_new=t_new, n_keys_pad=n_keys_pad,
                          first=first),
        grid=(nb,),
        in_specs=in_specs,
        out_specs=pl.BlockSpec((nb, t_new, d), const3),
        out_shape=jax.ShapeDtypeStruct((nb, t_new, d), F32),
        scratch_shapes=[pltpu.VMEM((n_keys_pad, WIDTH), BF16),
                        pltpu.VMEM((n_keys_pad, WIDTH), BF16),
                        pltpu.VMEM((nb * t_new, WIDTH), BF16)],
        compiler_params=pltpu.CompilerParams(
            dimension_semantics=("arbitrary",),
            vmem_limit_bytes=V7X_VMEM_LIMIT),
        name="sample_attn",
    )(x, q, kn, vn, ck, cv, km_bf, vm, sga, oc, subg, lq1, lk1, lq2, lk2, w_out_bf)


def _rope_tables(pos):
    half = ROT_DIM // 2
    inv = ROPE_THETA ** (-jnp.arange(0, ROT_DIM, 2, dtype=F32) / ROT_DIM)
    ang = pos.astype(F32)[:, None] * inv[None, :]
    cos, sin = lax.optimization_barrier((jnp.cos(ang), jnp.sin(ang)))
    t = pos.shape[0]
    rest = HEAD_DIM - 2 * half
    c64 = jnp.concatenate([cos, cos, jnp.ones((t, rest), F32)], axis=1)
    s64 = jnp.concatenate([-sin, sin, jnp.zeros((t, rest), F32)], axis=1)
    return jnp.tile(c64, (1, 2)), jnp.tile(s64, (1, 2))


def _score_bound(qg, kg):
    bound = Q_SCALE * HEAD_DIM * jnp.max(jnp.abs(qg)) * jnp.max(jnp.abs(kg)) * (1.0 + 2.0 ** -6)
    return bound.reshape(1).astype(F32)


def _pad_state(state):
    return jnp.pad(state, ((0, 0), (STATE_ROWS - (CONV_W - 1), 0), (0, 0)))


def _tile_sizes(seq):
    kb = KEY_BLOCK if seq % KEY_BLOCK == 0 else seq
    tq = QUERY_TILE if seq % QUERY_TILE == 0 else kb
    return tq, tq, kb


def kernel(x_prompt, x_sample, cache_k, cache_v, state_conv, meta_tokens, ln_g, w_in, q_norm_g,
           k_norm_g, lam_q1, lam_k1, lam_q2, lam_k2, subln_g, conv_w, conv_b, w_out):
    assert ln_g.shape[0] == 1, "single-layer stack"
    nb, seq, d = x_prompt.shape
    db, t_new, _ = x_sample.shape
    past = cache_k.shape[2]

    w_in_bf = w_in[0].astype(BF16)
    w_out_bf = w_out[0].astype(BF16)
    lng = ln_g[0][None, :]
    qg = jnp.tile(q_norm_g[0], 2 * N_HEADS)[None, :]
    kg = jnp.tile(k_norm_g[0], 2 * N_HEADS)[None, :]
    subg = subln_g[0][None, :]
    cw, cb = conv_w[0], conv_b[0][None, :]
    lams = [p[0][None, :] for p in (lam_q1, lam_k1, lam_q2, lam_k2)]
    grp = jnp.arange(GROUP_SUM_WIDTH, dtype=jnp.int32) // HEAD_DIM
    gmat = (grp[:, None] == grp[None, :]).astype(BF16)

    pos_m = jnp.arange(N_META, dtype=jnp.int32)
    pos_p = N_META + jnp.arange(seq, dtype=jnp.int32)
    pos_s = N_META + past + jnp.arange(t_new, dtype=jnp.int32)
    weights = (lng, w_in_bf, qg, kg, cw, cb, gmat)
    tm, tq, kb = _tile_sizes(seq)

    assert t_new == N_META, "meta tokens and new frames are projected as equal-length streams"
    cos_m, sin_m = _rope_tables(pos_m)
    cos_s, sin_s = _rope_tables(pos_s)
    n_small = 1 + db
    q_sm, k_sm, v_sm, kb_sm, sga_sm, oc_sm, tail_sm = _proj_call(
        jnp.concatenate([meta_tokens[None], x_sample], axis=0),
        jnp.concatenate([jnp.zeros((1, STATE_ROWS, WIDTH), F32), _pad_state(state_conv[0])], axis=0),
        jnp.concatenate([cos_m] + [cos_s] * db, axis=0),
        jnp.concatenate([sin_m] + [sin_s] * db, axis=0),
        *weights, sb=n_small, tm=t_new, tkv=t_new, emit_vt=False)
    km, vm, km_bf, tail_m = k_sm[0], v_sm[0], kb_sm[0], tail_sm[0:1]
    ks, vs, tail_s = k_sm[1:], v_sm[1:], tail_sm[1:]

    state_p = jnp.broadcast_to(tail_m, (nb, STATE_ROWS, WIDTH))
    qp, kp, vp, kp_bf, sga_p, oc_p, tail_p, vtp = _proj_call(
        x_prompt, state_p, *_rope_tables(pos_p), *weights, sb=1, tm=tm, tkv=kb, emit_vt=True,
        row_offset=N_META)
    y_prompt = _prompt_attn(
        _score_bound(q_norm_g[0], k_norm_g[0]), x_prompt, qp,
        kp_bf.reshape(nb, seq // kb, kb, WIDTH), vtp, km_bf,
        vm.reshape(N_META, WIDTH).T.astype(BF16),
        sga_p, oc_p, subg, *lams, w_out_bf, tq=tq, kb=kb, big=BIG_KEY_BLOCKS)
    kp = kp.at[:, :N_META * N_HEADS].set(jnp.broadcast_to(km[None], (nb,) + km.shape))
    vp = vp.at[:, :N_META * N_HEADS].set(jnp.broadcast_to(vm[None], (nb,) + vm.shape))

    y_sample = _sample_attn(
        x_sample, q_sm, k_sm, v_sm, cache_k[0].reshape(db, past * N_HEADS, V_DIM),
        cache_v[0].reshape(db, past * N_HEADS, V_DIM), km_bf, vm, sga_sm, oc_sm, subg, *lams,
        w_out_bf, first=1)

    def heads_form(a, rows):
        return a.reshape(1, a.shape[0], rows, N_HEADS, V_DIM)

    tail_rows = slice(STATE_ROWS - (CONV_W - 1), STATE_ROWS)
    return (y_prompt, y_sample, heads_form(kp, N_META + seq), heads_form(vp, N_META + seq),
            tail_p[None, :, tail_rows], heads_form(ks, t_new), heads_form(vs, t_new),
            tail_s[None, :, tail_rows])
```

```python
import functools
import math

import jax
import jax.numpy as jnp
from jax import lax
from jax.experimental import pallas as pl
from jax.experimental.pallas import tpu as pltpu

F32 = jnp.float32
BF16 = jnp.bfloat16

CHUNK = 64
N_META = 16
N_HEADS = 4
HEAD_DIM = 64
V_DIM = 2 * HEAD_DIM
WIDTH = N_HEADS * V_DIM
CONV_W = 3
ROT_DIM = HEAD_DIM // 4
ROPE_THETA = 500000.0
EPS = 1e-6
LAMBDA_INIT = 0.8 - 0.6 * math.exp(-0.3 * 0)
Q_SCALE = HEAD_DIM ** -0.5 * math.log2(math.e)
NEG_BIG = -1e30
SAFE_SCORE_BOUND = 40.0
V7X_SUBLANES = 8
V7X_MXU_WIDTH = 256
V7X_VMEM_LIMIT = 60 * 1024 * 1024

KEY_BLOCK = V7X_MXU_WIDTH
QUERY_TILE = 2 * KEY_BLOCK
BIG_KEY_BLOCKS = 8
GROUP_SUM_WIDTH = V7X_MXU_WIDTH
STATE_ROWS = V7X_SUBLANES


def _silu(x):
    return x / (1.0 + jnp.exp(-x))


def _proj_kernel(x_ref, st_ref, cos_ref, sin_ref, lng_ref, win_ref, qg_ref, kg_ref, cw_ref, cb_ref,
                 gmat_ref, q_ref, kf_ref, vf_ref, kb_ref, sga_ref, oc_ref, tail_ref, *rest,
                 sb, tm, tkv, emit_vt):
    if emit_vt:
        vt_ref, ubuf = rest
    else:
        (ubuf,) = rest
    j = pl.program_id(1)
    rows = sb * tm

    @pl.when(j == 0)
    def _():
        ubuf[:, 0:STATE_ROWS, :] = st_ref[...]

    @pl.when(j > 0)
    def _():
        ubuf[:, 0:STATE_ROWS, :] = ubuf[:, tm:tm + STATE_ROWS, :]

    x = x_ref[...].reshape(rows, x_ref.shape[-1])
    xg = (x * lng_ref[...]).astype(BF16)
    inv_rms = lax.rsqrt(jnp.mean(x * x, axis=-1, keepdims=True) + EPS)

    def proj(i):
        return inv_rms * jnp.dot(xg, win_ref[:, i * WIDTH:(i + 1) * WIDTH],
                                 preferred_element_type=F32)

    def group_mean_sq(t):
        tt = (t * t).astype(BF16)
        halves = [jnp.dot(tt[:, i * GROUP_SUM_WIDTH:(i + 1) * GROUP_SUM_WIDTH], gmat_ref[...],
                          preferred_element_type=F32) for i in range(WIDTH // GROUP_SUM_WIDTH)]
        return jnp.concatenate(halves, axis=1) * (1.0 / HEAD_DIM)

    cos = cos_ref[...]
    sin = sin_ref[...]
    lane = lax.broadcasted_iota(jnp.int32, (1, V_DIM), 1) % HEAD_DIM
    take_lower = lane >= ROT_DIM // 2

    def norm_rope(t, mean_sq, g):
        tn = t * lax.rsqrt(mean_sq + EPS) * g
        outs = []
        for h in range(N_HEADS):
            th = tn[:, h * V_DIM:(h + 1) * V_DIM]
            lower = pltpu.roll(th, ROT_DIM // 2, axis=1)
            upper = pltpu.roll(th, V_DIM - ROT_DIM // 2, axis=1)
            outs.append(th * cos + jnp.where(take_lower, lower, upper) * sin)
        return jnp.concatenate(outs, axis=1)

    def per_stream(a):
        return a.reshape(sb, tm, a.shape[-1])

    pq = proj(0)
    pk = proj(1)
    msq = group_mean_sq(pq)
    v = proj(2)
    msk = group_mean_sq(pk)
    ga = proj(3)
    q_ref[...] = per_stream((norm_rope(pq, msq, qg_ref[...]) * Q_SCALE).astype(BF16))
    bb = proj(4)
    k = norm_rope(pk, msk, kg_ref[...])
    kb_ref[...] = per_stream(k.astype(BF16))
    cc = proj(5)
    for b in range(sb):
        for h in range(N_HEADS):
            src = (slice(b * tm, (b + 1) * tm), slice(h * V_DIM, (h + 1) * V_DIM))
            kf_ref[b, pl.ds(h, tm, stride=N_HEADS), :] = k[src]
            vf_ref[b, pl.ds(h, tm, stride=N_HEADS), :] = v[src]
    hh = proj(6)
    if emit_vt:
        for s in range(tm // tkv):
            vt_ref[0, s] = v[s * tkv:(s + 1) * tkv, :].T.astype(BF16)
    gc = proj(7)
    sga_ref[...] = per_stream(_silu(ga).astype(BF16))

    u = cc * hh
    ys = []
    for b in range(sb):
        ubuf[b, STATE_ROWS:STATE_ROWS + tm, :] = u[b * tm:(b + 1) * tm, :]
        y = cb_ref[...]
        for tap in range(CONV_W):
            off = STATE_ROWS - (CONV_W - 1) + tap
            y = y + cw_ref[tap:tap + 1, :] * ubuf[b, off:off + tm, :]
        ys.append(y)
    y = ys[0] if sb == 1 else jnp.concatenate(ys, axis=0)
    oc_ref[...] = per_stream((_silu(gc) * (bb * y)).astype(BF16))
    tail_ref[...] = ubuf[:, tm:tm + STATE_ROWS, :]


def _proj_call(x, state8, cos_t, sin_t, ln_g, w_in_bf, qg, kg, conv_w, conv_b, gmat, *, sb, tm, tkv,
               emit_vt, row_offset=0):
    nb, t, d = x.shape
    nt = t // tm
    assert t % tm == 0 and tm % tkv == 0 and nb % sb == 0 and not (emit_vt and sb > 1)
    assert cos_t.shape[0] == sb * t
    row = lambda b, j: (b, j, 0)
    const2 = lambda b, j: (0, 0)
    in_specs = [
        pl.BlockSpec((sb, tm, d), row),
        pl.BlockSpec((sb, STATE_ROWS, WIDTH), lambda b, j: (b, 0, 0)),
        pl.BlockSpec((sb * tm, V_DIM), lambda b, j: (j, 0)),
        pl.BlockSpec((sb * tm, V_DIM), lambda b, j: (j, 0)),
        pl.BlockSpec((1, d), const2),
        pl.BlockSpec(w_in_bf.shape, const2),
        pl.BlockSpec((1, WIDTH), const2),
        pl.BlockSpec((1, WIDTH), const2),
        pl.BlockSpec((CONV_W, WIDTH), const2),
        pl.BlockSpec((1, WIDTH), const2),
        pl.BlockSpec((GROUP_SUM_WIDTH, GROUP_SUM_WIDTH), const2),
    ]
    tile = pl.BlockSpec((sb, tm, WIDTH), row)
    heads_tile = pl.BlockSpec(
        (pl.Element(sb), pl.Element(tm * N_HEADS), pl.Element(V_DIM)),
        lambda b, j: (b * sb, pl.multiple_of((row_offset + j * tm) * N_HEADS, V7X_SUBLANES), 0))
    out_shape = [
        jax.ShapeDtypeStruct((nb, t, WIDTH), BF16),
        jax.ShapeDtypeStruct((nb, (row_offset + t) * N_HEADS, V_DIM), F32),
        jax.ShapeDtypeStruct((nb, (row_offset + t) * N_HEADS, V_DIM), F32),
        jax.ShapeDtypeStruct((nb, t, WIDTH), BF16),
        jax.ShapeDtypeStruct((nb, t, WIDTH), BF16),
        jax.ShapeDtypeStruct((nb, t, WIDTH), BF16),
        jax.ShapeDtypeStruct((nb, STATE_ROWS, WIDTH), F32),
    ]
    out_specs = [tile, heads_tile, heads_tile, tile, tile, tile,
                 pl.BlockSpec((sb, STATE_ROWS, WIDTH), lambda b, j: (b, 0, 0))]
    if emit_vt:
        out_shape.append(jax.ShapeDtypeStruct((nb, t // tkv, WIDTH, tkv), BF16))
        out_specs.append(pl.BlockSpec((1, tm // tkv, WIDTH, tkv), lambda b, j: (b, j, 0, 0)))
    return pl.pallas_call(
        functools.partial(_proj_kernel, sb=sb, tm=tm, tkv=tkv, emit_vt=emit_vt),
        grid=(nb // sb, nt),
        in_specs=in_specs,
        out_specs=out_specs,
        out_shape=out_shape,
        scratch_shapes=[pltpu.VMEM((sb, tm + STATE_ROWS, WIDTH), F32)],
        compiler_params=pltpu.CompilerParams(
            dimension_semantics=("arbitrary", "arbitrary"),
            vmem_limit_bytes=V7X_VMEM_LIMIT),
        name="proj",
    )(x, state8, cos_t, sin_t, ln_g, w_in_bf, qg, kg, conv_w, conv_b, gmat)


def _lambda(lq1_ref, lk1_ref, lq2_ref, lk2_ref):
    a = jnp.sum(lq1_ref[...] * lk1_ref[...], axis=-1, keepdims=True)
    b = jnp.sum(lq2_ref[...] * lk2_ref[...], axis=-1, keepdims=True)
    return jnp.exp(a) - jnp.exp(b) + LAMBDA_INIT


def _conv_branch_out(x, oc, wout_ref):
    return x + jnp.dot(oc, wout_ref[WIDTH:, :], preferred_element_type=F32)


def _gated_attn(o_heads, sga, subg):
    parts = []
    for h, o in enumerate(o_heads):
        ms = jnp.mean(o * o, axis=-1, keepdims=True)
        on = o * lax.rsqrt(ms + EPS) * subg * (1.0 - LAMBDA_INIT)
        parts.append((sga[:, h * V_DIM:(h + 1) * V_DIM].astype(F32) * on).astype(BF16))
    return jnp.concatenate(parts, axis=1)


def _attn_branch_out(base, mix, wout_ref):
    return base + jnp.dot(mix, wout_ref[:WIDTH, :], preferred_element_type=F32)


def _prompt_attn_kernel(bound_ref, x_ref, q_ref, kb_ref, vt_ref, km_ref, vmt_ref, sga_ref, oc_ref,
                        subg_ref, lq1_ref, lk1_ref, lq2_ref, lk2_ref, wout_ref, y_ref,
                        qz_ref, m_ref, l_ref, acc_ref, *, tq, kb, big):
    j = pl.program_id(1)
    lanes = 2 * tq
    n_diag = tq // kb
    n_full = j * n_diag

    lane = lax.broadcasted_iota(jnp.int32, (1, V_DIM), 1)
    for h in range(N_HEADS):
        qh = q_ref[0, :, h * V_DIM:(h + 1) * V_DIM]
        qz_ref[h, 0:tq, :] = jnp.where(lane < HEAD_DIM, qh, jnp.zeros_like(qh))
        qz_ref[h, tq:lanes, :] = jnp.where(lane >= HEAD_DIM, qh, jnp.zeros_like(qh))

    def scores(k_h, h):
        return lax.dot_general(k_h, qz_ref[h], (((1,), (1,)), ((), ())), preferred_element_type=F32)

    def k_slab(k, h):
        return k[:, h * V_DIM:(h + 1) * V_DIM]

    def sublane_partial(e):
        return jnp.sum(e.reshape(e.shape[0] // V7X_SUBLANES, V7X_SUBLANES, lanes), axis=0)

    kchunk = lax.broadcasted_iota(jnp.int32, (kb, 1), 0) // CHUNK
    qchunk = (lax.broadcasted_iota(jnp.int32, (1, lanes), 1) % tq) // CHUNK
    diag_masks = [kchunk + d * (kb // CHUNK) <= qchunk for d in range(n_diag)]

    def late_queries(a, d, axis):
        lo = d * kb
        if axis == 0:
            return jnp.concatenate([a[lo:tq], a[tq + lo:lanes]], axis=0)
        return jnp.concatenate([a[:, lo:tq], a[:, tq + lo:lanes]], axis=1)

    def add_late(full, part, d):
        lo, w = d * kb, tq - d * kb
        return jnp.concatenate([full[:, 0:lo], full[:, lo:tq] + part[:, 0:w],
                                full[:, tq:tq + lo], full[:, tq + lo:lanes] + part[:, w:2 * w]],
                               axis=1)

    @pl.when(bound_ref[0] <= SAFE_SCORE_BOUND)
    def _():
        def key_pass(k, pieces, assign, late_blocks=()):
            s_next = scores(k_slab(k, 0), 0)
            for h in range(N_HEADS):
                s = s_next
                if h + 1 < N_HEADS:
                    s_next = scores(k_slab(k, h + 1), h + 1)
                s_late = [lax.dot_general(k_slab(kl, h), late_queries(qz_ref[h], d, 0),
                                          (((1,), (1,)), ((), ())), preferred_element_type=F32)
                          for d, kl, _ in late_blocks]
                e = jnp.exp2(s)
                l_new = acc_new = None
                for row0, rows, vt, mask in pieces:
                    ep = e[row0:row0 + rows, :]
                    if mask is not None:
                        ep = jnp.where(mask, ep, 0.0)
                    lp = sublane_partial(ep)
                    ap = jnp.dot(vt[h * V_DIM:(h + 1) * V_DIM, :], ep.astype(BF16),
                                 preferred_element_type=F32)
                    l_new = lp if l_new is None else l_new + lp
                    acc_new = ap if acc_new is None else acc_new + ap
                for (d, _, vt), sl in zip(late_blocks, s_late):
                    ep = jnp.where(late_queries(diag_masks[d], d, 1), jnp.exp2(sl), 0.0)
                    lp = jnp.sum(ep.reshape(kb // V7X_SUBLANES, V7X_SUBLANES, ep.shape[1]), axis=0)
                    ap = jnp.dot(vt[h * V_DIM:(h + 1) * V_DIM, :], ep.astype(BF16),
                                 preferred_element_type=F32)
                    l_new = add_late(l_new, lp, d)
                    acc_new = add_late(acc_new, ap, d)
                if assign:
                    l_ref[h] = l_new
                    acc_ref[h] = acc_new
                else:
                    l_ref[h] += l_new
                    acc_ref[h] += acc_new

        n_big = n_full // big
        rem = n_full - n_big * big
        for r in range(0, big, n_diag):
            @pl.when(rem == r)
            def _():
                first = n_full - r
                n_blocks = r + 1
                kblocks = kb_ref[0, pl.ds(first, n_blocks)].reshape(n_blocks * kb, WIDTH)
                k = jnp.concatenate([km_ref[...], kblocks], axis=0)
                pieces = [(0, N_META, vmt_ref[...], None)]
                pieces += [(N_META + t * kb, kb, vt_ref[0, first + t],
                            None if t < r else diag_masks[0]) for t in range(n_blocks)]
                late = [(d, kb_ref[0, n_full + d], vt_ref[0, n_full + d])
                        for d in range(1, n_diag)]
                key_pass(k, pieces, assign=True, late_blocks=late)

        def big_block(i, carry):
            first = i * big
            k = kb_ref[0, pl.ds(first, big)].reshape(big * kb, WIDTH)
            key_pass(k, [(t * kb, kb, vt_ref[0, first + t], None) for t in range(big)], assign=False)
            return carry

        lax.fori_loop(0, n_big, big_block, 0)

    @pl.when(jnp.logical_not(bound_ref[0] <= SAFE_SCORE_BOUND))
    def _():
        for h in range(N_HEADS):
            s = scores(k_slab(km_ref, h), h)
            m = jnp.max(s, axis=0, keepdims=True)
            e = jnp.exp2(s - m)
            m_ref[h:h + 1, :] = m
            l_ref[h] = jnp.zeros((V7X_SUBLANES, lanes), F32)
            l_ref[h, 0:1, :] = jnp.sum(e, axis=0, keepdims=True)
            acc_ref[h] = jnp.dot(vmt_ref[h * V_DIM:(h + 1) * V_DIM, :], e.astype(BF16),
                                 preferred_element_type=F32)

        def block(i, mask):
            k = kb_ref[0, i]
            vt = vt_ref[0, i]
            for h in range(N_HEADS):
                s = scores(k_slab(k, h), h)
                if mask is not None:
                    s = jnp.where(mask, s, NEG_BIG)
                m_old = m_ref[h:h + 1, :]
                m_new = jnp.maximum(m_old, jnp.max(s, axis=0, keepdims=True))
                alpha = jnp.exp2(m_old - m_new)
                e = jnp.exp2(s - m_new)
                l_ref[h, 0:1, :] = alpha * l_ref[h, 0:1, :] + jnp.sum(e, axis=0, keepdims=True)
                acc_ref[h] = alpha * acc_ref[h] + jnp.dot(
                    vt[h * V_DIM:(h + 1) * V_DIM, :], e.astype(BF16), preferred_element_type=F32)
                m_ref[h:h + 1, :] = m_new

        def full_block(i, carry):
            block(i, None)
            return carry

        lax.fori_loop(0, n_full, full_block, 0)
        for d in range(n_diag):
            block(n_full + d, diag_masks[d])

    base = _conv_branch_out(x_ref[0], oc_ref[0], wout_ref)
    lam = _lambda(lq1_ref, lk1_ref, lq2_ref, lk2_ref)
    o_heads = []
    for h in range(N_HEADS):
        o2 = acc_ref[h] / jnp.sum(l_ref[h], axis=0, keepdims=True)
        o_heads.append((o2[:, 0:tq] - lam * o2[:, tq:lanes]).T)
    y_ref[0] = _attn_branch_out(base, _gated_attn(o_heads, sga_ref[0], subg_ref[...]), wout_ref)


def _prompt_attn(bound, x, q, kb4, vt4, km_bf, vmt_bf, sga, oc, subg, lq1, lk1, lq2, lk2, w_out_bf,
                 *, tq, kb, big):
    nb, s, d = x.shape
    nq = s // tq
    assert tq % kb == 0 and big % (tq // kb) == 0
    single_buffer = dict(pipeline_mode=pl.Buffered(1))
    row = lambda b, j: (b, j, 0)
    const2 = lambda b, j: (0, 0)
    whole = lambda b, j: (b, 0, 0, 0)
    in_specs = [
        pl.BlockSpec(memory_space=pltpu.SMEM),
        pl.BlockSpec((1, tq, d), row),
        pl.BlockSpec((1, tq, WIDTH), row),
        pl.BlockSpec((1, s // kb, kb, WIDTH), whole),
        pl.BlockSpec((1, s // kb, WIDTH, kb), whole, **single_buffer),
        pl.BlockSpec((N_META, WIDTH), const2),
        pl.BlockSpec((WIDTH, N_META), const2),
        pl.BlockSpec((1, tq, WIDTH), row),
        pl.BlockSpec((1, tq, WIDTH), row),
        pl.BlockSpec((1, V_DIM), const2),
        pl.BlockSpec((1, HEAD_DIM), const2),
        pl.BlockSpec((1, HEAD_DIM), const2),
        pl.BlockSpec((1, HEAD_DIM), const2),
        pl.BlockSpec((1, HEAD_DIM), const2),
        pl.BlockSpec(w_out_bf.shape, const2),
    ]
    return pl.pallas_call(
        functools.partial(_prompt_attn_kernel, tq=tq, kb=kb, big=big),
        grid=(nb, nq),
        in_specs=in_specs,
        out_specs=pl.BlockSpec((1, tq, d), row),
        out_shape=jax.ShapeDtypeStruct((nb, s, d), F32),
        scratch_shapes=[
            pltpu.VMEM((N_HEADS, 2 * tq, V_DIM), BF16),
            pltpu.VMEM((N_HEADS, 2 * tq), F32),
            pltpu.VMEM((N_HEADS, V7X_SUBLANES, 2 * tq), F32),
            pltpu.VMEM((N_HEADS, V_DIM, 2 * tq), F32),
        ],
        compiler_params=pltpu.CompilerParams(
            dimension_semantics=("arbitrary", "arbitrary"),
            vmem_limit_bytes=V7X_VMEM_LIMIT),
        name="prompt_attn",
    )(bound, x, q, kb4, vt4, km_bf, vmt_bf, sga, oc, subg, lq1, lk1, lq2, lk2, w_out_bf)


def _sample_attn_kernel(x_ref, q_ref, kn_ref, vn_ref, ck_ref, cv_ref, km_ref, vm_ref, sga_ref,
                        oc_ref, subg_ref, lq1_ref, lk1_ref, lq2_ref, lk2_ref, wout_ref, y_ref,
                        kcat, vcat, mix_ref, *, past, t_new, n_keys_pad, first):
    b = pl.program_id(0)
    nb = pl.num_programs(0)
    n_keys = past + N_META + t_new
    for h in range(N_HEADS):
        slab = slice(h * V_DIM, (h + 1) * V_DIM)
        kcat[0:past, slab] = ck_ref[0, pl.ds(h, past, stride=N_HEADS), :].astype(BF16)
        kcat[past + N_META:n_keys, slab] = kn_ref[0, pl.ds(h, t_new, stride=N_HEADS), :].astype(BF16)
        vcat[0:past, slab] = cv_ref[0, pl.ds(h, past, stride=N_HEADS), :].astype(BF16)
        vcat[past:past + N_META, slab] = vm_ref[pl.ds(h, N_META, stride=N_HEADS), :].astype(BF16)
        vcat[past + N_META:n_keys, slab] = vn_ref[0, pl.ds(h, t_new, stride=N_HEADS), :].astype(BF16)
    kcat[past:past + N_META, :] = km_ref[...]
    kcat[n_keys:n_keys_pad, :] = jnp.zeros((n_keys_pad - n_keys, WIDTH), BF16)
    vcat[n_keys:n_keys_pad, :] = jnp.zeros((n_keys_pad - n_keys, WIDTH), BF16)

    lane = lax.broadcasted_iota(jnp.int32, (1, V_DIM), 1)
    valid = lax.broadcasted_iota(jnp.int32, (1, n_keys_pad), 1) < n_keys
    lam = _lambda(lq1_ref, lk1_ref, lq2_ref, lk2_ref)
    o_heads = []
    for h in range(N_HEADS):
        qh = q_ref[0, :, h * V_DIM:(h + 1) * V_DIM]
        q2 = jnp.concatenate([jnp.where(lane < HEAD_DIM, qh, jnp.zeros_like(qh)),
                              jnp.where(lane >= HEAD_DIM, qh, jnp.zeros_like(qh))], axis=0)
        s = lax.dot_general(q2, kcat[:, h * V_DIM:(h + 1) * V_DIM], (((1,), (1,)), ((), ())),
                            preferred_element_type=F32)
        s = jnp.where(valid, s, NEG_BIG)
        e = jnp.exp2(s - jnp.max(s, axis=-1, keepdims=True))
        o2 = jnp.dot(e.astype(BF16), vcat[:, h * V_DIM:(h + 1) * V_DIM],
                     preferred_element_type=F32) / jnp.sum(e, axis=-1, keepdims=True)
        o_heads.append(o2[0:t_new] - lam * o2[t_new:2 * t_new])
    rows = pl.ds(pl.multiple_of(b * t_new, t_new), t_new)
    mix_ref[rows, :] = _gated_attn(o_heads, sga_ref[b + first], subg_ref[...])

    @pl.when(b == nb - 1)
    def _():
        n_rows = mix_ref.shape[0]
        oc = oc_ref[first:first + n_rows // t_new]
        base = _conv_branch_out(x_ref[...].reshape(n_rows, x_ref.shape[-1]),
                                oc.reshape(n_rows, WIDTH), wout_ref)
        y_ref[...] = _attn_branch_out(base, mix_ref[...], wout_ref).reshape(y_ref.shape)


def _sample_attn(x, q, kn, vn, ck, cv, km_bf, vm, sga, oc, subg, lq1, lk1, lq2, lk2, w_out_bf, *,
                 first):
    nb, t_new, d = x.shape
    n_proj = q.shape[0]
    past = ck.shape[1] // N_HEADS
    n_keys_pad = -(-(past + N_META + t_new) // V_DIM) * V_DIM
    row = lambda b: (b, 0, 0)
    proj_row = lambda b: (b + first, 0, 0)
    const2 = lambda b: (0, 0)
    const3 = lambda b: (0, 0, 0)
    in_specs = [
        pl.BlockSpec((nb, t_new, d), const3),
        pl.BlockSpec((1, t_new, WIDTH), proj_row),
        pl.BlockSpec((1, t_new * N_HEADS, V_DIM), proj_row),
        pl.BlockSpec((1, t_new * N_HEADS, V_DIM), proj_row),
        pl.BlockSpec((1, past * N_HEADS, V_DIM), row),
        pl.BlockSpec((1, past * N_HEADS, V_DIM), row),
        pl.BlockSpec((N_META, WIDTH), const2),
        pl.BlockSpec((N_META * N_HEADS, V_DIM), const2),
        pl.BlockSpec((n_proj, t_new, WIDTH), const3),
        pl.BlockSpec((n_proj, t_new, WIDTH), const3),
        pl.BlockSpec((1, V_DIM), const2),
        pl.BlockSpec((1, HEAD_DIM), const2),
        pl.BlockSpec((1, HEAD_DIM), const2),
        pl.BlockSpec((1, HEAD_DIM), const2),
        pl.BlockSpec((1, HEAD_DIM), const2),
        pl.BlockSpec(w_out_bf.shape, const2),
    ]
    return pl.pallas_call(
        functools.partial(_sample_attn_kernel, past=past, t_new=t_new, n_keys_pad=n_keys_pad,
                          first=first),
        grid=(nb,),
        in_specs=in_specs,
        out_specs=pl.BlockSpec((nb, t_new, d), const3),
        out_shape=jax.ShapeDtypeStruct((nb, t_new, d), F32),
        scratch_shapes=[pltpu.VMEM((n_keys_pad, WIDTH), BF16),
                        pltpu.VMEM((n_keys_pad, WIDTH), BF16),
                        pltpu.VMEM((nb * t_new, WIDTH), BF16)],
        compiler_params=pltpu.CompilerParams(
            dimension_semantics=("arbitrary",),
            vmem_limit_bytes=V7X_VMEM_LIMIT),
        name="sample_attn",
    )(x, q, kn, vn, ck, cv, km_bf, vm, sga, oc, subg, lq1, lk1, lq2, lk2, w_out_bf)


def _rope_tables(pos):
    half = ROT_DIM // 2
    inv = ROPE_THETA ** (-jnp.arange(0, ROT_DIM, 2, dtype=F32) / ROT_DIM)
    ang = pos.astype(F32)[:, None] * inv[None, :]
    cos, sin = lax.optimization_barrier((jnp.cos(ang), jnp.sin(ang)))
    t = pos.shape[0]
    rest = HEAD_DIM - 2 * half
    c64 = jnp.concatenate([cos, cos, jnp.ones((t, rest), F32)], axis=1)
    s64 = jnp.concatenate([-sin, sin, jnp.zeros((t, rest), F32)], axis=1)
    return jnp.tile(c64, (1, 2)), jnp.tile(s64, (1, 2))


def _score_bound(qg, kg):
    bound = Q_SCALE * HEAD_DIM * jnp.max(jnp.abs(qg)) * jnp.max(jnp.abs(kg)) * (1.0 + 2.0 ** -6)
    return bound.reshape(1).astype(F32)


def _pad_state(state):
    return jnp.pad(state, ((0, 0), (STATE_ROWS - (CONV_W - 1), 0), (0, 0)))


def _tile_sizes(seq):
    kb = KEY_BLOCK if seq % KEY_BLOCK == 0 else seq
    tq = QUERY_TILE if seq % QUERY_TILE == 0 else kb
    return tq, tq, kb


def kernel(x_prompt, x_sample, cache_k, cache_v, state_conv, meta_tokens, ln_g, w_in, q_norm_g,
           k_norm_g, lam_q1, lam_k1, lam_q2, lam_k2, subln_g, conv_w, conv_b, w_out):
    assert ln_g.shape[0] == 1, "single-layer stack"
    nb, seq, d = x_prompt.shape
    db, t_new, _ = x_sample.shape
    past = cache_k.shape[2]

    w_in_bf = w_in[0].astype(BF16)
    w_out_bf = w_out[0].astype(BF16)
    lng = ln_g[0][None, :]
    qg = jnp.tile(q_norm_g[0], 2 * N_HEADS)[None, :]
    kg = jnp.tile(k_norm_g[0], 2 * N_HEADS)[None, :]
    subg = subln_g[0][None, :]
    cw, cb = conv_w[0], conv_b[0][None, :]
    lams = [p[0][None, :] for p in (lam_q1, lam_k1, lam_q2, lam_k2)]
    grp = jnp.arange(GROUP_SUM_WIDTH, dtype=jnp.int32) // HEAD_DIM
    gmat = (grp[:, None] == grp[None, :]).astype(BF16)

    pos_m = jnp.arange(N_META, dtype=jnp.int32)
    pos_p = N_META + jnp.arange(seq, dtype=jnp.int32)
    pos_s = N_META + past + jnp.arange(t_new, dtype=jnp.int32)
    weights = (lng, w_in_bf, qg, kg, cw, cb, gmat)
    tm, tq, kb = _tile_sizes(seq)

    assert t_new == N_META, "meta tokens and new frames are projected as equal-length streams"
    cos_m, sin_m = _rope_tables(pos_m)
    cos_s, sin_s = _rope_tables(pos_s)
    n_small = 1 + db
    q_sm, k_sm, v_sm, kb_sm, sga_sm, oc_sm, tail_sm = _proj_call(
        jnp.concatenate([meta_tokens[None], x_sample], axis=0),
        jnp.concatenate([jnp.zeros((1, STATE_ROWS, WIDTH), F32), _pad_state(state_conv[0])], axis=0),
        jnp.concatenate([cos_m] + [cos_s] * db, axis=0),
        jnp.concatenate([sin_m] + [sin_s] * db, axis=0),
        *weights, sb=n_small, tm=t_new, tkv=t_new, emit_vt=False)
    km, vm, km_bf, tail_m = k_sm[0], v_sm[0], kb_sm[0], tail_sm[0:1]
    ks, vs, tail_s = k_sm[1:], v_sm[1:], tail_sm[1:]

    state_p = jnp.broadcast_to(tail_m, (nb, STATE_ROWS, WIDTH))
    qp, kp, vp, kp_bf, sga_p, oc_p, tail_p, vtp = _proj_call(
        x_prompt, state_p, *_rope_tables(pos_p), *weights, sb=1, tm=tm, tkv=kb, emit_vt=True,
        row_offset=N_META)
    y_prompt = _prompt_attn(
        _score_bound(q_norm_g[0], k_norm_g[0]), x_prompt, qp,
        kp_bf.reshape(nb, seq // kb, kb, WIDTH), vtp, km_bf,
        vm.reshape(N_META, WIDTH).T.astype(BF16),
        sga_p, oc_p, subg, *lams, w_out_bf, tq=tq, kb=kb, big=BIG_KEY_BLOCKS)
    kp = kp.at[:, :N_META * N_HEADS].set(jnp.broadcast_to(km[None], (nb,) + km.shape))
    vp = vp.at[:, :N_META * N_HEADS].set(jnp.broadcast_to(vm[None], (nb,) + vm.shape))

    y_sample = _sample_attn(
        x_sample, q_sm, k_sm, v_sm, cache_k[0].reshape(db, past * N_HEADS, V_DIM),
        cache_v[0].reshape(db, past * N_HEADS, V_DIM), km_bf, vm, sga_sm, oc_sm, subg, *lams,
        w_out_bf, first=1)

    def heads_form(a, rows):
        return a.reshape(1, a.shape[0], rows, N_HEADS, V_DIM)

    tail_rows = slice(STATE_ROWS - (CONV_W - 1), STATE_ROWS)
    return (y_prompt, y_sample, heads_form(kp, N_META + seq), heads_form(vp, N_META + seq),
            tail_p[None, :, tail_rows], heads_form(ks, t_new), heads_form(vs, t_new),
            tail_s[None, :, tail_rows])
```

```python
import functools
import math

import jax
import jax.numpy as jnp
from jax import lax
from jax.experimental import pallas as pl
from jax.experimental.pallas import tpu as pltpu

F32 = jnp.float32
BF16 = jnp.bfloat16

CHUNK = 64
N_META = 16
N_HEADS = 4
HEAD_DIM = 64
V_DIM = 2 * HEAD_DIM
WIDTH = N_HEADS * V_DIM
CONV_W = 3
ROT_DIM = HEAD_DIM // 4
ROPE_THETA = 500000.0
EPS = 1e-6
LAMBDA_INIT = 0.8 - 0.6 * math.exp(-0.3 * 0)
Q_SCALE = HEAD_DIM ** -0.5 * math.log2(math.e)
NEG_BIG = -1e30
SAFE_SCORE_BOUND = 40.0
V7X_SUBLANES = 8
V7X_MXU_WIDTH = 256
V7X_VMEM_LIMIT = 60 * 1024 * 1024

KEY_BLOCK = V7X_MXU_WIDTH
QUERY_TILE = 2 * KEY_BLOCK
BIG_KEY_BLOCKS = 8
GROUP_SUM_WIDTH = V7X_MXU_WIDTH
STATE_ROWS = V7X_SUBLANES


def _silu(x):
    return x / (1.0 + jnp.exp(-x))


def _proj_kernel(x_ref, st_ref, cos_ref, sin_ref, lng_ref, win_ref, qg_ref, kg_ref, cw_ref, cb_ref,
                 gmat_ref, q_ref, kf_ref, vf_ref, kb_ref, sga_ref, oc_ref, tail_ref, *rest,
                 sb, tm, tkv, emit_vt):
    if emit_vt:
        vt_ref, ubuf = rest
    else:
        (ubuf,) = rest
    j = pl.program_id(1)
    rows = sb * tm

    @pl.when(j == 0)
    def _():
        ubuf[:, 0:STATE_ROWS, :] = st_ref[...]

    @pl.when(j > 0)
    def _():
        ubuf[:, 0:STATE_ROWS, :] = ubuf[:, tm:tm + STATE_ROWS, :]

    x = x_ref[...].reshape(rows, x_ref.shape[-1])
    xg = (x * lng_ref[...]).astype(BF16)
    inv_rms = lax.rsqrt(jnp.mean(x * x, axis=-1, keepdims=True) + EPS)

    def proj(i):
        return inv_rms * jnp.dot(xg, win_ref[:, i * WIDTH:(i + 1) * WIDTH],
                                 preferred_element_type=F32)

    def group_mean_sq(t):
        tt = (t * t).astype(BF16)
        halves = [jnp.dot(tt[:, i * GROUP_SUM_WIDTH:(i + 1) * GROUP_SUM_WIDTH], gmat_ref[...],
                          preferred_element_type=F32) for i in range(WIDTH // GROUP_SUM_WIDTH)]
        return jnp.concatenate(halves, axis=1) * (1.0 / HEAD_DIM)

    cos = cos_ref[...]
    sin = sin_ref[...]
    lane = lax.broadcasted_iota(jnp.int32, (1, V_DIM), 1) % HEAD_DIM
    take_lower = lane >= ROT_DIM // 2

    def norm_rope(t, mean_sq, g):
        tn = t * lax.rsqrt(mean_sq + EPS) * g
        outs = []
        for h in range(N_HEADS):
            th = tn[:, h * V_DIM:(h + 1) * V_DIM]
            lower = pltpu.roll(th, ROT_DIM // 2, axis=1)
            upper = pltpu.roll(th, V_DIM - ROT_DIM // 2, axis=1)
            outs.append(th * cos + jnp.where(take_lower, lower, upper) * sin)
        return jnp.concatenate(outs, axis=1)

    def per_stream(a):
        return a.reshape(sb, tm, a.shape[-1])

    pq = proj(0)
    pk = proj(1)
    msq = group_mean_sq(pq)
    v = proj(2)
    msk = group_mean_sq(pk)
    ga = proj(3)
    q_ref[...] = per_stream((norm_rope(pq, msq, qg_ref[...]) * Q_SCALE).astype(BF16))
    bb = proj(4)
    k = norm_rope(pk, msk, kg_ref[...])
    kb_ref[...] = per_stream(k.astype(BF16))
    cc = proj(5)
    for b in range(sb):
        for h in range(N_HEADS):
            src = (slice(b * tm, (b + 1) * tm), slice(h * V_DIM, (h + 1) * V_DIM))
            kf_ref[b, pl.ds(h, tm, stride=N_HEADS), :] = k[src]
            vf_ref[b, pl.ds(h, tm, stride=N_HEADS), :] = v[src]
    hh = proj(6)
    if emit_vt:
        for s in range(tm // tkv):
            vt_ref[0, s] = v[s * tkv:(s + 1) * tkv, :].T.astype(BF16)
    gc = proj(7)
    sga_ref[...] = per_stream(_silu(ga).astype(BF16))

    u = cc * hh
    ys = []
    for b in range(sb):
        ubuf[b, STATE_ROWS:STATE_ROWS + tm, :] = u[b * tm:(b + 1) * tm, :]
        y = cb_ref[...]
        for tap in range(CONV_W):
            off = STATE_ROWS - (CONV_W - 1) + tap
            y = y + cw_ref[tap:tap + 1, :] * ubuf[b, off:off + tm, :]
        ys.append(y)
    y = ys[0] if sb == 1 else jnp.concatenate(ys, axis=0)
    oc_ref[...] = per_stream((_silu(gc) * (bb * y)).astype(BF16))
    tail_ref[...] = ubuf[:, tm:tm + STATE_ROWS, :]


def _proj_call(x, state8, cos_t, sin_t, ln_g, w_in_bf, qg, kg, conv_w, conv_b, gmat, *, sb, tm, tkv,
               emit_vt, row_offset=0):
    nb, t, d = x.shape
    nt = t // tm
    assert t % tm == 0 and tm % tkv == 0 and nb % sb == 0 and not (emit_vt and sb > 1)
    assert cos_t.shape[0] == sb * t
    row = lambda b, j: (b, j, 0)
    const2 = lambda b, j: (0, 0)
    in_specs = [
        pl.BlockSpec((sb, tm, d), row),
        pl.BlockSpec((sb, STATE_ROWS, WIDTH), lambda b, j: (b, 0, 0)),
        pl.BlockSpec((sb * tm, V_DIM), lambda b, j: (j, 0)),
        pl.BlockSpec((sb * tm, V_DIM), lambda b, j: (j, 0)),
        pl.BlockSpec((1, d), const2),
        pl.BlockSpec(w_in_bf.shape, const2, pipeline_mode=pl.Buffered(1)),
        pl.BlockSpec((1, WIDTH), const2),
        pl.BlockSpec((1, WIDTH), const2),
        pl.BlockSpec((CONV_W, WIDTH), const2),
        pl.BlockSpec((1, WIDTH), const2),
        pl.BlockSpec((GROUP_SUM_WIDTH, GROUP_SUM_WIDTH), const2),
    ]
    tile = pl.BlockSpec((sb, tm, WIDTH), row)
    heads_tile = pl.BlockSpec(
        (pl.Element(sb), pl.Element(tm * N_HEADS), pl.Element(V_DIM)),
        lambda b, j: (b * sb, pl.multiple_of((row_offset + j * tm) * N_HEADS, V7X_SUBLANES), 0))
    out_shape = [
        jax.ShapeDtypeStruct((nb, t, WIDTH), BF16),
        jax.ShapeDtypeStruct((nb, (row_offset + t) * N_HEADS, V_DIM), F32),
        jax.ShapeDtypeStruct((nb, (row_offset + t) * N_HEADS, V_DIM), F32),
        jax.ShapeDtypeStruct((nb, t, WIDTH), BF16),
        jax.ShapeDtypeStruct((nb, t, WIDTH), BF16),
        jax.ShapeDtypeStruct((nb, t, WIDTH), BF16),
        jax.ShapeDtypeStruct((nb, STATE_ROWS, WIDTH), F32),
    ]
    out_specs = [tile, heads_tile, heads_tile, tile, tile, tile,
                 pl.BlockSpec((sb, STATE_ROWS, WIDTH), lambda b, j: (b, 0, 0))]
    if emit_vt:
        out_shape.append(jax.ShapeDtypeStruct((nb, t // tkv, WIDTH, tkv), BF16))
        out_specs.append(pl.BlockSpec((1, tm // tkv, WIDTH, tkv), lambda b, j: (b, j, 0, 0)))
    return pl.pallas_call(
        functools.partial(_proj_kernel, sb=sb, tm=tm, tkv=tkv, emit_vt=emit_vt),
        grid=(nb // sb, nt),
        in_specs=in_specs,
        out_specs=out_specs,
        out_shape=out_shape,
        scratch_shapes=[pltpu.VMEM((sb, tm + STATE_ROWS, WIDTH), F32)],
        compiler_params=pltpu.CompilerParams(
            dimension_semantics=("arbitrary", "arbitrary"),
            vmem_limit_bytes=V7X_VMEM_LIMIT),
        name="proj",
    )(x, state8, cos_t, sin_t, ln_g, w_in_bf, qg, kg, conv_w, conv_b, gmat)


def _lambda(lq1_ref, lk1_ref, lq2_ref, lk2_ref):
    a = jnp.sum(lq1_ref[...] * lk1_ref[...], axis=-1, keepdims=True)
    b = jnp.sum(lq2_ref[...] * lk2_ref[...], axis=-1, keepdims=True)
    return jnp.exp(a) - jnp.exp(b) + LAMBDA_INIT


def _conv_branch_out(x, oc, wout_ref):
    return x + jnp.dot(oc, wout_ref[WIDTH:, :], preferred_element_type=F32)


def _gated_attn(o_heads, sga, subg):
    parts = []
    for h, o in enumerate(o_heads):
        ms = jnp.mean(o * o, axis=-1, keepdims=True)
        on = o * lax.rsqrt(ms + EPS) * subg * (1.0 - LAMBDA_INIT)
        parts.append((sga[:, h * V_DIM:(h + 1) * V_DIM].astype(F32) * on).astype(BF16))
    return jnp.concatenate(parts, axis=1)


def _attn_branch_out(base, mix, wout_ref):
    return base + jnp.dot(mix, wout_ref[:WIDTH, :], preferred_element_type=F32)


def _prompt_attn_kernel(bound_ref, x_ref, q_ref, kb_ref, vt_ref, km_ref, vmt_ref, sga_ref, oc_ref,
                        subg_ref, lq1_ref, lk1_ref, lq2_ref, lk2_ref, wout_ref, y_ref,
                        qz_ref, m_ref, l_ref, acc_ref, *, tq, kb, big):
    j = pl.program_id(1)
    lanes = 2 * tq
    n_diag = tq // kb
    n_full = j * n_diag

    lane = lax.broadcasted_iota(jnp.int32, (1, V_DIM), 1)
    for h in range(N_HEADS):
        qh = q_ref[0, :, h * V_DIM:(h + 1) * V_DIM]
        qz_ref[h, 0:tq, :] = jnp.where(lane < HEAD_DIM, qh, jnp.zeros_like(qh))
        qz_ref[h, tq:lanes, :] = jnp.where(lane >= HEAD_DIM, qh, jnp.zeros_like(qh))

    def scores(k_h, h):
        return lax.dot_general(k_h, qz_ref[h], (((1,), (1,)), ((), ())), preferred_element_type=F32)

    def k_slab(k, h):
        return k[:, h * V_DIM:(h + 1) * V_DIM]

    def sublane_partial(e):
        return jnp.sum(e.reshape(e.shape[0] // V7X_SUBLANES, V7X_SUBLANES, lanes), axis=0)

    kchunk = lax.broadcasted_iota(jnp.int32, (kb, 1), 0) // CHUNK
    qchunk = (lax.broadcasted_iota(jnp.int32, (1, lanes), 1) % tq) // CHUNK
    diag_masks = [kchunk + d * (kb // CHUNK) <= qchunk for d in range(n_diag)]

    def late_queries(a, d, axis):
        lo = d * kb
        if axis == 0:
            return jnp.concatenate([a[lo:tq], a[tq + lo:lanes]], axis=0)
        return jnp.concatenate([a[:, lo:tq], a[:, tq + lo:lanes]], axis=1)

    def add_late(full, part, d):
        lo, w = d * kb, tq - d * kb
        return jnp.concatenate([full[:, 0:lo], full[:, lo:tq] + part[:, 0:w],
                                full[:, tq:tq + lo], full[:, tq + lo:lanes] + part[:, w:2 * w]],
                               axis=1)

    @pl.when(bound_ref[0] <= SAFE_SCORE_BOUND)
    def _():
        def key_pass(k, pieces, assign, late_blocks=()):
            s_next = scores(k_slab(k, 0), 0)
            for h in range(N_HEADS):
                s = s_next
                if h + 1 < N_HEADS:
                    s_next = scores(k_slab(k, h + 1), h + 1)
                s_late = [lax.dot_general(k_slab(kl, h), late_queries(qz_ref[h], d, 0),
                                          (((1,), (1,)), ((), ())), preferred_element_type=F32)
                          for d, kl, _ in late_blocks]
                e = jnp.exp2(s)
                l_new = acc_new = None
                for row0, rows, vt, mask in pieces:
                    ep = e[row0:row0 + rows, :]
                    if mask is not None:
                        ep = jnp.where(mask, ep, 0.0)
                    lp = sublane_partial(ep)
                    ap = jnp.dot(vt[h * V_DIM:(h + 1) * V_DIM, :], ep.astype(BF16),
                                 preferred_element_type=F32)
                    l_new = lp if l_new is None else l_new + lp
                    acc_new = ap if acc_new is None else acc_new + ap
                for (d, _, vt), sl in zip(late_blocks, s_late):
                    ep = jnp.where(late_queries(diag_masks[d], d, 1), jnp.exp2(sl), 0.0)
                    lp = jnp.sum(ep.reshape(kb // V7X_SUBLANES, V7X_SUBLANES, ep.shape[1]), axis=0)
                    ap = jnp.dot(vt[h * V_DIM:(h + 1) * V_DIM, :], ep.astype(BF16),
                                 preferred_element_type=F32)
                    l_new = add_late(l_new, lp, d)
                    acc_new = add_late(acc_new, ap, d)
                if assign:
                    l_ref[h] = l_new
                    acc_ref[h] = acc_new
                else:
                    l_ref[h] += l_new
                    acc_ref[h] += acc_new

        n_big = n_full // big
        rem = n_full - n_big * big
        for r in range(0, big, n_diag):
            @pl.when(rem == r)
            def _():
                first = n_full - r
                n_blocks = r + 1
                kblocks = kb_ref[0, pl.ds(first, n_blocks)].reshape(n_blocks * kb, WIDTH)
                k = jnp.concatenate([km_ref[...], kblocks], axis=0)
                pieces = [(0, N_META, vmt_ref[...], None)]
                pieces += [(N_META + t * kb, kb, vt_ref[0, first + t],
                            None if t < r else diag_masks[0]) for t in range(n_blocks)]
                late = [(d, kb_ref[0, n_full + d], vt_ref[0, n_full + d])
                        for d in range(1, n_diag)]
                key_pass(k, pieces, assign=True, late_blocks=late)

        def big_block(i, carry):
            first = i * big
            k = kb_ref[0, pl.ds(first, big)].reshape(big * kb, WIDTH)
            key_pass(k, [(t * kb, kb, vt_ref[0, first + t], None) for t in range(big)], assign=False)
            return carry

        lax.fori_loop(0, n_big, big_block, 0)

    @pl.when(jnp.logical_not(bound_ref[0] <= SAFE_SCORE_BOUND))
    def _():
        for h in range(N_HEADS):
            s = scores(k_slab(km_ref, h), h)
            m = jnp.max(s, axis=0, keepdims=True)
            e = jnp.exp2(s - m)
            m_ref[h:h + 1, :] = m
            l_ref[h] = jnp.zeros((V7X_SUBLANES, lanes), F32)
            l_ref[h, 0:1, :] = jnp.sum(e, axis=0, keepdims=True)
            acc_ref[h] = jnp.dot(vmt_ref[h * V_DIM:(h + 1) * V_DIM, :], e.astype(BF16),
                                 preferred_element_type=F32)

        def block(i, mask):
            k = kb_ref[0, i]
            vt = vt_ref[0, i]
            for h in range(N_HEADS):
                s = scores(k_slab(k, h), h)
                if mask is not None:
                    s = jnp.where(mask, s, NEG_BIG)
                m_old = m_ref[h:h + 1, :]
                m_new = jnp.maximum(m_old, jnp.max(s, axis=0, keepdims=True))
                alpha = jnp.exp2(m_old - m_new)
                e = jnp.exp2(s - m_new)
                l_ref[h, 0:1, :] = alpha * l_ref[h, 0:1, :] + jnp.sum(e, axis=0, keepdims=True)
                acc_ref[h] = alpha * acc_ref[h] + jnp.dot(
                    vt[h * V_DIM:(h + 1) * V_DIM, :], e.astype(BF16), preferred_element_type=F32)
                m_ref[h:h + 1, :] = m_new

        def full_block(i, carry):
            block(i, None)
            return carry

        lax.fori_loop(0, n_full, full_block, 0)
        for d in range(n_diag):
            block(n_full + d, diag_masks[d])

    base = _conv_branch_out(x_ref[0], oc_ref[0], wout_ref)
    lam = _lambda(lq1_ref, lk1_ref, lq2_ref, lk2_ref)
    o_heads = []
    for h in range(N_HEADS):
        o2 = acc_ref[h] / jnp.sum(l_ref[h], axis=0, keepdims=True)
        o_heads.append((o2[:, 0:tq] - lam * o2[:, tq:lanes]).T)
    y_ref[0] = _attn_branch_out(base, _gated_attn(o_heads, sga_ref[0], subg_ref[...]), wout_ref)


def _prompt_attn(bound, x, q, kb4, vt4, km_bf, vmt_bf, sga, oc, subg, lq1, lk1, lq2, lk2, w_out_bf,
                 *, tq, kb, big):
    nb, s, d = x.shape
    nq = s // tq
    assert tq % kb == 0 and big % (tq // kb) == 0
    single_buffer = dict(pipeline_mode=pl.Buffered(1))
    row = lambda b, j: (b, j, 0)
    const2 = lambda b, j: (0, 0)
    whole = lambda b, j: (b, 0, 0, 0)
    in_specs = [
        pl.BlockSpec(memory_space=pltpu.SMEM),
        pl.BlockSpec((1, tq, d), row),
        pl.BlockSpec((1, tq, WIDTH), row),
        pl.BlockSpec((1, s // kb, kb, WIDTH), whole),
        pl.BlockSpec((1, s // kb, WIDTH, kb), whole, **single_buffer),
        pl.BlockSpec((N_META, WIDTH), const2),
        pl.BlockSpec((WIDTH, N_META), const2),
        pl.BlockSpec((1, tq, WIDTH), row),
        pl.BlockSpec((1, tq, WIDTH), row),
        pl.BlockSpec((1, V_DIM), const2),
        pl.BlockSpec((1, HEAD_DIM), const2),
        pl.BlockSpec((1, HEAD_DIM), const2),
        pl.BlockSpec((1, HEAD_DIM), const2),
        pl.BlockSpec((1, HEAD_DIM), const2),
        pl.BlockSpec(w_out_bf.shape, const2),
    ]
    return pl.pallas_call(
        functools.partial(_prompt_attn_kernel, tq=tq, kb=kb, big=big),
        grid=(nb, nq),
        in_specs=in_specs,
        out_specs=pl.BlockSpec((1, tq, d), row),
        out_shape=jax.ShapeDtypeStruct((nb, s, d), F32),
        scratch_shapes=[
            pltpu.VMEM((N_HEADS, 2 * tq, V_DIM), BF16),
            pltpu.VMEM((N_HEADS, 2 * tq), F32),
            pltpu.VMEM((N_HEADS, V7X_SUBLANES, 2 * tq), F32),
            pltpu.VMEM((N_HEADS, V_DIM, 2 * tq), F32),
        ],
        compiler_params=pltpu.CompilerParams(
            dimension_semantics=("arbitrary", "arbitrary"),
            vmem_limit_bytes=V7X_VMEM_LIMIT),
        name="prompt_attn",
    )(bound, x, q, kb4, vt4, km_bf, vmt_bf, sga, oc, subg, lq1, lk1, lq2, lk2, w_out_bf)


def _sample_attn_kernel(x_ref, q_ref, kn_ref, vn_ref, ck_ref, cv_ref, km_ref, vm_ref, sga_ref,
                        oc_ref, subg_ref, lq1_ref, lk1_ref, lq2_ref, lk2_ref, wout_ref, y_ref,
                        kcat, vcat, mix_ref, *, past, t_new, n_keys_pad, first):
    b = pl.program_id(0)
    nb = pl.num_programs(0)
    n_keys = past + N_META + t_new
    for h in range(N_HEADS):
        slab = slice(h * V_DIM, (h + 1) * V_DIM)
        kcat[0:past, slab] = ck_ref[0, pl.ds(h, past, stride=N_HEADS), :].astype(BF16)
        kcat[past + N_META:n_keys, slab] = kn_ref[0, pl.ds(h, t_new, stride=N_HEADS), :].astype(BF16)
        vcat[0:past, slab] = cv_ref[0, pl.ds(h, past, stride=N_HEADS), :].astype(BF16)
        vcat[past:past + N_META, slab] = vm_ref[pl.ds(h, N_META, stride=N_HEADS), :].astype(BF16)
        vcat[past + N_META:n_keys, slab] = vn_ref[0, pl.ds(h, t_new, stride=N_HEADS), :].astype(BF16)
    kcat[past:past + N_META, :] = km_ref[...]
    kcat[n_keys:n_keys_pad, :] = jnp.zeros((n_keys_pad - n_keys, WIDTH), BF16)
    vcat[n_keys:n_keys_pad, :] = jnp.zeros((n_keys_pad - n_keys, WIDTH), BF16)

    lane = lax.broadcasted_iota(jnp.int32, (1, V_DIM), 1)
    valid = lax.broadcasted_iota(jnp.int32, (1, n_keys_pad), 1) < n_keys
    lam = _lambda(lq1_ref, lk1_ref, lq2_ref, lk2_ref)
    o_heads = []
    for h in range(N_HEADS):
        qh = q_ref[0, :, h * V_DIM:(h + 1) * V_DIM]
        q2 = jnp.concatenate([jnp.where(lane < HEAD_DIM, qh, jnp.zeros_like(qh)),
                              jnp.where(lane >= HEAD_DIM, qh, jnp.zeros_like(qh))], axis=0)
        s = lax.dot_general(q2, kcat[:, h * V_DIM:(h + 1) * V_DIM], (((1,), (1,)), ((), ())),
                            preferred_element_type=F32)
        s = jnp.where(valid, s, NEG_BIG)
        e = jnp.exp2(s - jnp.max(s, axis=-1, keepdims=True))
        o2 = jnp.dot(e.astype(BF16), vcat[:, h * V_DIM:(h + 1) * V_DIM],
                     preferred_element_type=F32) / jnp.sum(e, axis=-1, keepdims=True)
        o_heads.append(o2[0:t_new] - lam * o2[t_new:2 * t_new])
    rows = pl.ds(pl.multiple_of(b * t_new, t_new), t_new)
    mix_ref[rows, :] = _gated_attn(o_heads, sga_ref[b + first], subg_ref[...])

    @pl.when(b == nb - 1)
    def _():
        n_rows = mix_ref.shape[0]
        oc = oc_ref[first:first + n_rows // t_new]
        base = _conv_branch_out(x_ref[...].reshape(n_rows, x_ref.shape[-1]),
                                oc.reshape(n_rows, WIDTH), wout_ref)
        y_ref[...] = _attn_branch_out(base, mix_ref[...], wout_ref).reshape(y_ref.shape)


def _sample_attn(x, q, kn, vn, ck, cv, km_bf, vm, sga, oc, subg, lq1, lk1, lq2, lk2, w_out_bf, *,
                 first):
    nb, t_new, d = x.shape
    n_proj = q.shape[0]
    past = ck.shape[1] // N_HEADS
    n_keys_pad = -(-(past + N_META + t_new) // V_DIM) * V_DIM
    row = lambda b: (b, 0, 0)
    proj_row = lambda b: (b + first, 0, 0)
    const2 = lambda b: (0, 0)
    const3 = lambda b: (0, 0, 0)
    in_specs = [
        pl.BlockSpec((nb, t_new, d), const3),
        pl.BlockSpec((1, t_new, WIDTH), proj_row),
        pl.BlockSpec((1, t_new * N_HEADS, V_DIM), proj_row),
        pl.BlockSpec((1, t_new * N_HEADS, V_DIM), proj_row),
        pl.BlockSpec((1, past * N_HEADS, V_DIM), row),
        pl.BlockSpec((1, past * N_HEADS, V_DIM), row),
        pl.BlockSpec((N_META, WIDTH), const2),
        pl.BlockSpec((N_META * N_HEADS, V_DIM), const2),
        pl.BlockSpec((n_proj, t_new, WIDTH), const3),
        pl.BlockSpec((n_proj, t_new, WIDTH), const3),
        pl.BlockSpec((1, V_DIM), const2),
        pl.BlockSpec((1, HEAD_DIM), const2),
        pl.BlockSpec((1, HEAD_DIM), const2),
        pl.BlockSpec((1, HEAD_DIM), const2),
        pl.BlockSpec((1, HEAD_DIM), const2),
        pl.BlockSpec(w_out_bf.shape, const2),
    ]
    return pl.pallas_call(
        functools.partial(_sample_attn_kernel, past=past, t_new=t_new, n_keys_pad=n_keys_pad,
                          first=first),
        grid=(nb,),
        in_specs=in_specs,
        out_specs=pl.BlockSpec((nb, t_new, d), const3),
        out_shape=jax.ShapeDtypeStruct((nb, t_new, d), F32),
        scratch_shapes=[pltpu.VMEM((n_keys_pad, WIDTH), BF16),
                        pltpu.VMEM((n_keys_pad, WIDTH), BF16),
                        pltpu.VMEM((nb * t_new, WIDTH), BF16)],
        compiler_params=pltpu.CompilerParams(
            dimension_semantics=("arbitrary",),
            vmem_limit_bytes=V7X_VMEM_LIMIT),
        name="sample_attn",
    )(x, q, kn, vn, ck, cv, km_bf, vm, sga, oc, subg, lq1, lk1, lq2, lk2, w_out_bf)


def _rope_tables(pos):
    half = ROT_DIM // 2
    inv = ROPE_THETA ** (-jnp.arange(0, ROT_DIM, 2, dtype=F32) / ROT_DIM)
    ang = pos.astype(F32)[:, None] * inv[None, :]
    cos, sin = lax.optimization_barrier((jnp.cos(ang), jnp.sin(ang)))
    t = pos.shape[0]
    rest = HEAD_DIM - 2 * half
    c64 = jnp.concatenate([cos, cos, jnp.ones((t, rest), F32)], axis=1)
    s64 = jnp.concatenate([-sin, sin, jnp.zeros((t, rest), F32)], axis=1)
    return jnp.tile(c64, (1, 2)), jnp.tile(s64, (1, 2))


def _score_bound(qg, kg):
    bound = Q_SCALE * HEAD_DIM * jnp.max(jnp.abs(qg)) * jnp.max(jnp.abs(kg)) * (1.0 + 2.0 ** -6)
    return bound.reshape(1).astype(F32)


def _pad_state(state):
    return jnp.pad(state, ((0, 0), (STATE_ROWS - (CONV_W - 1), 0), (0, 0)))


def _tile_sizes(seq):
    kb = KEY_BLOCK if seq % KEY_BLOCK == 0 else seq
    tq = QUERY_TILE if seq % QUERY_TILE == 0 else kb
    return (2 * tq if seq % (2 * tq) == 0 else tq), tq, kb


def kernel(x_prompt, x_sample, cache_k, cache_v, state_conv, meta_tokens, ln_g, w_in, q_norm_g,
           k_norm_g, lam_q1, lam_k1, lam_q2, lam_k2, subln_g, conv_w, conv_b, w_out):
    assert ln_g.shape[0] == 1, "single-layer stack"
    nb, seq, d = x_prompt.shape
    db, t_new, _ = x_sample.shape
    past = cache_k.shape[2]

    w_in_bf = w_in[0].astype(BF16)
    w_out_bf = w_out[0].astype(BF16)
    lng = ln_g[0][None, :]
    qg = jnp.tile(q_norm_g[0], 2 * N_HEADS)[None, :]
    kg = jnp.tile(k_norm_g[0], 2 * N_HEADS)[None, :]
    subg = subln_g[0][None, :]
    cw, cb = conv_w[0], conv_b[0][None, :]
    lams = [p[0][None, :] for p in (lam_q1, lam_k1, lam_q2, lam_k2)]
    grp = jnp.arange(GROUP_SUM_WIDTH, dtype=jnp.int32) // HEAD_DIM
    gmat = (grp[:, None] == grp[None, :]).astype(BF16)

    pos_m = jnp.arange(N_META, dtype=jnp.int32)
    pos_p = N_META + jnp.arange(seq, dtype=jnp.int32)
    pos_s = N_META + past + jnp.arange(t_new, dtype=jnp.int32)
    weights = (lng, w_in_bf, qg, kg, cw, cb, gmat)
    tm, tq, kb = _tile_sizes(seq)

    assert t_new == N_META, "meta tokens and new frames are projected as equal-length streams"
    cos_m, sin_m = _rope_tables(pos_m)
    cos_s, sin_s = _rope_tables(pos_s)
    n_small = 1 + db
    q_sm, k_sm, v_sm, kb_sm, sga_sm, oc_sm, tail_sm = _proj_call(
        jnp.concatenate([meta_tokens[None], x_sample], axis=0),
        jnp.concatenate([jnp.zeros((1, STATE_ROWS, WIDTH), F32), _pad_state(state_conv[0])], axis=0),
        jnp.concatenate([cos_m] + [cos_s] * db, axis=0),
        jnp.concatenate([sin_m] + [sin_s] * db, axis=0),
        *weights, sb=n_small, tm=t_new, tkv=t_new, emit_vt=False)
    km, vm, km_bf, tail_m = k_sm[0], v_sm[0], kb_sm[0], tail_sm[0:1]
    ks, vs, tail_s = k_sm[1:], v_sm[1:], tail_sm[1:]

    state_p = jnp.broadcast_to(tail_m, (nb, STATE_ROWS, WIDTH))
    qp, kp, vp, kp_bf, sga_p, oc_p, tail_p, vtp = _proj_call(
        x_prompt, state_p, *_rope_tables(pos_p), *weights, sb=1, tm=tm, tkv=kb, emit_vt=True,
        row_offset=N_META)
    y_prompt = _prompt_attn(
        _score_bound(q_norm_g[0], k_norm_g[0]), x_prompt, qp,
        kp_bf.reshape(nb, seq // kb, kb, WIDTH), vtp, km_bf,
        vm.reshape(N_META, WIDTH).T.astype(BF16),
        sga_p, oc_p, subg, *lams, w_out_bf, tq=tq, kb=kb, big=BIG_KEY_BLOCKS)
    kp = kp.at[:, :N_META * N_HEADS].set(jnp.broadcast_to(km[None], (nb,) + km.shape))
    vp = vp.at[:, :N_META * N_HEADS].set(jnp.broadcast_to(vm[None], (nb,) + vm.shape))

    y_sample = _sample_attn(
        x_sample, q_sm, k_sm, v_sm, cache_k[0].reshape(db, past * N_HEADS, V_DIM),
        cache_v[0].reshape(db, past * N_HEADS, V_DIM), km_bf, vm, sga_sm, oc_sm, subg, *lams,
        w_out_bf, first=1)

    def heads_form(a, rows):
        return a.reshape(1, a.shape[0], rows, N_HEADS, V_DIM)

    tail_rows = slice(STATE_ROWS - (CONV_W - 1), STATE_ROWS)
    return (y_prompt, y_sample, heads_form(kp, N_META + seq), heads_form(vp, N_META + seq),
            tail_p[None, :, tail_rows], heads_form(ks, t_new), heads_form(vs, t_new),
            tail_s[None, :, tail_rows])
```

```python
import functools
import math

import jax
import jax.numpy as jnp
from jax import lax
from jax.experimental import pallas as pl
from jax.experimental.pallas import tpu as pltpu

F32 = jnp.float32
BF16 = jnp.bfloat16

CHUNK = 64
N_META = 16
N_HEADS = 4
HEAD_DIM = 64
V_DIM = 2 * HEAD_DIM
WIDTH = N_HEADS * V_DIM
CONV_W = 3
ROT_DIM = HEAD_DIM // 4
ROPE_THETA = 500000.0
EPS = 1e-6
LAMBDA_INIT = 0.8 - 0.6 * math.exp(-0.3 * 0)
Q_SCALE = HEAD_DIM ** -0.5 * math.log2(math.e)
NEG_BIG = -1e30
SAFE_SCORE_BOUND = 40.0
V7X_SUBLANES = 8
V7X_MXU_WIDTH = 256
V7X_VMEM_LIMIT = 60 * 1024 * 1024

KEY_BLOCK = V7X_MXU_WIDTH
QUERY_TILE = 2 * KEY_BLOCK
BIG_KEY_BLOCKS = 8
GROUP_SUM_WIDTH = V7X_MXU_WIDTH
STATE_ROWS = V7X_SUBLANES


def _silu(x):
    return x / (1.0 + jnp.exp(-x))


def _proj_kernel(x_ref, st_ref, cos_ref, sin_ref, lng_ref, win_ref, qg_ref, kg_ref, cw_ref, cb_ref,
                 gmat_ref, q_ref, kf_ref, vf_ref, kb_ref, sga_ref, oc_ref, tail_ref, *rest,
                 sb, tm, tkv, emit_vt):
    if emit_vt:
        vt_ref, ubuf = rest
    else:
        (ubuf,) = rest
    j = pl.program_id(1)
    rows = sb * tm

    @pl.when(j == 0)
    def _():
        ubuf[:, 0:STATE_ROWS, :] = st_ref[...]

    @pl.when(j > 0)
    def _():
        ubuf[:, 0:STATE_ROWS, :] = ubuf[:, tm:tm + STATE_ROWS, :]

    x = x_ref[...].reshape(rows, x_ref.shape[-1])
    xg = (x * lng_ref[...]).astype(BF16)
    inv_rms = lax.rsqrt(jnp.mean(x * x, axis=-1, keepdims=True) + EPS)

    def proj(i):
        return inv_rms * jnp.dot(xg, win_ref[:, i * WIDTH:(i + 1) * WIDTH],
                                 preferred_element_type=F32)

    def group_mean_sq(t):
        tt = (t * t).astype(BF16)
        halves = [jnp.dot(tt[:, i * GROUP_SUM_WIDTH:(i + 1) * GROUP_SUM_WIDTH], gmat_ref[...],
                          preferred_element_type=F32) for i in range(WIDTH // GROUP_SUM_WIDTH)]
        return jnp.concatenate(halves, axis=1) * (1.0 / HEAD_DIM)

    cos = cos_ref[...]
    sin = sin_ref[...]
    lane = lax.broadcasted_iota(jnp.int32, (1, V_DIM), 1) % HEAD_DIM
    take_lower = lane >= ROT_DIM // 2

    def norm_rope(t, mean_sq, g):
        tn = t * lax.rsqrt(mean_sq + EPS) * g
        outs = []
        for h in range(N_HEADS):
            th = tn[:, h * V_DIM:(h + 1) * V_DIM]
            lower = pltpu.roll(th, ROT_DIM // 2, axis=1)
            upper = pltpu.roll(th, V_DIM - ROT_DIM // 2, axis=1)
            outs.append(th * cos + jnp.where(take_lower, lower, upper) * sin)
        return jnp.concatenate(outs, axis=1)

    def per_stream(a):
        return a.reshape(sb, tm, a.shape[-1])

    pq = proj(0)
    pk = proj(1)
    msq = group_mean_sq(pq)
    v = proj(2)
    msk = group_mean_sq(pk)
    ga = proj(3)
    q_ref[...] = per_stream((norm_rope(pq, msq, qg_ref[...]) * Q_SCALE).astype(BF16))
    bb = proj(4)
    k = norm_rope(pk, msk, kg_ref[...])
    kb_ref[...] = per_stream(k.astype(BF16))
    cc = proj(5)
    for b in range(sb):
        for h in range(N_HEADS):
            src = (slice(b * tm, (b + 1) * tm), slice(h * V_DIM, (h + 1) * V_DIM))
            kf_ref[b, pl.ds(h, tm, stride=N_HEADS), :] = k[src]
            vf_ref[b, pl.ds(h, tm, stride=N_HEADS), :] = v[src]
    hh = proj(6)
    if emit_vt:
        for s in range(tm // tkv):
            vt_ref[0, s] = v[s * tkv:(s + 1) * tkv, :].T.astype(BF16)
    gc = proj(7)
    sga_ref[...] = per_stream(_silu(ga).astype(BF16))

    u = cc * hh
    ys = []
    for b in range(sb):
        ubuf[b, STATE_ROWS:STATE_ROWS + tm, :] = u[b * tm:(b + 1) * tm, :]
        y = cb_ref[...]
        for tap in range(CONV_W):
            off = STATE_ROWS - (CONV_W - 1) + tap
            y = y + cw_ref[tap:tap + 1, :] * ubuf[b, off:off + tm, :]
        ys.append(y)
    y = ys[0] if sb == 1 else jnp.concatenate(ys, axis=0)
    oc_ref[...] = per_stream((_silu(gc) * (bb * y)).astype(BF16))
    tail_ref[...] = ubuf[:, tm:tm + STATE_ROWS, :]


def _proj_call(x, state8, cos_t, sin_t, ln_g, w_in_bf, qg, kg, conv_w, conv_b, gmat, *, sb, tm, tkv,
               emit_vt, row_offset=0):
    nb, t, d = x.shape
    nt = t // tm
    assert t % tm == 0 and tm % tkv == 0 and nb % sb == 0 and not (emit_vt and sb > 1)
    assert cos_t.shape[0] == sb * t
    row = lambda b, j: (b, j, 0)
    const2 = lambda b, j: (0, 0)
    in_specs = [
        pl.BlockSpec((sb, tm, d), row),
        pl.BlockSpec((sb, STATE_ROWS, WIDTH), lambda b, j: (b, 0, 0)),
        pl.BlockSpec((sb * tm, V_DIM), lambda b, j: (j, 0)),
        pl.BlockSpec((sb * tm, V_DIM), lambda b, j: (j, 0)),
        pl.BlockSpec((1, d), const2),
        pl.BlockSpec(w_in_bf.shape, const2, pipeline_mode=pl.Buffered(1)),
        pl.BlockSpec((1, WIDTH), const2),
        pl.BlockSpec((1, WIDTH), const2),
        pl.BlockSpec((CONV_W, WIDTH), const2),
        pl.BlockSpec((1, WIDTH), const2),
        pl.BlockSpec((GROUP_SUM_WIDTH, GROUP_SUM_WIDTH), const2),
    ]
    tile = pl.BlockSpec((sb, tm, WIDTH), row)
    heads_tile = pl.BlockSpec(
        (pl.Element(sb), pl.Element(tm * N_HEADS), pl.Element(V_DIM)),
        lambda b, j: (b * sb, pl.multiple_of((row_offset + j * tm) * N_HEADS, V7X_SUBLANES), 0))
    out_shape = [
        jax.ShapeDtypeStruct((nb, t, WIDTH), BF16),
        jax.ShapeDtypeStruct((nb, (row_offset + t) * N_HEADS, V_DIM), F32),
        jax.ShapeDtypeStruct((nb, (row_offset + t) * N_HEADS, V_DIM), F32),
        jax.ShapeDtypeStruct((nb, t, WIDTH), BF16),
        jax.ShapeDtypeStruct((nb, t, WIDTH), BF16),
        jax.ShapeDtypeStruct((nb, t, WIDTH), BF16),
        jax.ShapeDtypeStruct((nb, STATE_ROWS, WIDTH), F32),
    ]
    out_specs = [tile, heads_tile, heads_tile, tile, tile, tile,
                 pl.BlockSpec((sb, STATE_ROWS, WIDTH), lambda b, j: (b, 0, 0))]
    if emit_vt:
        out_shape.append(jax.ShapeDtypeStruct((nb, t // tkv, WIDTH, tkv), BF16))
        out_specs.append(pl.BlockSpec((1, tm // tkv, WIDTH, tkv), lambda b, j: (b, j, 0, 0)))
    return pl.pallas_call(
        functools.partial(_proj_kernel, sb=sb, tm=tm, tkv=tkv, emit_vt=emit_vt),
        grid=(nb // sb, nt),
        in_specs=in_specs,
        out_specs=out_specs,
        out_shape=out_shape,
        scratch_shapes=[pltpu.VMEM((sb, tm + STATE_ROWS, WIDTH), F32)],
        compiler_params=pltpu.CompilerParams(
            dimension_semantics=("arbitrary", "arbitrary"),
            vmem_limit_bytes=V7X_VMEM_LIMIT),
        name="proj",
    )(x, state8, cos_t, sin_t, ln_g, w_in_bf, qg, kg, conv_w, conv_b, gmat)


def _lambda(lq1_ref, lk1_ref, lq2_ref, lk2_ref):
    a = jnp.sum(lq1_ref[...] * lk1_ref[...], axis=-1, keepdims=True)
    b = jnp.sum(lq2_ref[...] * lk2_ref[...], axis=-1, keepdims=True)
    return jnp.exp(a) - jnp.exp(b) + LAMBDA_INIT


def _conv_branch_out(x, oc, wout_ref):
    return x + jnp.dot(oc, wout_ref[WIDTH:, :], preferred_element_type=F32)


def _gated_attn(o_heads, sga, subg):
    parts = []
    for h, o in enumerate(o_heads):
        ms = jnp.mean(o * o, axis=-1, keepdims=True)
        on = o * lax.rsqrt(ms + EPS) * subg * (1.0 - LAMBDA_INIT)
        parts.append((sga[:, h * V_DIM:(h + 1) * V_DIM].astype(F32) * on).astype(BF16))
    return jnp.concatenate(parts, axis=1)


def _attn_branch_out(base, mix, wout_ref):
    return base + jnp.dot(mix, wout_ref[:WIDTH, :], preferred_element_type=F32)


def _prompt_attn_kernel(bound_ref, x_ref, q_ref, kb_ref, vt_ref, km_ref, vmt_ref, sga_ref, oc_ref,
                        subg_ref, lq1_ref, lk1_ref, lq2_ref, lk2_ref, wout_ref, y_ref,
                        qz_ref, m_ref, l_ref, acc_ref, *, tq, kb, big):
    j = pl.program_id(1)
    lanes = 2 * tq
    n_diag = tq // kb
    n_full = j * n_diag

    lane = lax.broadcasted_iota(jnp.int32, (1, V_DIM), 1)
    for h in range(N_HEADS):
        qh = q_ref[0, :, h * V_DIM:(h + 1) * V_DIM]
        qz_ref[h, 0:tq, :] = jnp.where(lane < HEAD_DIM, qh, jnp.zeros_like(qh))
        qz_ref[h, tq:lanes, :] = jnp.where(lane >= HEAD_DIM, qh, jnp.zeros_like(qh))

    def scores(k_h, h):
        return lax.dot_general(k_h, qz_ref[h], (((1,), (1,)), ((), ())), preferred_element_type=F32)

    def k_slab(k, h):
        return k[:, h * V_DIM:(h + 1) * V_DIM]

    def sublane_partial(e):
        return jnp.sum(e.reshape(e.shape[0] // V7X_SUBLANES, V7X_SUBLANES, lanes), axis=0)

    kchunk = lax.broadcasted_iota(jnp.int32, (kb, 1), 0) // CHUNK
    qchunk = (lax.broadcasted_iota(jnp.int32, (1, lanes), 1) % tq) // CHUNK
    diag_masks = [kchunk + d * (kb // CHUNK) <= qchunk for d in range(n_diag)]

    def late_queries(a, d, axis):
        lo = d * kb
        if axis == 0:
            return jnp.concatenate([a[lo:tq], a[tq + lo:lanes]], axis=0)
        return jnp.concatenate([a[:, lo:tq], a[:, tq + lo:lanes]], axis=1)

    def add_late(full, part, d):
        lo, w = d * kb, tq - d * kb
        return jnp.concatenate([full[:, 0:lo], full[:, lo:tq] + part[:, 0:w],
                                full[:, tq:tq + lo], full[:, tq + lo:lanes] + part[:, w:2 * w]],
                               axis=1)

    @pl.when(bound_ref[0] <= SAFE_SCORE_BOUND)
    def _():
        def key_pass(k, pieces, assign, late_blocks=()):
            s_next = scores(k_slab(k, 0), 0)
            for h in range(N_HEADS):
                s = s_next
                if h + 1 < N_HEADS:
                    s_next = scores(k_slab(k, h + 1), h + 1)
                s_late = [lax.dot_general(k_slab(kl, h), late_queries(qz_ref[h], d, 0),
                                          (((1,), (1,)), ((), ())), preferred_element_type=F32)
                          for d, kl, _ in late_blocks]
                e = jnp.exp2(s)
                l_new = acc_new = None
                for row0, rows, vt, mask in pieces:
                    ep = e[row0:row0 + rows, :]
                    if mask is not None:
                        ep = jnp.where(mask, ep, 0.0)
                    lp = sublane_partial(ep)
                    ap = jnp.dot(vt[h * V_DIM:(h + 1) * V_DIM, :], ep.astype(BF16),
                                 preferred_element_type=F32)
                    l_new = lp if l_new is None else l_new + lp
                    acc_new = ap if acc_new is None else acc_new + ap
                for (d, _, vt), sl in zip(late_blocks, s_late):
                    ep = jnp.where(late_queries(diag_masks[d], d, 1), jnp.exp2(sl), 0.0)
                    lp = jnp.sum(ep.reshape(kb // V7X_SUBLANES, V7X_SUBLANES, ep.shape[1]), axis=0)
                    ap = jnp.dot(vt[h * V_DIM:(h + 1) * V_DIM, :], ep.astype(BF16),
                                 preferred_element_type=F32)
                    l_new = add_late(l_new, lp, d)
                    acc_new = add_late(acc_new, ap, d)
                if assign:
                    l_ref[h] = l_new
                    acc_ref[h] = acc_new
                else:
                    l_ref[h] += l_new
                    acc_ref[h] += acc_new

        n_big = n_full // big
        rem = n_full - n_big * big
        for r in range(0, big, n_diag):
            @pl.when(rem == r)
            def _():
                first = n_full - r
                n_blocks = r + 1
                kblocks = kb_ref[0, pl.ds(first, n_blocks)].reshape(n_blocks * kb, WIDTH)
                k = jnp.concatenate([km_ref[...], kblocks], axis=0)
                pieces = [(0, N_META, vmt_ref[...], None)]
                pieces += [(N_META + t * kb, kb, vt_ref[0, first + t],
                            None if t < r else diag_masks[0]) for t in range(n_blocks)]
                late = [(d, kb_ref[0, n_full + d], vt_ref[0, n_full + d])
                        for d in range(1, n_diag)]
                key_pass(k, pieces, assign=True, late_blocks=late)

        def big_block(i, carry):
            first = i * big
            k = kb_ref[0, pl.ds(first, big)].reshape(big * kb, WIDTH)
            key_pass(k, [(t * kb, kb, vt_ref[0, first + t], None) for t in range(big)], assign=False)
            return carry

        lax.fori_loop(0, n_big, big_block, 0)

    @pl.when(jnp.logical_not(bound_ref[0] <= SAFE_SCORE_BOUND))
    def _():
        for h in range(N_HEADS):
            s = scores(k_slab(km_ref, h), h)
            m = jnp.max(s, axis=0, keepdims=True)
            e = jnp.exp2(s - m)
            m_ref[h:h + 1, :] = m
            l_ref[h] = jnp.zeros((V7X_SUBLANES, lanes), F32)
            l_ref[h, 0:1, :] = jnp.sum(e, axis=0, keepdims=True)
            acc_ref[h] = jnp.dot(vmt_ref[h * V_DIM:(h + 1) * V_DIM, :], e.astype(BF16),
                                 preferred_element_type=F32)

        def block(i, mask):
            k = kb_ref[0, i]
            vt = vt_ref[0, i]
            for h in range(N_HEADS):
                s = scores(k_slab(k, h), h)
                if mask is not None:
                    s = jnp.where(mask, s, NEG_BIG)
                m_old = m_ref[h:h + 1, :]
                m_new = jnp.maximum(m_old, jnp.max(s, axis=0, keepdims=True))
                alpha = jnp.exp2(m_old - m_new)
                e = jnp.exp2(s - m_new)
                l_ref[h, 0:1, :] = alpha * l_ref[h, 0:1, :] + jnp.sum(e, axis=0, keepdims=True)
                acc_ref[h] = alpha * acc_ref[h] + jnp.dot(
                    vt[h * V_DIM:(h + 1) * V_DIM, :], e.astype(BF16), preferred_element_type=F32)
                m_ref[h:h + 1, :] = m_new

        def full_block(i, carry):
            block(i, None)
            return carry

        lax.fori_loop(0, n_full, full_block, 0)
        for d in range(n_diag):
            block(n_full + d, diag_masks[d])

    base = _conv_branch_out(x_ref[0], oc_ref[0], wout_ref)
    lam = _lambda(lq1_ref, lk1_ref, lq2_ref, lk2_ref)
    o_heads = []
    for h in range(N_HEADS):
        o2 = acc_ref[h] / jnp.sum(l_ref[h], axis=0, keepdims=True)
        o_heads.append((o2[:, 0:tq] - lam * o2[:, tq:lanes]).T)
    y_ref[0] = _attn_branch_out(base, _gated_attn(o_heads, sga_ref[0], subg_ref[...]), wout_ref)


def _prompt_attn(bound, x, q, kb4, vt4, km_bf, vmt_bf, sga, oc, subg, lq1, lk1, lq2, lk2, w_out_bf,
                 *, tq, kb, big):
    nb, s, d = x.shape
    nq = s // tq
    assert tq % kb == 0 and big % (tq // kb) == 0
    single_buffer = dict(pipeline_mode=pl.Buffered(1))
    row = lambda b, j: (b, j, 0)
    const2 = lambda b, j: (0, 0)
    whole = lambda b, j: (b, 0, 0, 0)
    in_specs = [
        pl.BlockSpec(memory_space=pltpu.SMEM),
        pl.BlockSpec((1, tq, d), row),
        pl.BlockSpec((1, tq, WIDTH), row),
        pl.BlockSpec((1, s // kb, kb, WIDTH), whole),
        pl.BlockSpec((1, s // kb, WIDTH, kb), whole, **single_buffer),
        pl.BlockSpec((N_META, WIDTH), const2),
        pl.BlockSpec((WIDTH, N_META), const2),
        pl.BlockSpec((1, tq, WIDTH), row),
        pl.BlockSpec((1, tq, WIDTH), row),
        pl.BlockSpec((1, V_DIM), const2),
        pl.BlockSpec((1, HEAD_DIM), const2),
        pl.BlockSpec((1, HEAD_DIM), const2),
        pl.BlockSpec((1, HEAD_DIM), const2),
        pl.BlockSpec((1, HEAD_DIM), const2),
        pl.BlockSpec(w_out_bf.shape, const2),
    ]
    return pl.pallas_call(
        functools.partial(_prompt_attn_kernel, tq=tq, kb=kb, big=big),
        grid=(nb, nq),
        in_specs=in_specs,
        out_specs=pl.BlockSpec((1, tq, d), row),
        out_shape=jax.ShapeDtypeStruct((nb, s, d), F32),
        scratch_shapes=[
            pltpu.VMEM((N_HEADS, 2 * tq, V_DIM), BF16),
            pltpu.VMEM((N_HEADS, 2 * tq), F32),
            pltpu.VMEM((N_HEADS, V7X_SUBLANES, 2 * tq), F32),
            pltpu.VMEM((N_HEADS, V_DIM, 2 * tq), F32),
        ],
        compiler_params=pltpu.CompilerParams(
            dimension_semantics=("arbitrary", "arbitrary"),
            vmem_limit_bytes=V7X_VMEM_LIMIT),
        name="prompt_attn",
    )(bound, x, q, kb4, vt4, km_bf, vmt_bf, sga, oc, subg, lq1, lk1, lq2, lk2, w_out_bf)


def _sample_attn_kernel(x_ref, q_ref, kn_ref, vn_ref, ck_ref, cv_ref, km_ref, vm_ref, sga_ref,
                        oc_ref, subg_ref, lq1_ref, lk1_ref, lq2_ref, lk2_ref, wout_ref, y_ref,
                        kcat, vcat, mix_ref, *, past, t_new, n_keys_pad, first):
    b = pl.program_id(0)
    nb = pl.num_programs(0)
    n_keys = past + N_META + t_new
    for h in range(N_HEADS):
        slab = slice(h * V_DIM, (h + 1) * V_DIM)
        kcat[0:past, slab] = ck_ref[0, pl.ds(h, past, stride=N_HEADS), :].astype(BF16)
        kcat[past + N_META:n_keys, slab] = kn_ref[0, pl.ds(h, t_new, stride=N_HEADS), :].astype(BF16)
        vcat[0:past, slab] = cv_ref[0, pl.ds(h, past, stride=N_HEADS), :].astype(BF16)
        vcat[past:past + N_META, slab] = vm_ref[pl.ds(h, N_META, stride=N_HEADS), :].astype(BF16)
        vcat[past + N_META:n_keys, slab] = vn_ref[0, pl.ds(h, t_new, stride=N_HEADS), :].astype(BF16)
    kcat[past:past + N_META, :] = km_ref[...]
    kcat[n_keys:n_keys_pad, :] = jnp.zeros((n_keys_pad - n_keys, WIDTH), BF16)
    vcat[n_keys:n_keys_pad, :] = jnp.zeros((n_keys_pad - n_keys, WIDTH), BF16)

    lane = lax.broadcasted_iota(jnp.int32, (1, V_DIM), 1)
    valid = lax.broadcasted_iota(jnp.int32, (1, n_keys_pad), 1) < n_keys
    lam = _lambda(lq1_ref, lk1_ref, lq2_ref, lk2_ref)
    o_heads = []
    for h in range(N_HEADS):
        qh = q_ref[0, :, h * V_DIM:(h + 1) * V_DIM]
        q2 = jnp.concatenate([jnp.where(lane < HEAD_DIM, qh, jnp.zeros_like(qh)),
                              jnp.where(lane >= HEAD_DIM, qh, jnp.zeros_like(qh))], axis=0)
        s = lax.dot_general(q2, kcat[:, h * V_DIM:(h + 1) * V_DIM], (((1,), (1,)), ((), ())),
                            preferred_element_type=F32)
        s = jnp.where(valid, s, NEG_BIG)
        e = jnp.exp2(s - jnp.max(s, axis=-1, keepdims=True))
        o2 = jnp.dot(e.astype(BF16), vcat[:, h * V_DIM:(h + 1) * V_DIM],
                     preferred_element_type=F32) / jnp.sum(e, axis=-1, keepdims=True)
        o_heads.append(o2[0:t_new] - lam * o2[t_new:2 * t_new])
    rows = pl.ds(pl.multiple_of(b * t_new, t_new), t_new)
    mix_ref[rows, :] = _gated_attn(o_heads, sga_ref[b + first], subg_ref[...])

    @pl.when(b == nb - 1)
    def _():
        n_rows = mix_ref.shape[0]
        oc = oc_ref[first:first + n_rows // t_new]
        base = _conv_branch_out(x_ref[...].reshape(n_rows, x_ref.shape[-1]),
                                oc.reshape(n_rows, WIDTH), wout_ref)
        y_ref[...] = _attn_branch_out(base, mix_ref[...], wout_ref).reshape(y_ref.shape)


def _sample_attn(x, q, kn, vn, ck, cv, km_bf, vm, sga, oc, subg, lq1, lk1, lq2, lk2, w_out_bf, *,
                 first):
    nb, t_new, d = x.shape
    n_proj = q.shape[0]
    past = ck.shape[1] // N_HEADS
    n_keys_pad = -(-(past + N_META + t_new) // V_DIM) * V_DIM
    row = lambda b: (b, 0, 0)
    proj_row = lambda b: (b + first, 0, 0)
    const2 = lambda b: (0, 0)
    const3 = lambda b: (0, 0, 0)
    in_specs = [
        pl.BlockSpec((nb, t_new, d), const3),
        pl.BlockSpec((1, t_new, WIDTH), proj_row),
        pl.BlockSpec((1, t_new * N_HEADS, V_DIM), proj_row),
        pl.BlockSpec((1, t_new * N_HEADS, V_DIM), proj_row),
        pl.BlockSpec((1, past * N_HEADS, V_DIM), row),
        pl.BlockSpec((1, past * N_HEADS, V_DIM), row),
        pl.BlockSpec((N_META, WIDTH), const2),
        pl.BlockSpec((N_META * N_HEADS, V_DIM), const2),
        pl.BlockSpec((n_proj, t_new, WIDTH), const3),
        pl.BlockSpec((n_proj, t_new, WIDTH), const3),
        pl.BlockSpec((1, V_DIM), const2),
        pl.BlockSpec((1, HEAD_DIM), const2),
        pl.BlockSpec((1, HEAD_DIM), const2),
        pl.BlockSpec((1, HEAD_DIM), const2),
        pl.BlockSpec((1, HEAD_DIM), const2),
        pl.BlockSpec(w_out_bf.shape, const2),
    ]
    return pl.pallas_call(
        functools.partial(_sample_attn_kernel, past=past, t_new=t_new, n_keys_pad=n_keys_pad,
                          first=first),
        grid=(nb,),
        in_specs=in_specs,
        out_specs=pl.BlockSpec((nb, t_new, d), const3),
        out_shape=jax.ShapeDtypeStruct((nb, t_new, d), F32),
        scratch_shapes=[pltpu.VMEM((n_keys_pad, WIDTH), BF16),
                        pltpu.VMEM((n_keys_pad, WIDTH), BF16),
                        pltpu.VMEM((nb * t_new, WIDTH), BF16)],
        compiler_params=pltpu.CompilerParams(
            dimension_semantics=("arbitrary",),
            vmem_limit_bytes=V7X_VMEM_LIMIT),
        name="sample_attn",
    )(x, q, kn, vn, ck, cv, km_bf, vm, sga, oc, subg, lq1, lk1, lq2, lk2, w_out_bf)


def _rope_tables(pos):
    half = ROT_DIM // 2
    inv = ROPE_THETA ** (-jnp.arange(0, ROT_DIM, 2, dtype=F32) / ROT_DIM)
    ang = pos.astype(F32)[:, None] * inv[None, :]
    cos, sin = lax.optimization_barrier((jnp.cos(ang), jnp.sin(ang)))
    within = jnp.arange(V_DIM, dtype=jnp.int32)[None, :] % HEAD_DIM
    freq = jnp.where(within < ROT_DIM, within % half, -1)
    sign = jnp.where(within < half, -1.0, 1.0).astype(F32)
    t = pos.shape[0]
    cos_t = jnp.ones((t, V_DIM), F32)
    sin_t = jnp.zeros((t, V_DIM), F32)
    for f in range(half):
        cos_t = jnp.where(freq == f, cos[:, f:f + 1], cos_t)
        sin_t = jnp.where(freq == f, sign * sin[:, f:f + 1], sin_t)
    return cos_t, sin_t


def _score_bound(qg, kg):
    bound = Q_SCALE * HEAD_DIM * jnp.max(jnp.abs(qg)) * jnp.max(jnp.abs(kg)) * (1.0 + 2.0 ** -6)
    return bound.reshape(1).astype(F32)


def _pad_state(state):
    return jnp.pad(state, ((0, 0), (STATE_ROWS - (CONV_W - 1), 0), (0, 0)))


def _tile_sizes(seq):
    kb = KEY_BLOCK if seq % KEY_BLOCK == 0 else seq
    tq = QUERY_TILE if seq % QUERY_TILE == 0 else kb
    return (2 * tq if seq % (2 * tq) == 0 else tq), tq, kb


def kernel(x_prompt, x_sample, cache_k, cache_v, state_conv, meta_tokens, ln_g, w_in, q_norm_g,
           k_norm_g, lam_q1, lam_k1, lam_q2, lam_k2, subln_g, conv_w, conv_b, w_out):
    assert ln_g.shape[0] == 1, "single-layer stack"
    nb, seq, d = x_prompt.shape
    db, t_new, _ = x_sample.shape
    past = cache_k.shape[2]

    w_in_bf = w_in[0].astype(BF16)
    w_out_bf = w_out[0].astype(BF16)
    lng = ln_g[0][None, :]
    qg = jnp.tile(q_norm_g[0], 2 * N_HEADS)[None, :]
    kg = jnp.tile(k_norm_g[0], 2 * N_HEADS)[None, :]
    subg = subln_g[0][None, :]
    cw, cb = conv_w[0], conv_b[0][None, :]
    lams = [p[0][None, :] for p in (lam_q1, lam_k1, lam_q2, lam_k2)]
    grp = jnp.arange(GROUP_SUM_WIDTH, dtype=jnp.int32) // HEAD_DIM
    gmat = (grp[:, None] == grp[None, :]).astype(BF16)

    pos_m = jnp.arange(N_META, dtype=jnp.int32)
    pos_p = N_META + jnp.arange(seq, dtype=jnp.int32)
    pos_s = N_META + past + jnp.arange(t_new, dtype=jnp.int32)
    weights = (lng, w_in_bf, qg, kg, cw, cb, gmat)
    tm, tq, kb = _tile_sizes(seq)

    assert t_new == N_META, "meta tokens and new frames are projected as equal-length streams"
    cos_m, sin_m = _rope_tables(pos_m)
    cos_s, sin_s = _rope_tables(pos_s)
    n_small = 1 + db
    q_sm, k_sm, v_sm, kb_sm, sga_sm, oc_sm, tail_sm = _proj_call(
        jnp.concatenate([meta_tokens[None], x_sample], axis=0),
        jnp.concatenate([jnp.zeros((1, STATE_ROWS, WIDTH), F32), _pad_state(state_conv[0])], axis=0),
        jnp.concatenate([cos_m] + [cos_s] * db, axis=0),
        jnp.concatenate([sin_m] + [sin_s] * db, axis=0),
        *weights, sb=n_small, tm=t_new, tkv=t_new, emit_vt=False)
    km, vm, km_bf, tail_m = k_sm[0], v_sm[0], kb_sm[0], tail_sm[0:1]
    ks, vs, tail_s = k_sm[1:], v_sm[1:], tail_sm[1:]

    state_p = jnp.broadcast_to(tail_m, (nb, STATE_ROWS, WIDTH))
    qp, kp, vp, kp_bf, sga_p, oc_p, tail_p, vtp = _proj_call(
        x_prompt, state_p, *_rope_tables(pos_p), *weights, sb=1, tm=tm, tkv=kb, emit_vt=True,
        row_offset=N_META)
    y_prompt = _prompt_attn(
        _score_bound(q_norm_g[0], k_norm_g[0]), x_prompt, qp,
        kp_bf.reshape(nb, seq // kb, kb, WIDTH), vtp, km_bf,
        vm.reshape(N_META, WIDTH).T.astype(BF16),
        sga_p, oc_p, subg, *lams, w_out_bf, tq=tq, kb=kb, big=BIG_KEY_BLOCKS)
    kp = kp.at[:, :N_META * N_HEADS].set(jnp.broadcast_to(km[None], (nb,) + km.shape))
    vp = vp.at[:, :N_META * N_HEADS].set(jnp.broadcast_to(vm[None], (nb,) + vm.shape))

    y_sample = _sample_attn(
        x_sample, q_sm, k_sm, v_sm, cache_k[0].reshape(db, past * N_HEADS, V_DIM),
        cache_v[0].reshape(db, past * N_HEADS, V_DIM), km_bf, vm, sga_sm, oc_sm, subg, *lams,
        w_out_bf, first=1)

    def heads_form(a, rows):
        return a.reshape(1, a.shape[0], rows, N_HEADS, V_DIM)

    tail_rows = slice(STATE_ROWS - (CONV_W - 1), STATE_ROWS)
    return (y_prompt, y_sample, heads_form(kp, N_META + seq), heads_form(vp, N_META + seq),
            tail_p[None, :, tail_rows], heads_form(ks, t_new), heads_form(vs, t_new),
            tail_s[None, :, tail_rows])
```

```python
import functools
import math

import jax
import jax.numpy as jnp
from jax import lax
from jax.experimental import pallas as pl
from jax.experimental.pallas import tpu as pltpu

F32 = jnp.float32
BF16 = jnp.bfloat16

CHUNK = 64
N_META = 16
N_HEADS = 4
HEAD_DIM = 64
V_DIM = 2 * HEAD_DIM
WIDTH = N_HEADS * V_DIM
CONV_W = 3
ROT_DIM = HEAD_DIM // 4
ROPE_THETA = 500000.0
EPS = 1e-6
LAMBDA_INIT = 0.8 - 0.6 * math.exp(-0.3 * 0)
Q_SCALE = HEAD_DIM ** -0.5 * math.log2(math.e)
NEG_BIG = -1e30
SAFE_SCORE_BOUND = 40.0
V7X_SUBLANES = 8
V7X_MXU_WIDTH = 256
V7X_VMEM_LIMIT = 60 * 1024 * 1024

KEY_BLOCK = V7X_MXU_WIDTH
QUERY_TILE = 2 * KEY_BLOCK
BIG_KEY_BLOCKS = 8
GROUP_SUM_WIDTH = V7X_MXU_WIDTH
STATE_ROWS = V7X_SUBLANES


def _silu(x):
    return x / (1.0 + jnp.exp(-x))


def _proj_kernel(x_ref, st_ref, cos_ref, sin_ref, lng_ref, win_ref, qg_ref, kg_ref, cw_ref, cb_ref,
                 gmat_ref, q_ref, kf_ref, vf_ref, kb_ref, sga_ref, oc_ref, tail_ref, *rest,
                 sb, tm, tkv, emit_vt):
    if emit_vt:
        vt_ref, ubuf = rest
    else:
        (ubuf,) = rest
    j = pl.program_id(1)
    rows = sb * tm

    @pl.when(j == 0)
    def _():
        ubuf[:, 0:STATE_ROWS, :] = st_ref[...]

    @pl.when(j > 0)
    def _():
        ubuf[:, 0:STATE_ROWS, :] = ubuf[:, tm:tm + STATE_ROWS, :]

    x = x_ref[...].reshape(rows, x_ref.shape[-1])
    xg = (x * lng_ref[...]).astype(BF16)
    inv_rms = lax.rsqrt(jnp.mean(x * x, axis=-1, keepdims=True) + EPS)

    def proj(i):
        return inv_rms * jnp.dot(xg, win_ref[:, i * WIDTH:(i + 1) * WIDTH],
                                 preferred_element_type=F32)

    def group_mean_sq(t):
        tt = (t * t).astype(BF16)
        halves = [jnp.dot(tt[:, i * GROUP_SUM_WIDTH:(i + 1) * GROUP_SUM_WIDTH], gmat_ref[...],
                          preferred_element_type=F32) for i in range(WIDTH // GROUP_SUM_WIDTH)]
        return jnp.concatenate(halves, axis=1) * (1.0 / HEAD_DIM)

    cos = cos_ref[...]
    sin = sin_ref[...]
    lane = lax.broadcasted_iota(jnp.int32, (1, V_DIM), 1) % HEAD_DIM
    take_lower = lane >= ROT_DIM // 2

    def norm_rope(t, mean_sq, g):
        tn = t * lax.rsqrt(mean_sq + EPS) * g
        outs = []
        for h in range(N_HEADS):
            th = tn[:, h * V_DIM:(h + 1) * V_DIM]
            lower = pltpu.roll(th, ROT_DIM // 2, axis=1)
            upper = pltpu.roll(th, V_DIM - ROT_DIM // 2, axis=1)
            outs.append(th * cos + jnp.where(take_lower, lower, upper) * sin)
        return jnp.concatenate(outs, axis=1)

    def per_stream(a):
        return a.reshape(sb, tm, a.shape[-1])

    pq = proj(0)
    pk = proj(1)
    msq = group_mean_sq(pq)
    v = proj(2)
    msk = group_mean_sq(pk)
    ga = proj(3)
    q_ref[...] = per_stream((norm_rope(pq, msq, qg_ref[...]) * Q_SCALE).astype(BF16))
    bb = proj(4)
    k = norm_rope(pk, msk, kg_ref[...])
    kb_ref[...] = per_stream(k.astype(BF16))
    cc = proj(5)
    for b in range(sb):
        for h in range(N_HEADS):
            src = (slice(b * tm, (b + 1) * tm), slice(h * V_DIM, (h + 1) * V_DIM))
            kf_ref[b, pl.ds(h, tm, stride=N_HEADS), :] = k[src]
            vf_ref[b, pl.ds(h, tm, stride=N_HEADS), :] = v[src]
    hh = proj(6)
    if emit_vt:
        for s in range(tm // tkv):
            vt_ref[0, s] = v[s * tkv:(s + 1) * tkv, :].T.astype(BF16)
    gc = proj(7)
    sga_ref[...] = per_stream(_silu(ga).astype(BF16))

    u = cc * hh
    ys = []
    for b in range(sb):
        ubuf[b, STATE_ROWS:STATE_ROWS + tm, :] = u[b * tm:(b + 1) * tm, :]
        y = cb_ref[...]
        for tap in range(CONV_W):
            off = STATE_ROWS - (CONV_W - 1) + tap
            y = y + cw_ref[tap:tap + 1, :] * ubuf[b, off:off + tm, :]
        ys.append(y)
    y = ys[0] if sb == 1 else jnp.concatenate(ys, axis=0)
    oc_ref[...] = per_stream((_silu(gc) * (bb * y)).astype(BF16))
    tail_ref[...] = ubuf[:, tm:tm + STATE_ROWS, :]


def _proj_call(x, state8, cos_t, sin_t, ln_g, w_in_bf, qg, kg, conv_w, conv_b, gmat, *, sb, tm, tkv,
               emit_vt, row_offset=0):
    nb, t, d = x.shape
    nt = t // tm
    assert t % tm == 0 and tm % tkv == 0 and nb % sb == 0 and not (emit_vt and sb > 1)
    assert cos_t.shape[0] == sb * t
    row = lambda b, j: (b, j, 0)
    const2 = lambda b, j: (0, 0)
    in_specs = [
        pl.BlockSpec((sb, tm, d), row),
        pl.BlockSpec((sb, STATE_ROWS, WIDTH), lambda b, j: (b, 0, 0)),
        pl.BlockSpec((sb * tm, V_DIM), lambda b, j: (j, 0)),
        pl.BlockSpec((sb * tm, V_DIM), lambda b, j: (j, 0)),
        pl.BlockSpec((1, d), const2),
        pl.BlockSpec(w_in_bf.shape, const2, pipeline_mode=pl.Buffered(1)),
        pl.BlockSpec((1, WIDTH), const2),
        pl.BlockSpec((1, WIDTH), const2),
        pl.BlockSpec((CONV_W, WIDTH), const2),
        pl.BlockSpec((1, WIDTH), const2),
        pl.BlockSpec((GROUP_SUM_WIDTH, GROUP_SUM_WIDTH), const2),
    ]
    tile = pl.BlockSpec((sb, tm, WIDTH), row)
    heads_tile = pl.BlockSpec(
        (pl.Element(sb), pl.Element(tm * N_HEADS), pl.Element(V_DIM)),
        lambda b, j: (b * sb, pl.multiple_of((row_offset + j * tm) * N_HEADS, V7X_SUBLANES), 0))
    out_shape = [
        jax.ShapeDtypeStruct((nb, t, WIDTH), BF16),
        jax.ShapeDtypeStruct((nb, (row_offset + t) * N_HEADS, V_DIM), F32),
        jax.ShapeDtypeStruct((nb, (row_offset + t) * N_HEADS, V_DIM), F32),
        jax.ShapeDtypeStruct((nb, t, WIDTH), BF16),
        jax.ShapeDtypeStruct((nb, t, WIDTH), BF16),
        jax.ShapeDtypeStruct((nb, t, WIDTH), BF16),
        jax.ShapeDtypeStruct((nb, STATE_ROWS, WIDTH), F32),
    ]
    out_specs = [tile, heads_tile, heads_tile, tile, tile, tile,
                 pl.BlockSpec((sb, STATE_ROWS, WIDTH), lambda b, j: (b, 0, 0))]
    if emit_vt:
        out_shape.append(jax.ShapeDtypeStruct((nb, t // tkv, WIDTH, tkv), BF16))
        out_specs.append(pl.BlockSpec((1, tm // tkv, WIDTH, tkv), lambda b, j: (b, j, 0, 0)))
    return pl.pallas_call(
        functools.partial(_proj_kernel, sb=sb, tm=tm, tkv=tkv, emit_vt=emit_vt),
        grid=(nb // sb, nt),
        in_specs=in_specs,
        out_specs=out_specs,
        out_shape=out_shape,
        scratch_shapes=[pltpu.VMEM((sb, tm + STATE_ROWS, WIDTH), F32)],
        compiler_params=pltpu.CompilerParams(
            dimension_semantics=("arbitrary", "arbitrary"),
            vmem_limit_bytes=V7X_VMEM_LIMIT),
        name="proj",
    )(x, state8, cos_t, sin_t, ln_g, w_in_bf, qg, kg, conv_w, conv_b, gmat)


def _lambda(lq1_ref, lk1_ref, lq2_ref, lk2_ref):
    a = jnp.sum(lq1_ref[...] * lk1_ref[...], axis=-1, keepdims=True)
    b = jnp.sum(lq2_ref[...] * lk2_ref[...], axis=-1, keepdims=True)
    return jnp.exp(a) - jnp.exp(b) + LAMBDA_INIT


def _conv_branch_out(x, oc, wout_ref):
    return x + jnp.dot(oc, wout_ref[WIDTH:, :], preferred_element_type=F32)


def _gated_attn(o_heads, sga, subg):
    parts = []
    for h, o in enumerate(o_heads):
        ms = jnp.mean(o * o, axis=-1, keepdims=True)
        on = o * lax.rsqrt(ms + EPS) * subg * (1.0 - LAMBDA_INIT)
        parts.append((sga[:, h * V_DIM:(h + 1) * V_DIM].astype(F32) * on).astype(BF16))
    return jnp.concatenate(parts, axis=1)


def _attn_branch_out(base, mix, wout_ref):
    return base + jnp.dot(mix, wout_ref[:WIDTH, :], preferred_element_type=F32)


def _prompt_attn_kernel(bound_ref, x_ref, q_ref, kb_ref, vt_hbm, km_ref, vmt_ref, sga_ref, oc_ref,
                        subg_ref, lq1_ref, lk1_ref, lq2_ref, lk2_ref, wout_ref, y_ref,
                        qz_ref, m_ref, l_ref, acc_ref, vt_buf, vt_sem, *, tq, kb, big, nq):
    j = pl.program_id(1)
    lanes = 2 * tq
    n_diag = tq // kb
    n_full = j * n_diag

    def vt_chunk_copy(c):
        blocks = pl.ds(c * n_diag, n_diag)
        return pltpu.make_async_copy(vt_hbm.at[pl.program_id(0), blocks], vt_buf.at[blocks],
                                     vt_sem.at[c])

    @pl.when(j == 0)
    def _():
        for c in range(nq):
            vt_chunk_copy(c).start()

    vt_chunk_copy(j).wait()

    lane = lax.broadcasted_iota(jnp.int32, (1, V_DIM), 1)
    for h in range(N_HEADS):
        qh = q_ref[0, :, h * V_DIM:(h + 1) * V_DIM]
        qz_ref[h, 0:tq, :] = jnp.where(lane < HEAD_DIM, qh, jnp.zeros_like(qh))
        qz_ref[h, tq:lanes, :] = jnp.where(lane >= HEAD_DIM, qh, jnp.zeros_like(qh))

    def scores(k_h, h):
        return lax.dot_general(k_h, qz_ref[h], (((1,), (1,)), ((), ())), preferred_element_type=F32)

    def k_slab(k, h):
        return k[:, h * V_DIM:(h + 1) * V_DIM]

    def sublane_partial(e):
        return jnp.sum(e.reshape(e.shape[0] // V7X_SUBLANES, V7X_SUBLANES, lanes), axis=0)

    kchunk = lax.broadcasted_iota(jnp.int32, (kb, 1), 0) // CHUNK
    qchunk = (lax.broadcasted_iota(jnp.int32, (1, lanes), 1) % tq) // CHUNK
    diag_masks = [kchunk + d * (kb // CHUNK) <= qchunk for d in range(n_diag)]

    def late_queries(a, d, axis):
        lo = d * kb
        if axis == 0:
            return jnp.concatenate([a[lo:tq], a[tq + lo:lanes]], axis=0)
        return jnp.concatenate([a[:, lo:tq], a[:, tq + lo:lanes]], axis=1)

    def add_late(full, part, d):
        lo, w = d * kb, tq - d * kb
        return jnp.concatenate([full[:, 0:lo], full[:, lo:tq] + part[:, 0:w],
                                full[:, tq:tq + lo], full[:, tq + lo:lanes] + part[:, w:2 * w]],
                               axis=1)

    @pl.when(bound_ref[0] <= SAFE_SCORE_BOUND)
    def _():
        def key_pass(k, pieces, assign, late_blocks=()):
            s_next = scores(k_slab(k, 0), 0)
            for h in range(N_HEADS):
                s = s_next
                if h + 1 < N_HEADS:
                    s_next = scores(k_slab(k, h + 1), h + 1)
                s_late = [lax.dot_general(k_slab(kl, h), late_queries(qz_ref[h], d, 0),
                                          (((1,), (1,)), ((), ())), preferred_element_type=F32)
                          for d, kl, _ in late_blocks]
                e = jnp.exp2(s)
                l_new = acc_new = None
                for row0, rows, vt, mask in pieces:
                    ep = e[row0:row0 + rows, :]
                    if mask is not None:
                        ep = jnp.where(mask, ep, 0.0)
                    lp = sublane_partial(ep)
                    ap = jnp.dot(vt[h * V_DIM:(h + 1) * V_DIM, :], ep.astype(BF16),
                                 preferred_element_type=F32)
                    l_new = lp if l_new is None else l_new + lp
                    acc_new = ap if acc_new is None else acc_new + ap
                for (d, _, vt), sl in zip(late_blocks, s_late):
                    ep = jnp.where(late_queries(diag_masks[d], d, 1), jnp.exp2(sl), 0.0)
                    lp = jnp.sum(ep.reshape(kb // V7X_SUBLANES, V7X_SUBLANES, ep.shape[1]), axis=0)
                    ap = jnp.dot(vt[h * V_DIM:(h + 1) * V_DIM, :], ep.astype(BF16),
                                 preferred_element_type=F32)
                    l_new = add_late(l_new, lp, d)
                    acc_new = add_late(acc_new, ap, d)
                if assign:
                    l_ref[h] = l_new
                    acc_ref[h] = acc_new
                else:
                    l_ref[h] += l_new
                    acc_ref[h] += acc_new

        n_big = n_full // big
        rem = n_full - n_big * big
        for r in range(0, big, n_diag):
            @pl.when(rem == r)
            def _():
                first = n_full - r
                n_blocks = r + 1
                kblocks = kb_ref[0, pl.ds(first, n_blocks)].reshape(n_blocks * kb, WIDTH)
                k = jnp.concatenate([km_ref[...], kblocks], axis=0)
                pieces = [(0, N_META, vmt_ref[...], None)]
                pieces += [(N_META + t * kb, kb, vt_buf[first + t],
                            None if t < r else diag_masks[0]) for t in range(n_blocks)]
                late = [(d, kb_ref[0, n_full + d], vt_buf[n_full + d])
                        for d in range(1, n_diag)]
                key_pass(k, pieces, assign=True, late_blocks=late)

        def big_block(i, carry):
            first = i * big
            k = kb_ref[0, pl.ds(first, big)].reshape(big * kb, WIDTH)
            key_pass(k, [(t * kb, kb, vt_buf[first + t], None) for t in range(big)], assign=False)
            return carry

        lax.fori_loop(0, n_big, big_block, 0)

    @pl.when(jnp.logical_not(bound_ref[0] <= SAFE_SCORE_BOUND))
    def _():
        for h in range(N_HEADS):
            s = scores(k_slab(km_ref, h), h)
            m = jnp.max(s, axis=0, keepdims=True)
            e = jnp.exp2(s - m)
            m_ref[h:h + 1, :] = m
            l_ref[h] = jnp.zeros((V7X_SUBLANES, lanes), F32)
            l_ref[h, 0:1, :] = jnp.sum(e, axis=0, keepdims=True)
            acc_ref[h] = jnp.dot(vmt_ref[h * V_DIM:(h + 1) * V_DIM, :], e.astype(BF16),
                                 preferred_element_type=F32)

        def block(i, mask):
            k = kb_ref[0, i]
            vt = vt_buf[i]
            for h in range(N_HEADS):
                s = scores(k_slab(k, h), h)
                if mask is not None:
                    s = jnp.where(mask, s, NEG_BIG)
                m_old = m_ref[h:h + 1, :]
                m_new = jnp.maximum(m_old, jnp.max(s, axis=0, keepdims=True))
                alpha = jnp.exp2(m_old - m_new)
                e = jnp.exp2(s - m_new)
                l_ref[h, 0:1, :] = alpha * l_ref[h, 0:1, :] + jnp.sum(e, axis=0, keepdims=True)
                acc_ref[h] = alpha * acc_ref[h] + jnp.dot(
                    vt[h * V_DIM:(h + 1) * V_DIM, :], e.astype(BF16), preferred_element_type=F32)
                m_ref[h:h + 1, :] = m_new

        def full_block(i, carry):
            block(i, None)
            return carry

        lax.fori_loop(0, n_full, full_block, 0)
        for d in range(n_diag):
            block(n_full + d, diag_masks[d])

    base = _conv_branch_out(x_ref[0], oc_ref[0], wout_ref)
    lam = _lambda(lq1_ref, lk1_ref, lq2_ref, lk2_ref)
    o_heads = []
    for h in range(N_HEADS):
        o2 = acc_ref[h] / jnp.sum(l_ref[h], axis=0, keepdims=True)
        o_heads.append((o2[:, 0:tq] - lam * o2[:, tq:lanes]).T)
    y_ref[0] = _attn_branch_out(base, _gated_attn(o_heads, sga_ref[0], subg_ref[...]), wout_ref)


def _prompt_attn(bound, x, q, kb4, vt4, km_bf, vmt_bf, sga, oc, subg, lq1, lk1, lq2, lk2, w_out_bf,
                 *, tq, kb, big):
    nb, s, d = x.shape
    nq = s // tq
    assert tq % kb == 0 and big % (tq // kb) == 0
    row = lambda b, j: (b, j, 0)
    const2 = lambda b, j: (0, 0)
    whole = lambda b, j: (b, 0, 0, 0)
    in_specs = [
        pl.BlockSpec(memory_space=pltpu.SMEM),
        pl.BlockSpec((1, tq, d), row),
        pl.BlockSpec((1, tq, WIDTH), row),
        pl.BlockSpec((1, s // kb, kb, WIDTH), whole),
        pl.BlockSpec(memory_space=pl.ANY),
        pl.BlockSpec((N_META, WIDTH), const2),
        pl.BlockSpec((WIDTH, N_META), const2),
        pl.BlockSpec((1, tq, WIDTH), row),
        pl.BlockSpec((1, tq, WIDTH), row),
        pl.BlockSpec((1, V_DIM), const2),
        pl.BlockSpec((1, HEAD_DIM), const2),
        pl.BlockSpec((1, HEAD_DIM), const2),
        pl.BlockSpec((1, HEAD_DIM), const2),
        pl.BlockSpec((1, HEAD_DIM), const2),
        pl.BlockSpec(w_out_bf.shape, const2),
    ]
    return pl.pallas_call(
        functools.partial(_prompt_attn_kernel, tq=tq, kb=kb, big=big, nq=nq),
        grid=(nb, nq),
        in_specs=in_specs,
        out_specs=pl.BlockSpec((1, tq, d), row),
        out_shape=jax.ShapeDtypeStruct((nb, s, d), F32),
        scratch_shapes=[
            pltpu.VMEM((N_HEADS, 2 * tq, V_DIM), BF16),
            pltpu.VMEM((N_HEADS, 2 * tq), F32),
            pltpu.VMEM((N_HEADS, V7X_SUBLANES, 2 * tq), F32),
            pltpu.VMEM((N_HEADS, V_DIM, 2 * tq), F32),
            pltpu.VMEM((s // kb, WIDTH, kb), BF16),
            pltpu.SemaphoreType.DMA((nq,)),
        ],
        compiler_params=pltpu.CompilerParams(
            dimension_semantics=("arbitrary", "arbitrary"),
            vmem_limit_bytes=V7X_VMEM_LIMIT),
        name="prompt_attn",
    )(bound, x, q, kb4, vt4, km_bf, vmt_bf, sga, oc, subg, lq1, lk1, lq2, lk2, w_out_bf)


def _sample_attn_kernel(x_ref, q_ref, kn_ref, vn_ref, ck_ref, cv_ref, km_ref, vm_ref, sga_ref,
                        oc_ref, subg_ref, lq1_ref, lk1_ref, lq2_ref, lk2_ref, wout_ref, y_ref,
                        kcat, vcat, mix_ref, *, past, t_new, n_keys_pad, first):
    b = pl.program_id(0)
    nb = pl.num_programs(0)
    n_keys = past + N_META + t_new
    for h in range(N_HEADS):
        slab = slice(h * V_DIM, (h + 1) * V_DIM)
        kcat[0:past, slab] = ck_ref[0, pl.ds(h, past, stride=N_HEADS), :].astype(BF16)
        kcat[past + N_META:n_keys, slab] = kn_ref[0, pl.ds(h, t_new, stride=N_HEADS), :].astype(BF16)
        vcat[0:past, slab] = cv_ref[0, pl.ds(h, past, stride=N_HEADS), :].astype(BF16)
        vcat[past:past + N_META, slab] = vm_ref[pl.ds(h, N_META, stride=N_HEADS), :].astype(BF16)
        vcat[past + N_META:n_keys, slab] = vn_ref[0, pl.ds(h, t_new, stride=N_HEADS), :].astype(BF16)
    kcat[past:past + N_META, :] = km_ref[...]
    kcat[n_keys:n_keys_pad, :] = jnp.zeros((n_keys_pad - n_keys, WIDTH), BF16)
    vcat[n_keys:n_keys_pad, :] = jnp.zeros((n_keys_pad - n_keys, WIDTH), BF16)

    lane = lax.broadcasted_iota(jnp.int32, (1, V_DIM), 1)
    valid = lax.broadcasted_iota(jnp.int32, (1, n_keys_pad), 1) < n_keys
    lam = _lambda(lq1_ref, lk1_ref, lq2_ref, lk2_ref)
    o_heads = []
    for h in range(N_HEADS):
        qh = q_ref[0, :, h * V_DIM:(h + 1) * V_DIM]
        q2 = jnp.concatenate([jnp.where(lane < HEAD_DIM, qh, jnp.zeros_like(qh)),
                              jnp.where(lane >= HEAD_DIM, qh, jnp.zeros_like(qh))], axis=0)
        s = lax.dot_general(q2, kcat[:, h * V_DIM:(h + 1) * V_DIM], (((1,), (1,)), ((), ())),
                            preferred_element_type=F32)
        s = jnp.where(valid, s, NEG_BIG)
        e = jnp.exp2(s - jnp.max(s, axis=-1, keepdims=True))
        o2 = jnp.dot(e.astype(BF16), vcat[:, h * V_DIM:(h + 1) * V_DIM],
                     preferred_element_type=F32) / jnp.sum(e, axis=-1, keepdims=True)
        o_heads.append(o2[0:t_new] - lam * o2[t_new:2 * t_new])
    rows = pl.ds(pl.multiple_of(b * t_new, t_new), t_new)
    mix_ref[rows, :] = _gated_attn(o_heads, sga_ref[b + first], subg_ref[...])

    @pl.when(b == nb - 1)
    def _():
        n_rows = mix_ref.shape[0]
        oc = oc_ref[first:first + n_rows // t_new]
        base = _conv_branch_out(x_ref[...].reshape(n_rows, x_ref.shape[-1]),
                                oc.reshape(n_rows, WIDTH), wout_ref)
        y_ref[...] = _attn_branch_out(base, mix_ref[...], wout_ref).reshape(y_ref.shape)


def _sample_attn(x, q, kn, vn, ck, cv, km_bf, vm, sga, oc, subg, lq1, lk1, lq2, lk2, w_out_bf, *,
                 first):
    nb, t_new, d = x.shape
    n_proj = q.shape[0]
    past = ck.shape[1] // N_HEADS
    n_keys_pad = -(-(past + N_META + t_new) // V_DIM) * V_DIM
    row = lambda b: (b, 0, 0)
    proj_row = lambda b: (b + first, 0, 0)
    const2 = lambda b: (0, 0)
    const3 = lambda b: (0, 0, 0)
    in_specs = [
        pl.BlockSpec((nb, t_new, d), const3),
        pl.BlockSpec((1, t_new, WIDTH), proj_row),
        pl.BlockSpec((1, t_new * N_HEADS, V_DIM), proj_row),
        pl.BlockSpec((1, t_new * N_HEADS, V_DIM), proj_row),
        pl.BlockSpec((1, past * N_HEADS, V_DIM), row),
        pl.BlockSpec((1, past * N_HEADS, V_DIM), row),
        pl.BlockSpec((N_META, WIDTH), const2),
        pl.BlockSpec((N_META * N_HEADS, V_DIM), const2),
        pl.BlockSpec((n_proj, t_new, WIDTH), const3),
        pl.BlockSpec((n_proj, t_new, WIDTH), const3),
        pl.BlockSpec((1, V_DIM), const2),
        pl.BlockSpec((1, HEAD_DIM), const2),
        pl.BlockSpec((1, HEAD_DIM), const2),
        pl.BlockSpec((1, HEAD_DIM), const2),
        pl.BlockSpec((1, HEAD_DIM), const2),
        pl.BlockSpec(w_out_bf.shape, const2),
    ]
    return pl.pallas_call(
        functools.partial(_sample_attn_kernel, past=past, t_new=t_new, n_keys_pad=n_keys_pad,
                          first=first),
        grid=(nb,),
        in_specs=in_specs,
        out_specs=pl.BlockSpec((nb, t_new, d), const3),
        out_shape=jax.ShapeDtypeStruct((nb, t_new, d), F32),
        scratch_shapes=[pltpu.VMEM((n_keys_pad, WIDTH), BF16),
                        pltpu.VMEM((n_keys_pad, WIDTH), BF16),
                        pltpu.VMEM((nb * t_new, WIDTH), BF16)],
        compiler_params=pltpu.CompilerParams(
            dimension_semantics=("arbitrary",),
            vmem_limit_bytes=V7X_VMEM_LIMIT),
        name="sample_attn",
    )(x, q, kn, vn, ck, cv, km_bf, vm, sga, oc, subg, lq1, lk1, lq2, lk2, w_out_bf)


def _rope_tables(pos):
    half = ROT_DIM // 2
    inv = ROPE_THETA ** (-jnp.arange(0, ROT_DIM, 2, dtype=F32) / ROT_DIM)
    ang = pos.astype(F32)[:, None] * inv[None, :]
    cos, sin = lax.optimization_barrier((jnp.cos(ang), jnp.sin(ang)))
    t = pos.shape[0]
    rest = HEAD_DIM - 2 * half
    c64 = jnp.concatenate([cos, cos, jnp.ones((t, rest), F32)], axis=1)
    s64 = jnp.concatenate([-sin, sin, jnp.zeros((t, rest), F32)], axis=1)
    return jnp.tile(c64, (1, 2)), jnp.tile(s64, (1, 2))


def _score_bound(qg, kg):
    bound = Q_SCALE * HEAD_DIM * jnp.max(jnp.abs(qg)) * jnp.max(jnp.abs(kg)) * (1.0 + 2.0 ** -6)
    return bound.reshape(1).astype(F32)


def _pad_state(state):
    return jnp.pad(state, ((0, 0), (STATE_ROWS - (CONV_W - 1), 0), (0, 0)))


def _tile_sizes(seq):
    kb = KEY_BLOCK if seq % KEY_BLOCK == 0 else seq
    tq = QUERY_TILE if seq % QUERY_TILE == 0 else kb
    return (2 * tq if seq % (2 * tq) == 0 else tq), tq, kb


def kernel(x_prompt, x_sample, cache_k, cache_v, state_conv, meta_tokens, ln_g, w_in, q_norm_g,
           k_norm_g, lam_q1, lam_k1, lam_q2, lam_k2, subln_g, conv_w, conv_b, w_out):
    assert ln_g.shape[0] == 1, "single-layer stack"
    nb, seq, d = x_prompt.shape
    db, t_new, _ = x_sample.shape
    past = cache_k.shape[2]

    w_in_bf = w_in[0].astype(BF16)
    w_out_bf = w_out[0].astype(BF16)
    lng = ln_g[0][None, :]
    qg = jnp.tile(q_norm_g[0], 2 * N_HEADS)[None, :]
    kg = jnp.tile(k_norm_g[0], 2 * N_HEADS)[None, :]
    subg = subln_g[0][None, :]
    cw, cb = conv_w[0], conv_b[0][None, :]
    lams = [p[0][None, :] for p in (lam_q1, lam_k1, lam_q2, lam_k2)]
    grp = jnp.arange(GROUP_SUM_WIDTH, dtype=jnp.int32) // HEAD_DIM
    gmat = (grp[:, None] == grp[None, :]).astype(BF16)

    pos_m = jnp.arange(N_META, dtype=jnp.int32)
    pos_p = N_META + jnp.arange(seq, dtype=jnp.int32)
    pos_s = N_META + past + jnp.arange(t_new, dtype=jnp.int32)
    weights = (lng, w_in_bf, qg, kg, cw, cb, gmat)
    tm, tq, kb = _tile_sizes(seq)

    assert t_new == N_META, "meta tokens and new frames are projected as equal-length streams"
    cos_m, sin_m = _rope_tables(pos_m)
    cos_s, sin_s = _rope_tables(pos_s)
    n_small = 1 + db
    q_sm, k_sm, v_sm, kb_sm, sga_sm, oc_sm, tail_sm = _proj_call(
        jnp.concatenate([meta_tokens[None], x_sample], axis=0),
        jnp.concatenate([jnp.zeros((1, STATE_ROWS, WIDTH), F32), _pad_state(state_conv[0])], axis=0),
        jnp.concatenate([cos_m] + [cos_s] * db, axis=0),
        jnp.concatenate([sin_m] + [sin_s] * db, axis=0),
        *weights, sb=n_small, tm=t_new, tkv=t_new, emit_vt=False)
    km, vm, km_bf, tail_m = k_sm[0], v_sm[0], kb_sm[0], tail_sm[0:1]
    ks, vs, tail_s = k_sm[1:], v_sm[1:], tail_sm[1:]

    state_p = jnp.broadcast_to(tail_m, (nb, STATE_ROWS, WIDTH))
    qp, kp, vp, kp_bf, sga_p, oc_p, tail_p, vtp = _proj_call(
        x_prompt, state_p, *_rope_tables(pos_p), *weights, sb=1, tm=tm, tkv=kb, emit_vt=True,
        row_offset=N_META)
    y_prompt = _prompt_attn(
        _score_bound(q_norm_g[0], k_norm_g[0]), x_prompt, qp,
        kp_bf.reshape(nb, seq // kb, kb, WIDTH), vtp, km_bf,
        vm.reshape(N_META, WIDTH).T.astype(BF16),
        sga_p, oc_p, subg, *lams, w_out_bf, tq=tq, kb=kb, big=BIG_KEY_BLOCKS)
    kp = kp.at[:, :N_META * N_HEADS].set(jnp.broadcast_to(km[None], (nb,) + km.shape))
    vp = vp.at[:, :N_META * N_HEADS].set(jnp.broadcast_to(vm[None], (nb,) + vm.shape))

    y_sample = _sample_attn(
        x_sample, q_sm, k_sm, v_sm, cache_k[0].reshape(db, past * N_HEADS, V_DIM),
        cache_v[0].reshape(db, past * N_HEADS, V_DIM), km_bf, vm, sga_sm, oc_sm, subg, *lams,
        w_out_bf, first=1)

    def heads_form(a, rows):
        return a.reshape(1, a.shape[0], rows, N_HEADS, V_DIM)

    tail_rows = slice(STATE_ROWS - (CONV_W - 1), STATE_ROWS)
    return (y_prompt, y_sample, heads_form(kp, N_META + seq), heads_form(vp, N_META + seq),
            tail_p[None, :, tail_rows], heads_form(ks, t_new), heads_form(vs, t_new),
            tail_s[None, :, tail_rows])
```

```python
import functools
import math

import jax
import jax.numpy as jnp
from jax import lax
from jax.experimental import pallas as pl
from jax.experimental.pallas import tpu as pltpu

F32 = jnp.float32
BF16 = jnp.bfloat16

CHUNK = 64
N_META = 16
N_HEADS = 4
HEAD_DIM = 64
V_DIM = 2 * HEAD_DIM
WIDTH = N_HEADS * V_DIM
CONV_W = 3
ROT_DIM = HEAD_DIM // 4
ROPE_THETA = 500000.0
EPS = 1e-6
LAMBDA_INIT = 0.8 - 0.6 * math.exp(-0.3 * 0)
Q_SCALE = HEAD_DIM ** -0.5 * math.log2(math.e)
NEG_BIG = -1e30
SAFE_SCORE_BOUND = 40.0
V7X_SUBLANES = 8
V7X_MXU_WIDTH = 256
V7X_VMEM_LIMIT = 60 * 1024 * 1024

KEY_BLOCK = V7X_MXU_WIDTH
QUERY_TILE = 2 * KEY_BLOCK
BIG_KEY_BLOCKS = 12
GROUP_SUM_WIDTH = V7X_MXU_WIDTH
STATE_ROWS = V7X_SUBLANES


def _silu(x):
    return x / (1.0 + jnp.exp(-x))


def _proj_kernel(x_ref, st_ref, cos_ref, sin_ref, lng_ref, win_ref, qg_ref, kg_ref, cw_ref, cb_ref,
                 gmat_ref, q_ref, kf_ref, vf_ref, kb_ref, sga_ref, oc_ref, tail_ref, *rest,
                 sb, tm, tkv, emit_vt):
    if emit_vt:
        vt_ref, ubuf = rest
    else:
        (ubuf,) = rest
    j = pl.program_id(1)
    rows = sb * tm

    @pl.when(j == 0)
    def _():
        ubuf[:, 0:STATE_ROWS, :] = st_ref[...]

    @pl.when(j > 0)
    def _():
        ubuf[:, 0:STATE_ROWS, :] = ubuf[:, tm:tm + STATE_ROWS, :]

    x = x_ref[...].reshape(rows, x_ref.shape[-1])
    xg = (x * lng_ref[...]).astype(BF16)
    inv_rms = lax.rsqrt(jnp.mean(x * x, axis=-1, keepdims=True) + EPS)

    def proj(i):
        return inv_rms * jnp.dot(xg, win_ref[:, i * WIDTH:(i + 1) * WIDTH],
                                 preferred_element_type=F32)

    def group_mean_sq(t):
        tt = (t * t).astype(BF16)
        halves = [jnp.dot(tt[:, i * GROUP_SUM_WIDTH:(i + 1) * GROUP_SUM_WIDTH], gmat_ref[...],
                          preferred_element_type=F32) for i in range(WIDTH // GROUP_SUM_WIDTH)]
        return jnp.concatenate(halves, axis=1) * (1.0 / HEAD_DIM)

    cos = cos_ref[...]
    sin = sin_ref[...]
    lane = lax.broadcasted_iota(jnp.int32, (1, V_DIM), 1) % HEAD_DIM
    take_lower = lane >= ROT_DIM // 2

    def norm_rope(t, mean_sq, g):
        tn = t * lax.rsqrt(mean_sq + EPS) * g
        outs = []
        for h in range(N_HEADS):
            th = tn[:, h * V_DIM:(h + 1) * V_DIM]
            lower = pltpu.roll(th, ROT_DIM // 2, axis=1)
            upper = pltpu.roll(th, V_DIM - ROT_DIM // 2, axis=1)
            outs.append(th * cos + jnp.where(take_lower, lower, upper) * sin)
        return jnp.concatenate(outs, axis=1)

    def per_stream(a):
        return a.reshape(sb, tm, a.shape[-1])

    pq = proj(0)
    pk = proj(1)
    msq = group_mean_sq(pq)
    v = proj(2)
    msk = group_mean_sq(pk)
    ga = proj(3)
    q_ref[...] = per_stream((norm_rope(pq, msq, qg_ref[...]) * Q_SCALE).astype(BF16))
    bb = proj(4)
    k = norm_rope(pk, msk, kg_ref[...])
    kb_ref[...] = per_stream(k.astype(BF16))
    cc = proj(5)
    for b in range(sb):
        for h in range(N_HEADS):
            src = (slice(b * tm, (b + 1) * tm), slice(h * V_DIM, (h + 1) * V_DIM))
            kf_ref[b, pl.ds(h, tm, stride=N_HEADS), :] = k[src]
            vf_ref[b, pl.ds(h, tm, stride=N_HEADS), :] = v[src]
    hh = proj(6)
    if emit_vt:
        for s in range(tm // tkv):
            vt_ref[0, s] = v[s * tkv:(s + 1) * tkv, :].T.astype(BF16)
    gc = proj(7)
    sga_ref[...] = per_stream(_silu(ga).astype(BF16))

    u = cc * hh
    ys = []
    for b in range(sb):
        ubuf[b, STATE_ROWS:STATE_ROWS + tm, :] = u[b * tm:(b + 1) * tm, :]
        y = cb_ref[...]
        for tap in range(CONV_W):
            off = STATE_ROWS - (CONV_W - 1) + tap
            y = y + cw_ref[tap:tap + 1, :] * ubuf[b, off:off + tm, :]
        ys.append(y)
    y = ys[0] if sb == 1 else jnp.concatenate(ys, axis=0)
    oc_ref[...] = per_stream((_silu(gc) * (bb * y)).astype(BF16))
    tail_ref[...] = ubuf[:, tm:tm + STATE_ROWS, :]


def _proj_call(x, state8, cos_t, sin_t, ln_g, w_in_bf, qg, kg, conv_w, conv_b, gmat, *, sb, tm, tkv,
               emit_vt, row_offset=0):
    nb, t, d = x.shape
    nt = t // tm
    assert t % tm == 0 and tm % tkv == 0 and nb % sb == 0 and not (emit_vt and sb > 1)
    assert cos_t.shape[0] == sb * t
    row = lambda b, j: (b, j, 0)
    const2 = lambda b, j: (0, 0)
    in_specs = [
        pl.BlockSpec((sb, tm, d), row),
        pl.BlockSpec((sb, STATE_ROWS, WIDTH), lambda b, j: (b, 0, 0)),
        pl.BlockSpec((sb * tm, V_DIM), lambda b, j: (j, 0)),
        pl.BlockSpec((sb * tm, V_DIM), lambda b, j: (j, 0)),
        pl.BlockSpec((1, d), const2),
        pl.BlockSpec(w_in_bf.shape, const2, pipeline_mode=pl.Buffered(1)),
        pl.BlockSpec((1, WIDTH), const2),
        pl.BlockSpec((1, WIDTH), const2),
        pl.BlockSpec((CONV_W, WIDTH), const2),
        pl.BlockSpec((1, WIDTH), const2),
        pl.BlockSpec((GROUP_SUM_WIDTH, GROUP_SUM_WIDTH), const2),
    ]
    tile = pl.BlockSpec((sb, tm, WIDTH), row)
    heads_tile = pl.BlockSpec(
        (pl.Element(sb), pl.Element(tm * N_HEADS), pl.Element(V_DIM)),
        lambda b, j: (b * sb, pl.multiple_of((row_offset + j * tm) * N_HEADS, V7X_SUBLANES), 0))
    out_shape = [
        jax.ShapeDtypeStruct((nb, t, WIDTH), BF16),
        jax.ShapeDtypeStruct((nb, (row_offset + t) * N_HEADS, V_DIM), F32),
        jax.ShapeDtypeStruct((nb, (row_offset + t) * N_HEADS, V_DIM), F32),
        jax.ShapeDtypeStruct((nb, t, WIDTH), BF16),
        jax.ShapeDtypeStruct((nb, t, WIDTH), BF16),
        jax.ShapeDtypeStruct((nb, t, WIDTH), BF16),
        jax.ShapeDtypeStruct((nb, STATE_ROWS, WIDTH), F32),
    ]
    out_specs = [tile, heads_tile, heads_tile, tile, tile, tile,
                 pl.BlockSpec((sb, STATE_ROWS, WIDTH), lambda b, j: (b, 0, 0))]
    if emit_vt:
        out_shape.append(jax.ShapeDtypeStruct((nb, t // tkv, WIDTH, tkv), BF16))
        out_specs.append(pl.BlockSpec((1, tm // tkv, WIDTH, tkv), lambda b, j: (b, j, 0, 0)))
    return pl.pallas_call(
        functools.partial(_proj_kernel, sb=sb, tm=tm, tkv=tkv, emit_vt=emit_vt),
        grid=(nb // sb, nt),
        in_specs=in_specs,
        out_specs=out_specs,
        out_shape=out_shape,
        scratch_shapes=[pltpu.VMEM((sb, tm + STATE_ROWS, WIDTH), F32)],
        compiler_params=pltpu.CompilerParams(
            dimension_semantics=("arbitrary", "arbitrary"),
            vmem_limit_bytes=V7X_VMEM_LIMIT),
        name="proj",
    )(x, state8, cos_t, sin_t, ln_g, w_in_bf, qg, kg, conv_w, conv_b, gmat)


def _lambda(lq1_ref, lk1_ref, lq2_ref, lk2_ref):
    a = jnp.sum(lq1_ref[...] * lk1_ref[...], axis=-1, keepdims=True)
    b = jnp.sum(lq2_ref[...] * lk2_ref[...], axis=-1, keepdims=True)
    return jnp.exp(a) - jnp.exp(b) + LAMBDA_INIT


def _conv_branch_out(x, oc, wout_ref):
    return x + jnp.dot(oc, wout_ref[WIDTH:, :], preferred_element_type=F32)


def _gated_attn(o_heads, sga, subg):
    parts = []
    for h, o in enumerate(o_heads):
        ms = jnp.mean(o * o, axis=-1, keepdims=True)
        on = o * lax.rsqrt(ms + EPS) * subg * (1.0 - LAMBDA_INIT)
        parts.append((sga[:, h * V_DIM:(h + 1) * V_DIM].astype(F32) * on).astype(BF16))
    return jnp.concatenate(parts, axis=1)


def _attn_branch_out(base, mix, wout_ref):
    return base + jnp.dot(mix, wout_ref[:WIDTH, :], preferred_element_type=F32)


def _prompt_attn_kernel(bound_ref, x_ref, q_ref, kb_hbm, vt_hbm, km_ref, vmt_ref, sga_ref, oc_ref,
                        subg_ref, lq1_ref, lk1_ref, lq2_ref, lk2_ref, wout_ref, y_ref,
                        qz_ref, m_ref, l_ref, acc_ref, kb_buf, vt_buf, kv_sem, *, tq, kb, big, nq):
    j = pl.program_id(1)
    lanes = 2 * tq
    n_diag = tq // kb
    n_full = j * n_diag

    def chunk_copies(c):
        blocks = pl.ds(c * n_diag, n_diag)
        stream = pl.program_id(0)
        return (pltpu.make_async_copy(kb_hbm.at[stream, blocks], kb_buf.at[blocks], kv_sem.at[0, c]),
                pltpu.make_async_copy(vt_hbm.at[stream, blocks], vt_buf.at[blocks], kv_sem.at[1, c]))

    @pl.when(j == 0)
    def _():
        for c in range(nq):
            for copy in chunk_copies(c):
                copy.start()

    for copy in chunk_copies(j):
        copy.wait()

    lane = lax.broadcasted_iota(jnp.int32, (1, V_DIM), 1)
    for h in range(N_HEADS):
        qh = q_ref[0, :, h * V_DIM:(h + 1) * V_DIM]
        qz_ref[h, 0:tq, :] = jnp.where(lane < HEAD_DIM, qh, jnp.zeros_like(qh))
        qz_ref[h, tq:lanes, :] = jnp.where(lane >= HEAD_DIM, qh, jnp.zeros_like(qh))

    def scores(k_h, h):
        return lax.dot_general(k_h, qz_ref[h], (((1,), (1,)), ((), ())), preferred_element_type=F32)

    def k_slab(k, h):
        return k[:, h * V_DIM:(h + 1) * V_DIM]

    def sublane_partial(e):
        return jnp.sum(e.reshape(e.shape[0] // V7X_SUBLANES, V7X_SUBLANES, lanes), axis=0)

    kchunk = lax.broadcasted_iota(jnp.int32, (kb, 1), 0) // CHUNK
    qchunk = (lax.broadcasted_iota(jnp.int32, (1, lanes), 1) % tq) // CHUNK
    diag_masks = [kchunk + d * (kb // CHUNK) <= qchunk for d in range(n_diag)]

    def late_queries(a, d, axis):
        lo = d * kb
        if axis == 0:
            return jnp.concatenate([a[lo:tq], a[tq + lo:lanes]], axis=0)
        return jnp.concatenate([a[:, lo:tq], a[:, tq + lo:lanes]], axis=1)

    def add_late(full, part, d):
        lo, w = d * kb, tq - d * kb
        return jnp.concatenate([full[:, 0:lo], full[:, lo:tq] + part[:, 0:w],
                                full[:, tq:tq + lo], full[:, tq + lo:lanes] + part[:, w:2 * w]],
                               axis=1)

    @pl.when(bound_ref[0] <= SAFE_SCORE_BOUND)
    def _():
        def key_pass(k, pieces, assign, late_blocks=()):
            s_next = scores(k_slab(k, 0), 0)
            for h in range(N_HEADS):
                s = s_next
                if h + 1 < N_HEADS:
                    s_next = scores(k_slab(k, h + 1), h + 1)
                s_late = [lax.dot_general(k_slab(kl, h), late_queries(qz_ref[h], d, 0),
                                          (((1,), (1,)), ((), ())), preferred_element_type=F32)
                          for d, kl, _ in late_blocks]
                e = jnp.exp2(s)
                l_new = acc_new = None
                for row0, rows, vt, mask in pieces:
                    ep = e[row0:row0 + rows, :]
                    if mask is not None:
                        ep = jnp.where(mask, ep, 0.0)
                    lp = sublane_partial(ep)
                    ap = jnp.dot(vt[h * V_DIM:(h + 1) * V_DIM, :], ep.astype(BF16),
                                 preferred_element_type=F32)
                    l_new = lp if l_new is None else l_new + lp
                    acc_new = ap if acc_new is None else acc_new + ap
                for (d, _, vt), sl in zip(late_blocks, s_late):
                    ep = jnp.where(late_queries(diag_masks[d], d, 1), jnp.exp2(sl), 0.0)
                    lp = jnp.sum(ep.reshape(kb // V7X_SUBLANES, V7X_SUBLANES, ep.shape[1]), axis=0)
                    ap = jnp.dot(vt[h * V_DIM:(h + 1) * V_DIM, :], ep.astype(BF16),
                                 preferred_element_type=F32)
                    l_new = add_late(l_new, lp, d)
                    acc_new = add_late(acc_new, ap, d)
                if assign:
                    l_ref[h] = l_new
                    acc_ref[h] = acc_new
                else:
                    l_ref[h] += l_new
                    acc_ref[h] += acc_new

        n_big = n_full // big
        rem = n_full - n_big * big
        for r in range(0, big, n_diag):
            @pl.when(rem == r)
            def _():
                first = n_full - r
                n_blocks = r + 1
                kblocks = kb_buf[pl.ds(first, n_blocks)].reshape(n_blocks * kb, WIDTH)
                k = jnp.concatenate([km_ref[...], kblocks], axis=0)
                pieces = [(0, N_META, vmt_ref[...], None)]
                pieces += [(N_META + t * kb, kb, vt_buf[first + t],
                            None if t < r else diag_masks[0]) for t in range(n_blocks)]
                late = [(d, kb_buf[n_full + d], vt_buf[n_full + d])
                        for d in range(1, n_diag)]
                key_pass(k, pieces, assign=True, late_blocks=late)

        def big_block(i, carry):
            first = i * big
            k = kb_buf[pl.ds(first, big)].reshape(big * kb, WIDTH)
            key_pass(k, [(t * kb, kb, vt_buf[first + t], None) for t in range(big)], assign=False)
            return carry

        lax.fori_loop(0, n_big, big_block, 0)

    @pl.when(jnp.logical_not(bound_ref[0] <= SAFE_SCORE_BOUND))
    def _():
        for h in range(N_HEADS):
            s = scores(k_slab(km_ref, h), h)
            m = jnp.max(s, axis=0, keepdims=True)
            e = jnp.exp2(s - m)
            m_ref[h:h + 1, :] = m
            l_ref[h] = jnp.zeros((V7X_SUBLANES, lanes), F32)
            l_ref[h, 0:1, :] = jnp.sum(e, axis=0, keepdims=True)
            acc_ref[h] = jnp.dot(vmt_ref[h * V_DIM:(h + 1) * V_DIM, :], e.astype(BF16),
                                 preferred_element_type=F32)

        def block(i, mask):
            k = kb_buf[i]
            vt = vt_buf[i]
            for h in range(N_HEADS):
                s = scores(k_slab(k, h), h)
                if mask is not None:
                    s = jnp.where(mask, s, NEG_BIG)
                m_old = m_ref[h:h + 1, :]
                m_new = jnp.maximum(m_old, jnp.max(s, axis=0, keepdims=True))
                alpha = jnp.exp2(m_old - m_new)
                e = jnp.exp2(s - m_new)
                l_ref[h, 0:1, :] = alpha * l_ref[h, 0:1, :] + jnp.sum(e, axis=0, keepdims=True)
                acc_ref[h] = alpha * acc_ref[h] + jnp.dot(
                    vt[h * V_DIM:(h + 1) * V_DIM, :], e.astype(BF16), preferred_element_type=F32)
                m_ref[h:h + 1, :] = m_new

        def full_block(i, carry):
            block(i, None)
            return carry

        lax.fori_loop(0, n_full, full_block, 0)
        for d in range(n_diag):
            block(n_full + d, diag_masks[d])

    base = _conv_branch_out(x_ref[0], oc_ref[0], wout_ref)
    lam = _lambda(lq1_ref, lk1_ref, lq2_ref, lk2_ref)
    o_heads = []
    for h in range(N_HEADS):
        o2 = acc_ref[h] / jnp.sum(l_ref[h], axis=0, keepdims=True)
        o_heads.append((o2[:, 0:tq] - lam * o2[:, tq:lanes]).T)
    y_ref[0] = _attn_branch_out(base, _gated_attn(o_heads, sga_ref[0], subg_ref[...]), wout_ref)


def _prompt_attn(bound, x, q, kb4, vt4, km_bf, vmt_bf, sga, oc, subg, lq1, lk1, lq2, lk2, w_out_bf,
                 *, tq, kb, big):
    nb, s, d = x.shape
    nq = s // tq
    assert tq % kb == 0 and big % (tq // kb) == 0
    row = lambda b, j: (b, j, 0)
    const2 = lambda b, j: (0, 0)
    in_specs = [
        pl.BlockSpec(memory_space=pltpu.SMEM),
        pl.BlockSpec((1, tq, d), row),
        pl.BlockSpec((1, tq, WIDTH), row),
        pl.BlockSpec(memory_space=pl.ANY),
        pl.BlockSpec(memory_space=pl.ANY),
        pl.BlockSpec((N_META, WIDTH), const2),
        pl.BlockSpec((WIDTH, N_META), const2),
        pl.BlockSpec((1, tq, WIDTH), row),
        pl.BlockSpec((1, tq, WIDTH), row),
        pl.BlockSpec((1, V_DIM), const2),
        pl.BlockSpec((1, HEAD_DIM), const2),
        pl.BlockSpec((1, HEAD_DIM), const2),
        pl.BlockSpec((1, HEAD_DIM), const2),
        pl.BlockSpec((1, HEAD_DIM), const2),
        pl.BlockSpec(w_out_bf.shape, const2),
    ]
    return pl.pallas_call(
        functools.partial(_prompt_attn_kernel, tq=tq, kb=kb, big=big, nq=nq),
        grid=(nb, nq),
        in_specs=in_specs,
        out_specs=pl.BlockSpec((1, tq, d), row),
        out_shape=jax.ShapeDtypeStruct((nb, s, d), F32),
        scratch_shapes=[
            pltpu.VMEM((N_HEADS, 2 * tq, V_DIM), BF16),
            pltpu.VMEM((N_HEADS, 2 * tq), F32),
            pltpu.VMEM((N_HEADS, V7X_SUBLANES, 2 * tq), F32),
            pltpu.VMEM((N_HEADS, V_DIM, 2 * tq), F32),
            pltpu.VMEM((s // kb, kb, WIDTH), BF16),
            pltpu.VMEM((s // kb, WIDTH, kb), BF16),
            pltpu.SemaphoreType.DMA((2, nq)),
        ],
        compiler_params=pltpu.CompilerParams(
            dimension_semantics=("arbitrary", "arbitrary"),
            vmem_limit_bytes=V7X_VMEM_LIMIT),
        name="prompt_attn",
    )(bound, x, q, kb4, vt4, km_bf, vmt_bf, sga, oc, subg, lq1, lk1, lq2, lk2, w_out_bf)


def _sample_attn_kernel(x_ref, q_ref, kn_ref, vn_ref, ck_ref, cv_ref, km_ref, vm_ref, sga_ref,
                        oc_ref, subg_ref, lq1_ref, lk1_ref, lq2_ref, lk2_ref, wout_ref, y_ref,
                        kcat, vcat, mix_ref, *, past, t_new, n_keys_pad, first):
    b = pl.program_id(0)
    nb = pl.num_programs(0)
    n_keys = past + N_META + t_new
    for h in range(N_HEADS):
        slab = slice(h * V_DIM, (h + 1) * V_DIM)
        kcat[0:past, slab] = ck_ref[0, pl.ds(h, past, stride=N_HEADS), :].astype(BF16)
        kcat[past + N_META:n_keys, slab] = kn_ref[0, pl.ds(h, t_new, stride=N_HEADS), :].astype(BF16)
        vcat[0:past, slab] = cv_ref[0, pl.ds(h, past, stride=N_HEADS), :].astype(BF16)
        vcat[past:past + N_META, slab] = vm_ref[pl.ds(h, N_META, stride=N_HEADS), :].astype(BF16)
        vcat[past + N_META:n_keys, slab] = vn_ref[0, pl.ds(h, t_new, stride=N_HEADS), :].astype(BF16)
    kcat[past:past + N_META, :] = km_ref[...]
    kcat[n_keys:n_keys_pad, :] = jnp.zeros((n_keys_pad - n_keys, WIDTH), BF16)
    vcat[n_keys:n_keys_pad, :] = jnp.zeros((n_keys_pad - n_keys, WIDTH), BF16)

    lane = lax.broadcasted_iota(jnp.int32, (1, V_DIM), 1)
    valid = lax.broadcasted_iota(jnp.int32, (1, n_keys_pad), 1) < n_keys
    lam = _lambda(lq1_ref, lk1_ref, lq2_ref, lk2_ref)
    o_heads = []
    for h in range(N_HEADS):
        qh = q_ref[0, :, h * V_DIM:(h + 1) * V_DIM]
        q2 = jnp.concatenate([jnp.where(lane < HEAD_DIM, qh, jnp.zeros_like(qh)),
                              jnp.where(lane >= HEAD_DIM, qh, jnp.zeros_like(qh))], axis=0)
        s = lax.dot_general(q2, kcat[:, h * V_DIM:(h + 1) * V_DIM], (((1,), (1,)), ((), ())),
                            preferred_element_type=F32)
        s = jnp.where(valid, s, NEG_BIG)
        e = jnp.exp2(s - jnp.max(s, axis=-1, keepdims=True))
        o2 = jnp.dot(e.astype(BF16), vcat[:, h * V_DIM:(h + 1) * V_DIM],
                     preferred_element_type=F32) / jnp.sum(e, axis=-1, keepdims=True)
        o_heads.append(o2[0:t_new] - lam * o2[t_new:2 * t_new])
    rows = pl.ds(pl.multiple_of(b * t_new, t_new), t_new)
    mix_ref[rows, :] = _gated_attn(o_heads, sga_ref[b + first], subg_ref[...])

    @pl.when(b == nb - 1)
    def _():
        n_rows = mix_ref.shape[0]
        oc = oc_ref[first:first + n_rows // t_new]
        base = _conv_branch_out(x_ref[...].reshape(n_rows, x_ref.shape[-1]),
                                oc.reshape(n_rows, WIDTH), wout_ref)
        y_ref[...] = _attn_branch_out(base, mix_ref[...], wout_ref).reshape(y_ref.shape)


def _sample_attn(x, q, kn, vn, ck, cv, km_bf, vm, sga, oc, subg, lq1, lk1, lq2, lk2, w_out_bf, *,
                 first):
    nb, t_new, d = x.shape
    n_proj = q.shape[0]
    past = ck.shape[1] // N_HEADS
    n_keys_pad = -(-(past + N_META + t_new) // V_DIM) * V_DIM
    row = lambda b: (b, 0, 0)
    proj_row = lambda b: (b + first, 0, 0)
    const2 = lambda b: (0, 0)
    const3 = lambda b: (0, 0, 0)
    in_specs = [
        pl.BlockSpec((nb, t_new, d), const3),
        pl.BlockSpec((1, t_new, WIDTH), proj_row),
        pl.BlockSpec((1, t_new * N_HEADS, V_DIM), proj_row),
        pl.BlockSpec((1, t_new * N_HEADS, V_DIM), proj_row),
        pl.BlockSpec((1, past * N_HEADS, V_DIM), row),
        pl.BlockSpec((1, past * N_HEADS, V_DIM), row),
        pl.BlockSpec((N_META, WIDTH), const2),
        pl.BlockSpec((N_META * N_HEADS, V_DIM), const2),
        pl.BlockSpec((n_proj, t_new, WIDTH), const3),
        pl.BlockSpec((n_proj, t_new, WIDTH), const3),
        pl.BlockSpec((1, V_DIM), const2),
        pl.BlockSpec((1, HEAD_DIM), const2),
        pl.BlockSpec((1, HEAD_DIM), const2),
        pl.BlockSpec((1, HEAD_DIM), const2),
        pl.BlockSpec((1, HEAD_DIM), const2),
        pl.BlockSpec(w_out_bf.shape, const2),
    ]
    return pl.pallas_call(
        functools.partial(_sample_attn_kernel, past=past, t_new=t_new, n_keys_pad=n_keys_pad,
                          first=first),
        grid=(nb,),
        in_specs=in_specs,
        out_specs=pl.BlockSpec((nb, t_new, d), const3),
        out_shape=jax.ShapeDtypeStruct((nb, t_new, d), F32),
        scratch_shapes=[pltpu.VMEM((n_keys_pad, WIDTH), BF16),
                        pltpu.VMEM((n_keys_pad, WIDTH), BF16),
                        pltpu.VMEM((nb * t_new, WIDTH), BF16)],
        compiler_params=pltpu.CompilerParams(
            dimension_semantics=("arbitrary",),
            vmem_limit_bytes=V7X_VMEM_LIMIT),
        name="sample_attn",
    )(x, q, kn, vn, ck, cv, km_bf, vm, sga, oc, subg, lq1, lk1, lq2, lk2, w_out_bf)


def _rope_tables(pos):
    half = ROT_DIM // 2
    inv = ROPE_THETA ** (-jnp.arange(0, ROT_DIM, 2, dtype=F32) / ROT_DIM)
    ang = pos.astype(F32)[:, None] * inv[None, :]
    cos, sin = lax.optimization_barrier((jnp.cos(ang), jnp.sin(ang)))
    t = pos.shape[0]
    rest = HEAD_DIM - 2 * half
    c64 = jnp.concatenate([cos, cos, jnp.ones((t, rest), F32)], axis=1)
    s64 = jnp.concatenate([-sin, sin, jnp.zeros((t, rest), F32)], axis=1)
    return jnp.tile(c64, (1, 2)), jnp.tile(s64, (1, 2))


def _score_bound(qg, kg):
    bound = Q_SCALE * HEAD_DIM * jnp.max(jnp.abs(qg)) * jnp.max(jnp.abs(kg)) * (1.0 + 2.0 ** -6)
    return bound.reshape(1).astype(F32)


def _pad_state(state):
    return jnp.pad(state, ((0, 0), (STATE_ROWS - (CONV_W - 1), 0), (0, 0)))


def _tile_sizes(seq):
    kb = KEY_BLOCK if seq % KEY_BLOCK == 0 else seq
    tq = QUERY_TILE if seq % QUERY_TILE == 0 else kb
    return (2 * tq if seq % (2 * tq) == 0 else tq), tq, kb


def kernel(x_prompt, x_sample, cache_k, cache_v, state_conv, meta_tokens, ln_g, w_in, q_norm_g,
           k_norm_g, lam_q1, lam_k1, lam_q2, lam_k2, subln_g, conv_w, conv_b, w_out):
    assert ln_g.shape[0] == 1, "single-layer stack"
    nb, seq, d = x_prompt.shape
    db, t_new, _ = x_sample.shape
    past = cache_k.shape[2]

    w_in_bf = w_in[0].astype(BF16)
    w_out_bf = w_out[0].astype(BF16)
    lng = ln_g[0][None, :]
    qg = jnp.tile(q_norm_g[0], 2 * N_HEADS)[None, :]
    kg = jnp.tile(k_norm_g[0], 2 * N_HEADS)[None, :]
    subg = subln_g[0][None, :]
    cw, cb = conv_w[0], conv_b[0][None, :]
    lams = [p[0][None, :] for p in (lam_q1, lam_k1, lam_q2, lam_k2)]
    grp = jnp.arange(GROUP_SUM_WIDTH, dtype=jnp.int32) // HEAD_DIM
    gmat = (grp[:, None] == grp[None, :]).astype(BF16)

    pos_m = jnp.arange(N_META, dtype=jnp.int32)
    pos_p = N_META + jnp.arange(seq, dtype=jnp.int32)
    pos_s = N_META + past + jnp.arange(t_new, dtype=jnp.int32)
    weights = (lng, w_in_bf, qg, kg, cw, cb, gmat)
    tm, tq, kb = _tile_sizes(seq)

    assert t_new == N_META, "meta tokens and new frames are projected as equal-length streams"
    cos_m, sin_m = _rope_tables(pos_m)
    cos_s, sin_s = _rope_tables(pos_s)
    n_small = 1 + db
    q_sm, k_sm, v_sm, kb_sm, sga_sm, oc_sm, tail_sm = _proj_call(
        jnp.concatenate([meta_tokens[None], x_sample], axis=0),
        jnp.concatenate([jnp.zeros((1, STATE_ROWS, WIDTH), F32), _pad_state(state_conv[0])], axis=0),
        jnp.concatenate([cos_m] + [cos_s] * db, axis=0),
        jnp.concatenate([sin_m] + [sin_s] * db, axis=0),
        *weights, sb=n_small, tm=t_new, tkv=t_new, emit_vt=False)
    km, vm, km_bf, tail_m = k_sm[0], v_sm[0], kb_sm[0], tail_sm[0:1]
    ks, vs, tail_s = k_sm[1:], v_sm[1:], tail_sm[1:]

    state_p = jnp.broadcast_to(tail_m, (nb, STATE_ROWS, WIDTH))
    qp, kp, vp, kp_bf, sga_p, oc_p, tail_p, vtp = _proj_call(
        x_prompt, state_p, *_rope_tables(pos_p), *weights, sb=1, tm=tm, tkv=kb, emit_vt=True,
        row_offset=N_META)
    y_prompt = _prompt_attn(
        _score_bound(q_norm_g[0], k_norm_g[0]), x_prompt, qp,
        kp_bf.reshape(nb, seq // kb, kb, WIDTH), vtp, km_bf,
        vm.reshape(N_META, WIDTH).T.astype(BF16),
        sga_p, oc_p, subg, *lams, w_out_bf, tq=tq, kb=kb, big=BIG_KEY_BLOCKS)
    kp = kp.at[:, :N_META * N_HEADS].set(jnp.broadcast_to(km[None], (nb,) + km.shape))
    vp = vp.at[:, :N_META * N_HEADS].set(jnp.broadcast_to(vm[None], (nb,) + vm.shape))

    y_sample = _sample_attn(
        x_sample, q_sm, k_sm, v_sm, cache_k[0].reshape(db, past * N_HEADS, V_DIM),
        cache_v[0].reshape(db, past * N_HEADS, V_DIM), km_bf, vm, sga_sm, oc_sm, subg, *lams,
        w_out_bf, first=1)

    def heads_form(a, rows):
        return a.reshape(1, a.shape[0], rows, N_HEADS, V_DIM)

    tail_rows = slice(STATE_ROWS - (CONV_W - 1), STATE_ROWS)
    return (y_prompt, y_sample, heads_form(kp, N_META + seq), heads_form(vp, N_META + seq),
            tail_p[None, :, tail_rows], heads_form(ks, t_new), heads_form(vs, t_new),
            tail_s[None, :, tail_rows])
```

```python
import functools
import math

import jax
import jax.numpy as jnp
from jax import lax
from jax.experimental import pallas as pl
from jax.experimental.pallas import tpu as pltpu

F32 = jnp.float32
BF16 = jnp.bfloat16

CHUNK = 64
N_META = 16
N_HEADS = 4
HEAD_DIM = 64
V_DIM = 2 * HEAD_DIM
WIDTH = N_HEADS * V_DIM
CONV_W = 3
ROT_DIM = HEAD_DIM // 4
ROPE_THETA = 500000.0
EPS = 1e-6
LAMBDA_INIT = 0.8 - 0.6 * math.exp(-0.3 * 0)
Q_SCALE = HEAD_DIM ** -0.5 * math.log2(math.e)
NEG_BIG = -1e30
SAFE_SCORE_BOUND = 40.0
V7X_SUBLANES = 8
V7X_MXU_WIDTH = 256
V7X_VMEM_LIMIT = 60 * 1024 * 1024

KEY_BLOCK = V7X_MXU_WIDTH
QUERY_TILE = 2 * KEY_BLOCK
BIG_KEY_BLOCKS = 8
GROUP_SUM_WIDTH = V7X_MXU_WIDTH
STATE_ROWS = V7X_SUBLANES


def _silu(x):
    return x / (1.0 + jnp.exp(-x))


def _proj_kernel(x_ref, st_ref, cos_ref, sin_ref, lng_ref, win_ref, qg_ref, kg_ref, cw_ref, cb_ref,
                 gmat_ref, q_ref, kf_ref, vf_ref, kb_ref, sga_ref, oc_ref, tail_ref, *rest,
                 sb, tm, tkv, emit_vt):
    if emit_vt:
        vt_ref, ubuf = rest
    else:
        (ubuf,) = rest
    j = pl.program_id(1)
    rows = sb * tm

    @pl.when(j == 0)
    def _():
        ubuf[:, 0:STATE_ROWS, :] = st_ref[...]

    @pl.when(j > 0)
    def _():
        ubuf[:, 0:STATE_ROWS, :] = ubuf[:, tm:tm + STATE_ROWS, :]

    x = x_ref[...].reshape(rows, x_ref.shape[-1])
    xg = (x * lng_ref[...]).astype(BF16)
    inv_rms = lax.rsqrt(jnp.mean(x * x, axis=-1, keepdims=True) + EPS)

    def proj(i):
        return inv_rms * jnp.dot(xg, win_ref[:, i * WIDTH:(i + 1) * WIDTH],
                                 preferred_element_type=F32)

    def group_mean_sq(t):
        tt = (t * t).astype(BF16)
        halves = [jnp.dot(tt[:, i * GROUP_SUM_WIDTH:(i + 1) * GROUP_SUM_WIDTH], gmat_ref[...],
                          preferred_element_type=F32) for i in range(WIDTH // GROUP_SUM_WIDTH)]
        return jnp.concatenate(halves, axis=1) * (1.0 / HEAD_DIM)

    cos = cos_ref[...]
    sin = sin_ref[...]
    lane = lax.broadcasted_iota(jnp.int32, (1, V_DIM), 1) % HEAD_DIM
    take_lower = lane >= ROT_DIM // 2

    def norm_rope(t, mean_sq, g):
        tn = t * lax.rsqrt(mean_sq + EPS) * g
        outs = []
        for h in range(N_HEADS):
            th = tn[:, h * V_DIM:(h + 1) * V_DIM]
            lower = pltpu.roll(th, ROT_DIM // 2, axis=1)
            upper = pltpu.roll(th, V_DIM - ROT_DIM // 2, axis=1)
            outs.append(th * cos + jnp.where(take_lower, lower, upper) * sin)
        return jnp.concatenate(outs, axis=1)

    def per_stream(a):
        return a.reshape(sb, tm, a.shape[-1])

    pq = proj(0)
    pk = proj(1)
    msq = group_mean_sq(pq)
    v = proj(2)
    msk = group_mean_sq(pk)
    ga = proj(3)
    q_ref[...] = per_stream((norm_rope(pq, msq, qg_ref[...]) * Q_SCALE).astype(BF16))
    bb = proj(4)
    k = norm_rope(pk, msk, kg_ref[...])
    kb_ref[...] = per_stream(k.astype(BF16))
    cc = proj(5)
    for b in range(sb):
        for h in range(N_HEADS):
            src = (slice(b * tm, (b + 1) * tm), slice(h * V_DIM, (h + 1) * V_DIM))
            kf_ref[b, pl.ds(h, tm, stride=N_HEADS), :] = k[src]
            vf_ref[b, pl.ds(h, tm, stride=N_HEADS), :] = v[src]
    hh = proj(6)
    if emit_vt:
        for s in range(tm // tkv):
            vt_ref[0, s] = v[s * tkv:(s + 1) * tkv, :].T.astype(BF16)
    gc = proj(7)
    sga_ref[...] = per_stream(_silu(ga).astype(BF16))

    u = cc * hh
    ys = []
    for b in range(sb):
        ubuf[b, STATE_ROWS:STATE_ROWS + tm, :] = u[b * tm:(b + 1) * tm, :]
        y = cb_ref[...]
        for tap in range(CONV_W):
            off = STATE_ROWS - (CONV_W - 1) + tap
            y = y + cw_ref[tap:tap + 1, :] * ubuf[b, off:off + tm, :]
        ys.append(y)
    y = ys[0] if sb == 1 else jnp.concatenate(ys, axis=0)
    oc_ref[...] = per_stream((_silu(gc) * (bb * y)).astype(BF16))
    tail_ref[...] = ubuf[:, tm:tm + STATE_ROWS, :]


def _proj_call(x, state8, cos_t, sin_t, ln_g, w_in_bf, qg, kg, conv_w, conv_b, gmat, *, sb, tm, tkv,
               emit_vt, row_offset=0):
    nb, t, d = x.shape
    nt = t // tm
    assert t % tm == 0 and tm % tkv == 0 and nb % sb == 0 and not (emit_vt and sb > 1)
    assert cos_t.shape[0] == sb * t
    row = lambda b, j: (b, j, 0)
    const2 = lambda b, j: (0, 0)
    in_specs = [
        pl.BlockSpec((sb, tm, d), row),
        pl.BlockSpec((sb, STATE_ROWS, WIDTH), lambda b, j: (b, 0, 0)),
        pl.BlockSpec((sb * tm, V_DIM), lambda b, j: (j, 0)),
        pl.BlockSpec((sb * tm, V_DIM), lambda b, j: (j, 0)),
        pl.BlockSpec((1, d), const2),
        pl.BlockSpec(w_in_bf.shape, const2, pipeline_mode=pl.Buffered(1)),
        pl.BlockSpec((1, WIDTH), const2),
        pl.BlockSpec((1, WIDTH), const2),
        pl.BlockSpec((CONV_W, WIDTH), const2),
        pl.BlockSpec((1, WIDTH), const2),
        pl.BlockSpec((GROUP_SUM_WIDTH, GROUP_SUM_WIDTH), const2),
    ]
    tile = pl.BlockSpec((sb, tm, WIDTH), row)
    heads_tile = pl.BlockSpec(
        (pl.Element(sb), pl.Element(tm * N_HEADS), pl.Element(V_DIM)),
        lambda b, j: (b * sb, pl.multiple_of((row_offset + j * tm) * N_HEADS, V7X_SUBLANES), 0))
    out_shape = [
        jax.ShapeDtypeStruct((nb, t, WIDTH), BF16),
        jax.ShapeDtypeStruct((nb, (row_offset + t) * N_HEADS, V_DIM), F32),
        jax.ShapeDtypeStruct((nb, (row_offset + t) * N_HEADS, V_DIM), F32),
        jax.ShapeDtypeStruct((nb, t, WIDTH), BF16),
        jax.ShapeDtypeStruct((nb, t, WIDTH), BF16),
        jax.ShapeDtypeStruct((nb, t, WIDTH), BF16),
        jax.ShapeDtypeStruct((nb, STATE_ROWS, WIDTH), F32),
    ]
    out_specs = [tile, heads_tile, heads_tile, tile, tile, tile,
                 pl.BlockSpec((sb, STATE_ROWS, WIDTH), lambda b, j: (b, 0, 0))]
    if emit_vt:
        out_shape.append(jax.ShapeDtypeStruct((nb, t // tkv, WIDTH, tkv), BF16))
        out_specs.append(pl.BlockSpec((1, tm // tkv, WIDTH, tkv), lambda b, j: (b, j, 0, 0)))
    return pl.pallas_call(
        functools.partial(_proj_kernel, sb=sb, tm=tm, tkv=tkv, emit_vt=emit_vt),
        grid=(nb // sb, nt),
        in_specs=in_specs,
        out_specs=out_specs,
        out_shape=out_shape,
        scratch_shapes=[pltpu.VMEM((sb, tm + STATE_ROWS, WIDTH), F32)],
        compiler_params=pltpu.CompilerParams(
            dimension_semantics=("arbitrary", "arbitrary"),
            vmem_limit_bytes=V7X_VMEM_LIMIT),
        name="proj",
    )(x, state8, cos_t, sin_t, ln_g, w_in_bf, qg, kg, conv_w, conv_b, gmat)


def _lambda(lq1_ref, lk1_ref, lq2_ref, lk2_ref):
    a = jnp.sum(lq1_ref[...] * lk1_ref[...], axis=-1, keepdims=True)
    b = jnp.sum(lq2_ref[...] * lk2_ref[...], axis=-1, keepdims=True)
    return jnp.exp(a) - jnp.exp(b) + LAMBDA_INIT


def _conv_branch_out(x, oc, wout_ref):
    return x + jnp.dot(oc, wout_ref[WIDTH:, :], preferred_element_type=F32)


def _gated_attn(o_heads, sga, subg):
    parts = []
    for h, o in enumerate(o_heads):
        ms = jnp.mean(o * o, axis=-1, keepdims=True)
        on = o * lax.rsqrt(ms + EPS) * subg * (1.0 - LAMBDA_INIT)
        parts.append((sga[:, h * V_DIM:(h + 1) * V_DIM].astype(F32) * on).astype(BF16))
    return jnp.concatenate(parts, axis=1)


def _attn_branch_out(base, mix, wout_ref):
    return base + jnp.dot(mix, wout_ref[:WIDTH, :], preferred_element_type=F32)


def _prompt_attn_kernel(x_ref, q_ref, kb_ref, vt_ref, km_ref, vmt_ref, sga_ref, oc_ref,
                        subg_ref, lq1_ref, lk1_ref, lq2_ref, lk2_ref, wout_ref, y_ref,
                        qz_ref, m_ref, l_ref, acc_ref, *, tq, kb, big, bounded):
    j = pl.program_id(1)
    lanes = 2 * tq
    n_diag = tq // kb
    n_full = j * n_diag

    lane = lax.broadcasted_iota(jnp.int32, (1, V_DIM), 1)
    for h in range(N_HEADS):
        qh = q_ref[0, :, h * V_DIM:(h + 1) * V_DIM]
        qz_ref[h, 0:tq, :] = jnp.where(lane < HEAD_DIM, qh, jnp.zeros_like(qh))
        qz_ref[h, tq:lanes, :] = jnp.where(lane >= HEAD_DIM, qh, jnp.zeros_like(qh))

    def scores(k_h, h):
        return lax.dot_general(k_h, qz_ref[h], (((1,), (1,)), ((), ())), preferred_element_type=F32)

    def k_slab(k, h):
        return k[:, h * V_DIM:(h + 1) * V_DIM]

    def sublane_partial(e):
        return jnp.sum(e.reshape(e.shape[0] // V7X_SUBLANES, V7X_SUBLANES, lanes), axis=0)

    kchunk = lax.broadcasted_iota(jnp.int32, (kb, 1), 0) // CHUNK
    qchunk = (lax.broadcasted_iota(jnp.int32, (1, lanes), 1) % tq) // CHUNK
    diag_masks = [kchunk + d * (kb // CHUNK) <= qchunk for d in range(n_diag)]

    def late_queries(a, d, axis):
        lo = d * kb
        if axis == 0:
            return jnp.concatenate([a[lo:tq], a[tq + lo:lanes]], axis=0)
        return jnp.concatenate([a[:, lo:tq], a[:, tq + lo:lanes]], axis=1)

    def add_late(full, part, d):
        lo, w = d * kb, tq - d * kb
        return jnp.concatenate([full[:, 0:lo], full[:, lo:tq] + part[:, 0:w],
                                full[:, tq:tq + lo], full[:, tq + lo:lanes] + part[:, w:2 * w]],
                               axis=1)

    def bounded_path():
        def key_pass(k, pieces, assign, late_blocks=()):
            s_next = scores(k_slab(k, 0), 0)
            for h in range(N_HEADS):
                s = s_next
                if h + 1 < N_HEADS:
                    s_next = scores(k_slab(k, h + 1), h + 1)
                s_late = [lax.dot_general(k_slab(kl, h), late_queries(qz_ref[h], d, 0),
                                          (((1,), (1,)), ((), ())), preferred_element_type=F32)
                          for d, kl, _ in late_blocks]
                e = jnp.exp2(s)
                l_new = acc_new = None
                for row0, rows, vt, mask in pieces:
                    ep = e[row0:row0 + rows, :]
                    if mask is not None:
                        ep = jnp.where(mask, ep, 0.0)
                    lp = sublane_partial(ep)
                    ap = jnp.dot(vt[h * V_DIM:(h + 1) * V_DIM, :], ep.astype(BF16),
                                 preferred_element_type=F32)
                    l_new = lp if l_new is None else l_new + lp
                    acc_new = ap if acc_new is None else acc_new + ap
                for (d, _, vt), sl in zip(late_blocks, s_late):
                    ep = jnp.where(late_queries(diag_masks[d], d, 1), jnp.exp2(sl), 0.0)
                    lp = jnp.sum(ep.reshape(kb // V7X_SUBLANES, V7X_SUBLANES, ep.shape[1]), axis=0)
                    ap = jnp.dot(vt[h * V_DIM:(h + 1) * V_DIM, :], ep.astype(BF16),
                                 preferred_element_type=F32)
                    l_new = add_late(l_new, lp, d)
                    acc_new = add_late(acc_new, ap, d)
                if assign:
                    l_ref[h] = l_new
                    acc_ref[h] = acc_new
                else:
                    l_ref[h] += l_new
                    acc_ref[h] += acc_new

        n_big = n_full // big
        rem = n_full - n_big * big
        for r in range(0, big, n_diag):
            @pl.when(rem == r)
            def _():
                first = n_full - r
                n_blocks = r + 1
                kblocks = kb_ref[0, pl.ds(first, n_blocks)].reshape(n_blocks * kb, WIDTH)
                k = jnp.concatenate([km_ref[...], kblocks], axis=0)
                pieces = [(0, N_META, vmt_ref[...], None)]
                pieces += [(N_META + t * kb, kb, vt_ref[0, first + t],
                            None if t < r else diag_masks[0]) for t in range(n_blocks)]
                late = [(d, kb_ref[0, n_full + d], vt_ref[0, n_full + d])
                        for d in range(1, n_diag)]
                key_pass(k, pieces, assign=True, late_blocks=late)

        def big_block(i, carry):
            first = i * big
            k = kb_ref[0, pl.ds(first, big)].reshape(big * kb, WIDTH)
            key_pass(k, [(t * kb, kb, vt_ref[0, first + t], None) for t in range(big)], assign=False)
            return carry

        lax.fori_loop(0, n_big, big_block, 0)

    def general_path():
        for h in range(N_HEADS):
            s = scores(k_slab(km_ref, h), h)
            m = jnp.max(s, axis=0, keepdims=True)
            e = jnp.exp2(s - m)
            m_ref[h:h + 1, :] = m
            l_ref[h] = jnp.zeros((V7X_SUBLANES, lanes), F32)
            l_ref[h, 0:1, :] = jnp.sum(e, axis=0, keepdims=True)
            acc_ref[h] = jnp.dot(vmt_ref[h * V_DIM:(h + 1) * V_DIM, :], e.astype(BF16),
                                 preferred_element_type=F32)

        def block(i, mask):
            k = kb_ref[0, i]
            vt = vt_ref[0, i]
            for h in range(N_HEADS):
                s = scores(k_slab(k, h), h)
                if mask is not None:
                    s = jnp.where(mask, s, NEG_BIG)
                m_old = m_ref[h:h + 1, :]
                m_new = jnp.maximum(m_old, jnp.max(s, axis=0, keepdims=True))
                alpha = jnp.exp2(m_old - m_new)
                e = jnp.exp2(s - m_new)
                l_ref[h, 0:1, :] = alpha * l_ref[h, 0:1, :] + jnp.sum(e, axis=0, keepdims=True)
                acc_ref[h] = alpha * acc_ref[h] + jnp.dot(
                    vt[h * V_DIM:(h + 1) * V_DIM, :], e.astype(BF16), preferred_element_type=F32)
                m_ref[h:h + 1, :] = m_new

        def full_block(i, carry):
            block(i, None)
            return carry

        lax.fori_loop(0, n_full, full_block, 0)
        for d in range(n_diag):
            block(n_full + d, diag_masks[d])

    (bounded_path if bounded else general_path)()

    base = _conv_branch_out(x_ref[0], oc_ref[0], wout_ref)
    lam = _lambda(lq1_ref, lk1_ref, lq2_ref, lk2_ref)
    o_heads = []
    for h in range(N_HEADS):
        o2 = acc_ref[h] / jnp.sum(l_ref[h], axis=0, keepdims=True)
        o_heads.append((o2[:, 0:tq] - lam * o2[:, tq:lanes]).T)
    y_ref[0] = _attn_branch_out(base, _gated_attn(o_heads, sga_ref[0], subg_ref[...]), wout_ref)


def _prompt_attn(x, q, kb4, vt4, km_bf, vmt_bf, sga, oc, subg, lq1, lk1, lq2, lk2, w_out_bf,
                 *, tq, kb, big, bounded):
    nb, s, d = x.shape
    nq = s // tq
    assert tq % kb == 0 and big % (tq // kb) == 0
    single_buffer = dict(pipeline_mode=pl.Buffered(1))
    row = lambda b, j: (b, j, 0)
    const2 = lambda b, j: (0, 0)
    whole = lambda b, j: (b, 0, 0, 0)
    in_specs = [
        pl.BlockSpec((1, tq, d), row),
        pl.BlockSpec((1, tq, WIDTH), row),
        pl.BlockSpec((1, s // kb, kb, WIDTH), whole),
        pl.BlockSpec((1, s // kb, WIDTH, kb), whole, **single_buffer),
        pl.BlockSpec((N_META, WIDTH), const2),
        pl.BlockSpec((WIDTH, N_META), const2),
        pl.BlockSpec((1, tq, WIDTH), row),
        pl.BlockSpec((1, tq, WIDTH), row),
        pl.BlockSpec((1, V_DIM), const2),
        pl.BlockSpec((1, HEAD_DIM), const2),
        pl.BlockSpec((1, HEAD_DIM), const2),
        pl.BlockSpec((1, HEAD_DIM), const2),
        pl.BlockSpec((1, HEAD_DIM), const2),
        pl.BlockSpec(w_out_bf.shape, const2),
    ]
    return pl.pallas_call(
        functools.partial(_prompt_attn_kernel, tq=tq, kb=kb, big=big, bounded=bounded),
        grid=(nb, nq),
        in_specs=in_specs,
        out_specs=pl.BlockSpec((1, tq, d), row),
        out_shape=jax.ShapeDtypeStruct((nb, s, d), F32),
        scratch_shapes=[
            pltpu.VMEM((N_HEADS, 2 * tq, V_DIM), BF16),
            pltpu.VMEM((N_HEADS, 2 * tq), F32),
            pltpu.VMEM((N_HEADS, V7X_SUBLANES, 2 * tq), F32),
            pltpu.VMEM((N_HEADS, V_DIM, 2 * tq), F32),
        ],
        compiler_params=pltpu.CompilerParams(
            dimension_semantics=("arbitrary", "arbitrary"),
            vmem_limit_bytes=V7X_VMEM_LIMIT),
        name="prompt_attn" if bounded else "prompt_attn_general",
    )(x, q, kb4, vt4, km_bf, vmt_bf, sga, oc, subg, lq1, lk1, lq2, lk2, w_out_bf)


def _sample_attn_kernel(x_ref, q_ref, kn_ref, vn_ref, ck_ref, cv_ref, km_ref, vm_ref, sga_ref,
                        oc_ref, subg_ref, lq1_ref, lk1_ref, lq2_ref, lk2_ref, wout_ref, y_ref,
                        kcat, vcat, mix_ref, *, past, t_new, n_keys_pad, first):
    b = pl.program_id(0)
    nb = pl.num_programs(0)
    n_keys = past + N_META + t_new
    for h in range(N_HEADS):
        slab = slice(h * V_DIM, (h + 1) * V_DIM)
        kcat[0:past, slab] = ck_ref[0, pl.ds(h, past, stride=N_HEADS), :].astype(BF16)
        kcat[past + N_META:n_keys, slab] = kn_ref[0, pl.ds(h, t_new, stride=N_HEADS), :].astype(BF16)
        vcat[0:past, slab] = cv_ref[0, pl.ds(h, past, stride=N_HEADS), :].astype(BF16)
        vcat[past:past + N_META, slab] = vm_ref[pl.ds(h, N_META, stride=N_HEADS), :].astype(BF16)
        vcat[past + N_META:n_keys, slab] = vn_ref[0, pl.ds(h, t_new, stride=N_HEADS), :].astype(BF16)
    kcat[past:past + N_META, :] = km_ref[...]
    kcat[n_keys:n_keys_pad, :] = jnp.zeros((n_keys_pad - n_keys, WIDTH), BF16)
    vcat[n_keys:n_keys_pad, :] = jnp.zeros((n_keys_pad - n_keys, WIDTH), BF16)

    lane = lax.broadcasted_iota(jnp.int32, (1, V_DIM), 1)
    valid = lax.broadcasted_iota(jnp.int32, (1, n_keys_pad), 1) < n_keys
    lam = _lambda(lq1_ref, lk1_ref, lq2_ref, lk2_ref)
    o_heads = []
    for h in range(N_HEADS):
        qh = q_ref[0, :, h * V_DIM:(h + 1) * V_DIM]
        q2 = jnp.concatenate([jnp.where(lane < HEAD_DIM, qh, jnp.zeros_like(qh)),
                              jnp.where(lane >= HEAD_DIM, qh, jnp.zeros_like(qh))], axis=0)
        s = lax.dot_general(q2, kcat[:, h * V_DIM:(h + 1) * V_DIM], (((1,), (1,)), ((), ())),
                            preferred_element_type=F32)
        s = jnp.where(valid, s, NEG_BIG)
        e = jnp.exp2(s - jnp.max(s, axis=-1, keepdims=True))
        o2 = jnp.dot(e.astype(BF16), vcat[:, h * V_DIM:(h + 1) * V_DIM],
                     preferred_element_type=F32) / jnp.sum(e, axis=-1, keepdims=True)
        o_heads.append(o2[0:t_new] - lam * o2[t_new:2 * t_new])
    rows = pl.ds(pl.multiple_of(b * t_new, t_new), t_new)
    mix_ref[rows, :] = _gated_attn(o_heads, sga_ref[b + first], subg_ref[...])

    @pl.when(b == nb - 1)
    def _():
        n_rows = mix_ref.shape[0]
        oc = oc_ref[first:first + n_rows // t_new]
        base = _conv_branch_out(x_ref[...].reshape(n_rows, x_ref.shape[-1]),
                                oc.reshape(n_rows, WIDTH), wout_ref)
        y_ref[...] = _attn_branch_out(base, mix_ref[...], wout_ref).reshape(y_ref.shape)


def _sample_attn(x, q, kn, vn, ck, cv, km_bf, vm, sga, oc, subg, lq1, lk1, lq2, lk2, w_out_bf, *,
                 first):
    nb, t_new, d = x.shape
    n_proj = q.shape[0]
    past = ck.shape[1] // N_HEADS
    n_keys_pad = -(-(past + N_META + t_new) // V_DIM) * V_DIM
    row = lambda b: (b, 0, 0)
    proj_row = lambda b: (b + first, 0, 0)
    const2 = lambda b: (0, 0)
    const3 = lambda b: (0, 0, 0)
    in_specs = [
        pl.BlockSpec((nb, t_new, d), const3),
        pl.BlockSpec((1, t_new, WIDTH), proj_row),
        pl.BlockSpec((1, t_new * N_HEADS, V_DIM), proj_row),
        pl.BlockSpec((1, t_new * N_HEADS, V_DIM), proj_row),
        pl.BlockSpec((1, past * N_HEADS, V_DIM), row),
        pl.BlockSpec((1, past * N_HEADS, V_DIM), row),
        pl.BlockSpec((N_META, WIDTH), const2),
        pl.BlockSpec((N_META * N_HEADS, V_DIM), const2),
        pl.BlockSpec((n_proj, t_new, WIDTH), const3),
        pl.BlockSpec((n_proj, t_new, WIDTH), const3),
        pl.BlockSpec((1, V_DIM), const2),
        pl.BlockSpec((1, HEAD_DIM), const2),
        pl.BlockSpec((1, HEAD_DIM), const2),
        pl.BlockSpec((1, HEAD_DIM), const2),
        pl.BlockSpec((1, HEAD_DIM), const2),
        pl.BlockSpec(w_out_bf.shape, const2),
    ]
    return pl.pallas_call(
        functools.partial(_sample_attn_kernel, past=past, t_new=t_new, n_keys_pad=n_keys_pad,
                          first=first),
        grid=(nb,),
        in_specs=in_specs,
        out_specs=pl.BlockSpec((nb, t_new, d), const3),
        out_shape=jax.ShapeDtypeStruct((nb, t_new, d), F32),
        scratch_shapes=[pltpu.VMEM((n_keys_pad, WIDTH), BF16),
                        pltpu.VMEM((n_keys_pad, WIDTH), BF16),
                        pltpu.VMEM((nb * t_new, WIDTH), BF16)],
        compiler_params=pltpu.CompilerParams(
            dimension_semantics=("arbitrary",),
            vmem_limit_bytes=V7X_VMEM_LIMIT),
        name="sample_attn",
    )(x, q, kn, vn, ck, cv, km_bf, vm, sga, oc, subg, lq1, lk1, lq2, lk2, w_out_bf)


def _rope_tables(pos):
    half = ROT_DIM // 2
    inv = ROPE_THETA ** (-jnp.arange(0, ROT_DIM, 2, dtype=F32) / ROT_DIM)
    ang = pos.astype(F32)[:, None] * inv[None, :]
    cos, sin = lax.optimization_barrier((jnp.cos(ang), jnp.sin(ang)))
    t = pos.shape[0]
    rest = HEAD_DIM - 2 * half
    c64 = jnp.concatenate([cos, cos, jnp.ones((t, rest), F32)], axis=1)
    s64 = jnp.concatenate([-sin, sin, jnp.zeros((t, rest), F32)], axis=1)
    return jnp.tile(c64, (1, 2)), jnp.tile(s64, (1, 2))


def _score_bound(qg, kg):
    bound = Q_SCALE * HEAD_DIM * jnp.max(jnp.abs(qg)) * jnp.max(jnp.abs(kg)) * (1.0 + 2.0 ** -6)
    return bound.reshape(1).astype(F32)


def _pad_state(state):
    return jnp.pad(state, ((0, 0), (STATE_ROWS - (CONV_W - 1), 0), (0, 0)))


def _tile_sizes(seq):
    kb = KEY_BLOCK if seq % KEY_BLOCK == 0 else seq
    tq = QUERY_TILE if seq % QUERY_TILE == 0 else kb
    return (2 * tq if seq % (2 * tq) == 0 else tq), tq, kb


def kernel(x_prompt, x_sample, cache_k, cache_v, state_conv, meta_tokens, ln_g, w_in, q_norm_g,
           k_norm_g, lam_q1, lam_k1, lam_q2, lam_k2, subln_g, conv_w, conv_b, w_out):
    assert ln_g.shape[0] == 1, "single-layer stack"
    nb, seq, d = x_prompt.shape
    db, t_new, _ = x_sample.shape
    past = cache_k.shape[2]

    w_in_bf = w_in[0].astype(BF16)
    w_out_bf = w_out[0].astype(BF16)
    lng = ln_g[0][None, :]
    qg = jnp.tile(q_norm_g[0], 2 * N_HEADS)[None, :]
    kg = jnp.tile(k_norm_g[0], 2 * N_HEADS)[None, :]
    subg = subln_g[0][None, :]
    cw, cb = conv_w[0], conv_b[0][None, :]
    lams = [p[0][None, :] for p in (lam_q1, lam_k1, lam_q2, lam_k2)]
    grp = jnp.arange(GROUP_SUM_WIDTH, dtype=jnp.int32) // HEAD_DIM
    gmat = (grp[:, None] == grp[None, :]).astype(BF16)

    pos_m = jnp.arange(N_META, dtype=jnp.int32)
    pos_p = N_META + jnp.arange(seq, dtype=jnp.int32)
    pos_s = N_META + past + jnp.arange(t_new, dtype=jnp.int32)
    weights = (lng, w_in_bf, qg, kg, cw, cb, gmat)
    tm, tq, kb = _tile_sizes(seq)

    assert t_new == N_META, "meta tokens and new frames are projected as equal-length streams"
    cos_m, sin_m = _rope_tables(pos_m)
    cos_s, sin_s = _rope_tables(pos_s)
    n_small = 1 + db
    q_sm, k_sm, v_sm, kb_sm, sga_sm, oc_sm, tail_sm = _proj_call(
        jnp.concatenate([meta_tokens[None], x_sample], axis=0),
        jnp.concatenate([jnp.zeros((1, STATE_ROWS, WIDTH), F32), _pad_state(state_conv[0])], axis=0),
        jnp.concatenate([cos_m] + [cos_s] * db, axis=0),
        jnp.concatenate([sin_m] + [sin_s] * db, axis=0),
        *weights, sb=n_small, tm=t_new, tkv=t_new, emit_vt=False)
    km, vm, km_bf, tail_m = k_sm[0], v_sm[0], kb_sm[0], tail_sm[0:1]
    ks, vs, tail_s = k_sm[1:], v_sm[1:], tail_sm[1:]

    state_p = jnp.broadcast_to(tail_m, (nb, STATE_ROWS, WIDTH))
    qp, kp, vp, kp_bf, sga_p, oc_p, tail_p, vtp = _proj_call(
        x_prompt, state_p, *_rope_tables(pos_p), *weights, sb=1, tm=tm, tkv=kb, emit_vt=True,
        row_offset=N_META)
    attn = lambda bounded: functools.partial(_prompt_attn, tq=tq, kb=kb, big=BIG_KEY_BLOCKS,
                                             bounded=bounded)
    y_prompt = lax.cond(
        _score_bound(q_norm_g[0], k_norm_g[0])[0] <= SAFE_SCORE_BOUND, attn(True), attn(False),
        x_prompt, qp, kp_bf.reshape(nb, seq // kb, kb, WIDTH), vtp, km_bf,
        vm.reshape(N_META, WIDTH).T.astype(BF16), sga_p, oc_p, subg, *lams, w_out_bf)
    kp = kp.at[:, :N_META * N_HEADS].set(jnp.broadcast_to(km[None], (nb,) + km.shape))
    vp = vp.at[:, :N_META * N_HEADS].set(jnp.broadcast_to(vm[None], (nb,) + vm.shape))

    y_sample = _sample_attn(
        x_sample, q_sm, k_sm, v_sm, cache_k[0].reshape(db, past * N_HEADS, V_DIM),
        cache_v[0].reshape(db, past * N_HEADS, V_DIM), km_bf, vm, sga_sm, oc_sm, subg, *lams,
        w_out_bf, first=1)

    def heads_form(a, rows):
        return a.reshape(1, a.shape[0], rows, N_HEADS, V_DIM)

    tail_rows = slice(STATE_ROWS - (CONV_W - 1), STATE_ROWS)
    return (y_prompt, y_sample, heads_form(kp, N_META + seq), heads_form(vp, N_META + seq),
            tail_p[None, :, tail_rows], heads_form(ks, t_new), heads_form(vs, t_new),
            tail_s[None, :, tail_rows])
```

```python
import functools
import math

import jax
import jax.numpy as jnp
from jax import lax
from jax.experimental import pallas as pl
from jax.experimental.pallas import tpu as pltpu

F32 = jnp.float32
BF16 = jnp.bfloat16

CHUNK = 64
N_META = 16
N_HEADS = 4
HEAD_DIM = 64
V_DIM = 2 * HEAD_DIM
WIDTH = N_HEADS * V_DIM
CONV_W = 3
ROT_DIM = HEAD_DIM // 4
ROPE_THETA = 500000.0
EPS = 1e-6
LAMBDA_INIT = 0.8 - 0.6 * math.exp(-0.3 * 0)
Q_SCALE = HEAD_DIM ** -0.5 * math.log2(math.e)
NEG_BIG = -1e30
SAFE_SCORE_BOUND = 40.0
V7X_SUBLANES = 8
V7X_MXU_WIDTH = 256
V7X_VMEM_LIMIT = 60 * 1024 * 1024

KEY_BLOCK = V7X_MXU_WIDTH
QUERY_TILE = 2 * KEY_BLOCK
BIG_KEY_BLOCKS = 8
GROUP_SUM_WIDTH = V7X_MXU_WIDTH
STATE_ROWS = V7X_SUBLANES


def _silu(x):
    return x / (1.0 + jnp.exp(-x))


def _proj_kernel(x_ref, st_ref, cos_ref, sin_ref, lng_ref, win_ref, qg_ref, kg_ref, cw_ref, cb_ref,
                 gmat_ref, q_ref, kf_ref, vf_ref, kb_ref, sga_ref, oc_ref, tail_ref, *rest,
                 sb, tm, tkv, emit_vt):
    if emit_vt:
        vt_ref, ubuf = rest
    else:
        (ubuf,) = rest
    j = pl.program_id(1)
    rows = sb * tm

    @pl.when(j == 0)
    def _():
        ubuf[:, 0:STATE_ROWS, :] = st_ref[...]

    @pl.when(j > 0)
    def _():
        ubuf[:, 0:STATE_ROWS, :] = ubuf[:, tm:tm + STATE_ROWS, :]

    x = x_ref[...].reshape(rows, x_ref.shape[-1])
    xg = (x * lng_ref[...]).astype(BF16)
    inv_rms = lax.rsqrt(jnp.mean(x * x, axis=-1, keepdims=True) + EPS)

    def proj(i):
        return inv_rms * jnp.dot(xg, win_ref[:, i * WIDTH:(i + 1) * WIDTH],
                                 preferred_element_type=F32)

    def group_mean_sq(t):
        tt = (t * t).astype(BF16)
        halves = [jnp.dot(tt[:, i * GROUP_SUM_WIDTH:(i + 1) * GROUP_SUM_WIDTH], gmat_ref[...],
                          preferred_element_type=F32) for i in range(WIDTH // GROUP_SUM_WIDTH)]
        return jnp.concatenate(halves, axis=1) * (1.0 / HEAD_DIM)

    cos = cos_ref[...]
    sin = sin_ref[...]
    lane = lax.broadcasted_iota(jnp.int32, (1, V_DIM), 1) % HEAD_DIM
    take_lower = lane >= ROT_DIM // 2

    def norm_rope(t, mean_sq, g):
        tn = t * lax.rsqrt(mean_sq + EPS) * g
        outs = []
        for h in range(N_HEADS):
            th = tn[:, h * V_DIM:(h + 1) * V_DIM]
            lower = pltpu.roll(th, ROT_DIM // 2, axis=1)
            upper = pltpu.roll(th, V_DIM - ROT_DIM // 2, axis=1)
            outs.append(th * cos + jnp.where(take_lower, lower, upper) * sin)
        return jnp.concatenate(outs, axis=1)

    def per_stream(a):
        return a.reshape(sb, tm, a.shape[-1])

    pq = proj(0)
    pk = proj(1)
    msq = group_mean_sq(pq)
    v = proj(2)
    msk = group_mean_sq(pk)
    ga = proj(3)
    q_ref[...] = per_stream((norm_rope(pq, msq, qg_ref[...]) * Q_SCALE).astype(BF16))
    bb = proj(4)
    k = norm_rope(pk, msk, kg_ref[...])
    kb_ref[...] = per_stream(k.astype(BF16))
    cc = proj(5)
    for b in range(sb):
        for h in range(N_HEADS):
            src = (slice(b * tm, (b + 1) * tm), slice(h * V_DIM, (h + 1) * V_DIM))
            kf_ref[b, pl.ds(h, tm, stride=N_HEADS), :] = k[src]
            vf_ref[b, pl.ds(h, tm, stride=N_HEADS), :] = v[src]
    hh = proj(6)
    if emit_vt:
        for s in range(tm // tkv):
            vt_ref[0, s] = v[s * tkv:(s + 1) * tkv, :].T.astype(BF16)
    gc = proj(7)
    sga_ref[...] = per_stream(_silu(ga).astype(BF16))

    u = cc * hh
    ys = []
    for b in range(sb):
        ubuf[b, STATE_ROWS:STATE_ROWS + tm, :] = u[b * tm:(b + 1) * tm, :]
        y = cb_ref[...]
        for tap in range(CONV_W):
            off = STATE_ROWS - (CONV_W - 1) + tap
            y = y + cw_ref[tap:tap + 1, :] * ubuf[b, off:off + tm, :]
        ys.append(y)
    y = ys[0] if sb == 1 else jnp.concatenate(ys, axis=0)
    oc_ref[...] = per_stream((_silu(gc) * (bb * y)).astype(BF16))
    tail_ref[...] = ubuf[:, tm:tm + STATE_ROWS, :]


def _proj_call(x, state8, cos_t, sin_t, ln_g, w_in_bf, qg, kg, conv_w, conv_b, gmat, *, sb, tm, tkv,
               emit_vt, row_offset=0):
    nb, t, d = x.shape
    nt = t // tm
    assert t % tm == 0 and tm % tkv == 0 and nb % sb == 0 and not (emit_vt and sb > 1)
    assert cos_t.shape[0] == sb * t
    row = lambda b, j: (b, j, 0)
    const2 = lambda b, j: (0, 0)
    in_specs = [
        pl.BlockSpec((sb, tm, d), row),
        pl.BlockSpec((sb, STATE_ROWS, WIDTH), lambda b, j: (b, 0, 0)),
        pl.BlockSpec((sb * tm, V_DIM), lambda b, j: (j, 0)),
        pl.BlockSpec((sb * tm, V_DIM), lambda b, j: (j, 0)),
        pl.BlockSpec((1, d), const2),
        pl.BlockSpec(w_in_bf.shape, const2, pipeline_mode=pl.Buffered(1)),
        pl.BlockSpec((1, WIDTH), const2),
        pl.BlockSpec((1, WIDTH), const2),
        pl.BlockSpec((CONV_W, WIDTH), const2),
        pl.BlockSpec((1, WIDTH), const2),
        pl.BlockSpec((GROUP_SUM_WIDTH, GROUP_SUM_WIDTH), const2),
    ]
    tile = pl.BlockSpec((sb, tm, WIDTH), row)
    heads_tile = pl.BlockSpec(
        (pl.Element(sb), pl.Element(tm * N_HEADS), pl.Element(V_DIM)),
        lambda b, j: (b * sb, pl.multiple_of((row_offset + j * tm) * N_HEADS, V7X_SUBLANES), 0))
    out_shape = [
        jax.ShapeDtypeStruct((nb, t, WIDTH), BF16),
        jax.ShapeDtypeStruct((nb, (row_offset + t) * N_HEADS, V_DIM), F32),
        jax.ShapeDtypeStruct((nb, (row_offset + t) * N_HEADS, V_DIM), F32),
        jax.ShapeDtypeStruct((nb, t, WIDTH), BF16),
        jax.ShapeDtypeStruct((nb, t, WIDTH), BF16),
        jax.ShapeDtypeStruct((nb, t, WIDTH), BF16),
        jax.ShapeDtypeStruct((nb, STATE_ROWS, WIDTH), F32),
    ]
    out_specs = [tile, heads_tile, heads_tile, tile, tile, tile,
                 pl.BlockSpec((sb, STATE_ROWS, WIDTH), lambda b, j: (b, 0, 0))]
    if emit_vt:
        out_shape.append(jax.ShapeDtypeStruct((nb, t // tkv, WIDTH, tkv), BF16))
        out_specs.append(pl.BlockSpec((1, tm // tkv, WIDTH, tkv), lambda b, j: (b, j, 0, 0)))
    return pl.pallas_call(
        functools.partial(_proj_kernel, sb=sb, tm=tm, tkv=tkv, emit_vt=emit_vt),
        grid=(nb // sb, nt),
        in_specs=in_specs,
        out_specs=out_specs,
        out_shape=out_shape,
        scratch_shapes=[pltpu.VMEM((sb, tm + STATE_ROWS, WIDTH), F32)],
        compiler_params=pltpu.CompilerParams(
            dimension_semantics=("arbitrary", "arbitrary"),
            vmem_limit_bytes=V7X_VMEM_LIMIT),
        name="proj",
    )(x, state8, cos_t, sin_t, ln_g, w_in_bf, qg, kg, conv_w, conv_b, gmat)


def _lambda(lq1_ref, lk1_ref, lq2_ref, lk2_ref):
    a = jnp.sum(lq1_ref[...] * lk1_ref[...], axis=-1, keepdims=True)
    b = jnp.sum(lq2_ref[...] * lk2_ref[...], axis=-1, keepdims=True)
    return jnp.exp(a) - jnp.exp(b) + LAMBDA_INIT


def _conv_branch_out(x, oc, wout_ref):
    return x + jnp.dot(oc, wout_ref[WIDTH:, :], preferred_element_type=F32)


def _gated_attn(o_heads, sga, subg):
    parts = []
    for h, o in enumerate(o_heads):
        ms = jnp.mean(o * o, axis=-1, keepdims=True)
        on = o * lax.rsqrt(ms + EPS) * subg * (1.0 - LAMBDA_INIT)
        parts.append((sga[:, h * V_DIM:(h + 1) * V_DIM].astype(F32) * on).astype(BF16))
    return jnp.concatenate(parts, axis=1)


def _attn_branch_out(base, mix, wout_ref):
    return base + jnp.dot(mix, wout_ref[:WIDTH, :], preferred_element_type=F32)


def _prompt_attn_kernel(x_ref, q_ref, kb_ref, vt_ref, km_ref, vmt_ref, sga_ref, oc_ref,
                        subg_ref, lq1_ref, lk1_ref, lq2_ref, lk2_ref, wout_ref, y_ref,
                        qz_ref, m_ref, l_ref, acc_ref, *, tq, kb, big, bounded):
    j = pl.program_id(1)
    lanes = 2 * tq
    n_diag = tq // kb
    n_full = j * n_diag

    lane = lax.broadcasted_iota(jnp.int32, (1, V_DIM), 1)
    for h in range(N_HEADS):
        qh = q_ref[0, :, h * V_DIM:(h + 1) * V_DIM]
        qz_ref[h, 0:tq, :] = jnp.where(lane < HEAD_DIM, qh, jnp.zeros_like(qh))
        qz_ref[h, tq:lanes, :] = jnp.where(lane >= HEAD_DIM, qh, jnp.zeros_like(qh))

    def scores(k_h, h):
        return lax.dot_general(k_h, qz_ref[h], (((1,), (1,)), ((), ())), preferred_element_type=F32)

    def k_slab(k, h):
        return k[:, h * V_DIM:(h + 1) * V_DIM]

    def sublane_partial(e):
        return jnp.sum(e.reshape(e.shape[0] // V7X_SUBLANES, V7X_SUBLANES, lanes), axis=0)

    kchunk = lax.broadcasted_iota(jnp.int32, (kb, 1), 0) // CHUNK
    qchunk = (lax.broadcasted_iota(jnp.int32, (1, lanes), 1) % tq) // CHUNK
    diag_masks = [kchunk + d * (kb // CHUNK) <= qchunk for d in range(n_diag)]

    def late_queries(a, d, axis):
        lo = d * kb
        if axis == 0:
            return jnp.concatenate([a[lo:tq], a[tq + lo:lanes]], axis=0)
        return jnp.concatenate([a[:, lo:tq], a[:, tq + lo:lanes]], axis=1)

    def add_late(full, part, d):
        lo, w = d * kb, tq - d * kb
        return jnp.concatenate([full[:, 0:lo], full[:, lo:tq] + part[:, 0:w],
                                full[:, tq:tq + lo], full[:, tq + lo:lanes] + part[:, w:2 * w]],
                               axis=1)

    def bounded_path():
        def key_pass(k, pieces, assign, late_blocks=()):
            s_next = scores(k_slab(k, 0), 0)
            for h in range(N_HEADS):
                s = s_next
                if h + 1 < N_HEADS:
                    s_next = scores(k_slab(k, h + 1), h + 1)
                s_late = [lax.dot_general(k_slab(kl, h), late_queries(qz_ref[h], d, 0),
                                          (((1,), (1,)), ((), ())), preferred_element_type=F32)
                          for d, kl, _ in late_blocks]
                e = jnp.exp2(s)
                l_new = acc_new = None
                for row0, rows, vt, mask in pieces:
                    ep = e[row0:row0 + rows, :]
                    if mask is not None:
                        ep = jnp.where(mask, ep, 0.0)
                    lp = sublane_partial(ep)
                    ap = jnp.dot(vt[h * V_DIM:(h + 1) * V_DIM, :], ep.astype(BF16),
                                 preferred_element_type=F32)
                    l_new = lp if l_new is None else l_new + lp
                    acc_new = ap if acc_new is None else acc_new + ap
                for (d, _, vt), sl in zip(late_blocks, s_late):
                    ep = jnp.where(late_queries(diag_masks[d], d, 1), jnp.exp2(sl), 0.0)
                    lp = jnp.sum(ep.reshape(kb // V7X_SUBLANES, V7X_SUBLANES, ep.shape[1]), axis=0)
                    ap = jnp.dot(vt[h * V_DIM:(h + 1) * V_DIM, :], ep.astype(BF16),
                                 preferred_element_type=F32)
                    l_new = add_late(l_new, lp, d)
                    acc_new = add_late(acc_new, ap, d)
                if assign:
                    l_ref[h] = l_new
                    acc_ref[h] = acc_new
                else:
                    l_ref[h] += l_new
                    acc_ref[h] += acc_new

        n_big = n_full // big
        rem = n_full - n_big * big
        for r in range(0, big, n_diag):
            @pl.when(rem == r)
            def _():
                first = n_full - r
                n_blocks = r + 1
                kblocks = kb_ref[0, pl.ds(first, n_blocks)].reshape(n_blocks * kb, WIDTH)
                k = jnp.concatenate([km_ref[...], kblocks], axis=0)
                pieces = [(0, N_META, vmt_ref[...], None)]
                pieces += [(N_META + t * kb, kb, vt_ref[0, first + t],
                            None if t < r else diag_masks[0]) for t in range(n_blocks)]
                late = [(d, kb_ref[0, n_full + d], vt_ref[0, n_full + d])
                        for d in range(1, n_diag)]
                key_pass(k, pieces, assign=True, late_blocks=late)

        def big_block(i, carry):
            first = i * big
            k = kb_ref[0, pl.ds(first, big)].reshape(big * kb, WIDTH)
            key_pass(k, [(t * kb, kb, vt_ref[0, first + t], None) for t in range(big)], assign=False)
            return carry

        lax.fori_loop(0, n_big, big_block, 0)

    def general_path():
        for h in range(N_HEADS):
            s = scores(k_slab(km_ref, h), h)
            m = jnp.max(s, axis=0, keepdims=True)
            e = jnp.exp2(s - m)
            m_ref[h:h + 1, :] = m
            l_ref[h] = jnp.zeros((V7X_SUBLANES, lanes), F32)
            l_ref[h, 0:1, :] = jnp.sum(e, axis=0, keepdims=True)
            acc_ref[h] = jnp.dot(vmt_ref[h * V_DIM:(h + 1) * V_DIM, :], e.astype(BF16),
                                 preferred_element_type=F32)

        def block(i, mask):
            k = kb_ref[0, i]
            vt = vt_ref[0, i]
            for h in range(N_HEADS):
                s = scores(k_slab(k, h), h)
                if mask is not None:
                    s = jnp.where(mask, s, NEG_BIG)
                m_old = m_ref[h:h + 1, :]
                m_new = jnp.maximum(m_old, jnp.max(s, axis=0, keepdims=True))
                alpha = jnp.exp2(m_old - m_new)
                e = jnp.exp2(s - m_new)
                l_ref[h, 0:1, :] = alpha * l_ref[h, 0:1, :] + jnp.sum(e, axis=0, keepdims=True)
                acc_ref[h] = alpha * acc_ref[h] + jnp.dot(
                    vt[h * V_DIM:(h + 1) * V_DIM, :], e.astype(BF16), preferred_element_type=F32)
                m_ref[h:h + 1, :] = m_new

        def full_block(i, carry):
            block(i, None)
            return carry

        lax.fori_loop(0, n_full, full_block, 0)
        for d in range(n_diag):
            block(n_full + d, diag_masks[d])

    (bounded_path if bounded else general_path)()

    base = _conv_branch_out(x_ref[0], oc_ref[0], wout_ref)
    lam = _lambda(lq1_ref, lk1_ref, lq2_ref, lk2_ref)
    o_heads = []
    for h in range(N_HEADS):
        o2 = acc_ref[h] / jnp.sum(l_ref[h], axis=0, keepdims=True)
        o_heads.append((o2[:, 0:tq] - lam * o2[:, tq:lanes]).T)
    y_ref[0] = _attn_branch_out(base, _gated_attn(o_heads, sga_ref[0], subg_ref[...]), wout_ref)


def _prompt_attn(x, q, kb4, vt4, km_bf, vmt_bf, sga, oc, subg, lq1, lk1, lq2, lk2, w_out_bf,
                 *, tq, kb, big, bounded):
    nb, s, d = x.shape
    nq = s // tq
    assert tq % kb == 0 and big % (tq // kb) == 0
    single_buffer = dict(pipeline_mode=pl.Buffered(1))
    row = lambda b, j: (b, j, 0)
    const2 = lambda b, j: (0, 0)
    whole = lambda b, j: (b, 0, 0, 0)
    in_specs = [
        pl.BlockSpec((1, tq, d), row),
        pl.BlockSpec((1, tq, WIDTH), row),
        pl.BlockSpec((1, s // kb, kb, WIDTH), whole),
        pl.BlockSpec((1, s // kb, WIDTH, kb), whole, **single_buffer),
        pl.BlockSpec((N_META, WIDTH), const2),
        pl.BlockSpec((WIDTH, N_META), const2),
        pl.BlockSpec((1, tq, WIDTH), row),
        pl.BlockSpec((1, tq, WIDTH), row),
        pl.BlockSpec((1, V_DIM), const2),
        pl.BlockSpec((1, HEAD_DIM), const2),
        pl.BlockSpec((1, HEAD_DIM), const2),
        pl.BlockSpec((1, HEAD_DIM), const2),
        pl.BlockSpec((1, HEAD_DIM), const2),
        pl.BlockSpec(w_out_bf.shape, const2),
    ]
    return pl.pallas_call(
        functools.partial(_prompt_attn_kernel, tq=tq, kb=kb, big=big, bounded=bounded),
        grid=(nb, nq),
        in_specs=in_specs,
        out_specs=pl.BlockSpec((1, tq, d), row),
        out_shape=jax.ShapeDtypeStruct((nb, s, d), F32),
        scratch_shapes=[
            pltpu.VMEM((N_HEADS, 2 * tq, V_DIM), BF16),
            pltpu.VMEM((N_HEADS, 2 * tq), F32),
            pltpu.VMEM((N_HEADS, V7X_SUBLANES, 2 * tq), F32),
            pltpu.VMEM((N_HEADS, V_DIM, 2 * tq), F32),
        ],
        compiler_params=pltpu.CompilerParams(
            dimension_semantics=("parallel", "arbitrary"),
            vmem_limit_bytes=V7X_VMEM_LIMIT),
        name="prompt_attn" if bounded else "prompt_attn_general",
    )(x, q, kb4, vt4, km_bf, vmt_bf, sga, oc, subg, lq1, lk1, lq2, lk2, w_out_bf)


def _sample_attn_kernel(x_ref, q_ref, kn_ref, vn_ref, ck_ref, cv_ref, km_ref, vm_ref, sga_ref,
                        oc_ref, subg_ref, lq1_ref, lk1_ref, lq2_ref, lk2_ref, wout_ref, y_ref,
                        kcat, vcat, mix_ref, *, past, t_new, n_keys_pad, first):
    b = pl.program_id(0)
    nb = pl.num_programs(0)
    n_keys = past + N_META + t_new
    for h in range(N_HEADS):
        slab = slice(h * V_DIM, (h + 1) * V_DIM)
        kcat[0:past, slab] = ck_ref[0, pl.ds(h, past, stride=N_HEADS), :].astype(BF16)
        kcat[past + N_META:n_keys, slab] = kn_ref[0, pl.ds(h, t_new, stride=N_HEADS), :].astype(BF16)
        vcat[0:past, slab] = cv_ref[0, pl.ds(h, past, stride=N_HEADS), :].astype(BF16)
        vcat[past:past + N_META, slab] = vm_ref[pl.ds(h, N_META, stride=N_HEADS), :].astype(BF16)
        vcat[past + N_META:n_keys, slab] = vn_ref[0, pl.ds(h, t_new, stride=N_HEADS), :].astype(BF16)
    kcat[past:past + N_META, :] = km_ref[...]
    kcat[n_keys:n_keys_pad, :] = jnp.zeros((n_keys_pad - n_keys, WIDTH), BF16)
    vcat[n_keys:n_keys_pad, :] = jnp.zeros((n_keys_pad - n_keys, WIDTH), BF16)

    lane = lax.broadcasted_iota(jnp.int32, (1, V_DIM), 1)
    valid = lax.broadcasted_iota(jnp.int32, (1, n_keys_pad), 1) < n_keys
    lam = _lambda(lq1_ref, lk1_ref, lq2_ref, lk2_ref)
    o_heads = []
    for h in range(N_HEADS):
        qh = q_ref[0, :, h * V_DIM:(h + 1) * V_DIM]
        q2 = jnp.concatenate([jnp.where(lane < HEAD_DIM, qh, jnp.zeros_like(qh)),
                              jnp.where(lane >= HEAD_DIM, qh, jnp.zeros_like(qh))], axis=0)
        s = lax.dot_general(q2, kcat[:, h * V_DIM:(h + 1) * V_DIM], (((1,), (1,)), ((), ())),
                            preferred_element_type=F32)
        s = jnp.where(valid, s, NEG_BIG)
        e = jnp.exp2(s - jnp.max(s, axis=-1, keepdims=True))
        o2 = jnp.dot(e.astype(BF16), vcat[:, h * V_DIM:(h + 1) * V_DIM],
                     preferred_element_type=F32) / jnp.sum(e, axis=-1, keepdims=True)
        o_heads.append(o2[0:t_new] - lam * o2[t_new:2 * t_new])
    rows = pl.ds(pl.multiple_of(b * t_new, t_new), t_new)
    mix_ref[rows, :] = _gated_attn(o_heads, sga_ref[b + first], subg_ref[...])

    @pl.when(b == nb - 1)
    def _():
        n_rows = mix_ref.shape[0]
        oc = oc_ref[first:first + n_rows // t_new]
        base = _conv_branch_out(x_ref[...].reshape(n_rows, x_ref.shape[-1]),
                                oc.reshape(n_rows, WIDTH), wout_ref)
        y_ref[...] = _attn_branch_out(base, mix_ref[...], wout_ref).reshape(y_ref.shape)


def _sample_attn(x, q, kn, vn, ck, cv, km_bf, vm, sga, oc, subg, lq1, lk1, lq2, lk2, w_out_bf, *,
                 first):
    nb, t_new, d = x.shape
    n_proj = q.shape[0]
    past = ck.shape[1] // N_HEADS
    n_keys_pad = -(-(past + N_META + t_new) // V_DIM) * V_DIM
    row = lambda b: (b, 0, 0)
    proj_row = lambda b: (b + first, 0, 0)
    const2 = lambda b: (0, 0)
    const3 = lambda b: (0, 0, 0)
    in_specs = [
        pl.BlockSpec((nb, t_new, d), const3),
        pl.BlockSpec((1, t_new, WIDTH), proj_row),
        pl.BlockSpec((1, t_new * N_HEADS, V_DIM), proj_row),
        pl.BlockSpec((1, t_new * N_HEADS, V_DIM), proj_row),
        pl.BlockSpec((1, past * N_HEADS, V_DIM), row),
        pl.BlockSpec((1, past * N_HEADS, V_DIM), row),
        pl.BlockSpec((N_META, WIDTH), const2),
        pl.BlockSpec((N_META * N_HEADS, V_DIM), const2),
        pl.BlockSpec((n_proj, t_new, WIDTH), const3),
        pl.BlockSpec((n_proj, t_new, WIDTH), const3),
        pl.BlockSpec((1, V_DIM), const2),
        pl.BlockSpec((1, HEAD_DIM), const2),
        pl.BlockSpec((1, HEAD_DIM), const2),
        pl.BlockSpec((1, HEAD_DIM), const2),
        pl.BlockSpec((1, HEAD_DIM), const2),
        pl.BlockSpec(w_out_bf.shape, const2),
    ]
    return pl.pallas_call(
        functools.partial(_sample_attn_kernel, past=past, t_new=t_new, n_keys_pad=n_keys_pad,
                          first=first),
        grid=(nb,),
        in_specs=in_specs,
        out_specs=pl.BlockSpec((nb, t_new, d), const3),
        out_shape=jax.ShapeDtypeStruct((nb, t_new, d), F32),
        scratch_shapes=[pltpu.VMEM((n_keys_pad, WIDTH), BF16),
                        pltpu.VMEM((n_keys_pad, WIDTH), BF16),
                        pltpu.VMEM((nb * t_new, WIDTH), BF16)],
        compiler_params=pltpu.CompilerParams(
            dimension_semantics=("arbitrary",),
            vmem_limit_bytes=V7X_VMEM_LIMIT),
        name="sample_attn",
    )(x, q, kn, vn, ck, cv, km_bf, vm, sga, oc, subg, lq1, lk1, lq2, lk2, w_out_bf)


def _rope_tables(pos):
    half = ROT_DIM // 2
    inv = ROPE_THETA ** (-jnp.arange(0, ROT_DIM, 2, dtype=F32) / ROT_DIM)
    ang = pos.astype(F32)[:, None] * inv[None, :]
    cos, sin = lax.optimization_barrier((jnp.cos(ang), jnp.sin(ang)))
    t = pos.shape[0]
    rest = HEAD_DIM - 2 * half
    c64 = jnp.concatenate([cos, cos, jnp.ones((t, rest), F32)], axis=1)
    s64 = jnp.concatenate([-sin, sin, jnp.zeros((t, rest), F32)], axis=1)
    return jnp.tile(c64, (1, 2)), jnp.tile(s64, (1, 2))


def _score_bound(qg, kg):
    bound = Q_SCALE * HEAD_DIM * jnp.max(jnp.abs(qg)) * jnp.max(jnp.abs(kg)) * (1.0 + 2.0 ** -6)
    return bound.reshape(1).astype(F32)


def _pad_state(state):
    return jnp.pad(state, ((0, 0), (STATE_ROWS - (CONV_W - 1), 0), (0, 0)))


def _tile_sizes(seq):
    kb = KEY_BLOCK if seq % KEY_BLOCK == 0 else seq
    tq = QUERY_TILE if seq % QUERY_TILE == 0 else kb
    return (2 * tq if seq % (2 * tq) == 0 else tq), tq, kb


def kernel(x_prompt, x_sample, cache_k, cache_v, state_conv, meta_tokens, ln_g, w_in, q_norm_g,
           k_norm_g, lam_q1, lam_k1, lam_q2, lam_k2, subln_g, conv_w, conv_b, w_out):
    assert ln_g.shape[0] == 1, "single-layer stack"
    nb, seq, d = x_prompt.shape
    db, t_new, _ = x_sample.shape
    past = cache_k.shape[2]

    w_in_bf = w_in[0].astype(BF16)
    w_out_bf = w_out[0].astype(BF16)
    lng = ln_g[0][None, :]
    qg = jnp.tile(q_norm_g[0], 2 * N_HEADS)[None, :]
    kg = jnp.tile(k_norm_g[0], 2 * N_HEADS)[None, :]
    subg = subln_g[0][None, :]
    cw, cb = conv_w[0], conv_b[0][None, :]
    lams = [p[0][None, :] for p in (lam_q1, lam_k1, lam_q2, lam_k2)]
    grp = jnp.arange(GROUP_SUM_WIDTH, dtype=jnp.int32) // HEAD_DIM
    gmat = (grp[:, None] == grp[None, :]).astype(BF16)

    pos_m = jnp.arange(N_META, dtype=jnp.int32)
    pos_p = N_META + jnp.arange(seq, dtype=jnp.int32)
    pos_s = N_META + past + jnp.arange(t_new, dtype=jnp.int32)
    weights = (lng, w_in_bf, qg, kg, cw, cb, gmat)
    tm, tq, kb = _tile_sizes(seq)

    assert t_new == N_META, "meta tokens and new frames are projected as equal-length streams"
    cos_m, sin_m = _rope_tables(pos_m)
    cos_s, sin_s = _rope_tables(pos_s)
    n_small = 1 + db
    q_sm, k_sm, v_sm, kb_sm, sga_sm, oc_sm, tail_sm = _proj_call(
        jnp.concatenate([meta_tokens[None], x_sample], axis=0),
        jnp.concatenate([jnp.zeros((1, STATE_ROWS, WIDTH), F32), _pad_state(state_conv[0])], axis=0),
        jnp.concatenate([cos_m] + [cos_s] * db, axis=0),
        jnp.concatenate([sin_m] + [sin_s] * db, axis=0),
        *weights, sb=n_small, tm=t_new, tkv=t_new, emit_vt=False)
    km, vm, km_bf, tail_m = k_sm[0], v_sm[0], kb_sm[0], tail_sm[0:1]
    ks, vs, tail_s = k_sm[1:], v_sm[1:], tail_sm[1:]

    state_p = jnp.broadcast_to(tail_m, (nb, STATE_ROWS, WIDTH))
    qp, kp, vp, kp_bf, sga_p, oc_p, tail_p, vtp = _proj_call(
        x_prompt, state_p, *_rope_tables(pos_p), *weights, sb=1, tm=tm, tkv=kb, emit_vt=True,
        row_offset=N_META)
    attn = lambda bounded: functools.partial(_prompt_attn, tq=tq, kb=kb, big=BIG_KEY_BLOCKS,
                                             bounded=bounded)
    y_prompt = lax.cond(
        _score_bound(q_norm_g[0], k_norm_g[0])[0] <= SAFE_SCORE_BOUND, attn(True), attn(False),
        x_prompt, qp, kp_bf.reshape(nb, seq // kb, kb, WIDTH), vtp, km_bf,
        vm.reshape(N_META, WIDTH).T.astype(BF16), sga_p, oc_p, subg, *lams, w_out_bf)
    kp = kp.at[:, :N_META * N_HEADS].set(jnp.broadcast_to(km[None], (nb,) + km.shape))
    vp = vp.at[:, :N_META * N_HEADS].set(jnp.broadcast_to(vm[None], (nb,) + vm.shape))

    y_sample = _sample_attn(
        x_sample, q_sm, k_sm, v_sm, cache_k[0].reshape(db, past * N_HEADS, V_DIM),
        cache_v[0].reshape(db, past * N_HEADS, V_DIM), km_bf, vm, sga_sm, oc_sm, subg, *lams,
        w_out_bf, first=1)

    def heads_form(a, rows):
        return a.reshape(1, a.shape[0], rows, N_HEADS, V_DIM)

    tail_rows = slice(STATE_ROWS - (CONV_W - 1), STATE_ROWS)
    return (y_prompt, y_sample, heads_form(kp, N_META + seq), heads_form(vp, N_META + seq),
            tail_p[None, :, tail_rows], heads_form(ks, t_new), heads_form(vs, t_new),
            tail_s[None, :, tail_rows])
```
